```python
import jax, jax.numpy as jnp
from jax import lax
import numpy as np

D_MODEL = 1024
BATCH = 8
SEQ = 4096
DEPTH = 2

GRID_W = 64
CTX_LEN = 256
D_RNN = 1024
RNN_BLOCKS = 8
RNN_BLOCK_W = D_RNN // RNN_BLOCKS
RNN_CONV_W = 4
RNN_PAD_L = 2
RNN_PAD_R = 1
RG_C = 8.0
ATT_HEADS = 8
ATT_KV_HEADS = 2
ATT_HEAD_DIM = 128
ATT_GROUP = ATT_HEADS // ATT_KV_HEADS
Q_BLOCK = 128
ROPE_THETA = 10000.0
RET_HEADS = 4
RET_QK_DIM = 256
RET_V_DIM = 256
RET_CHUNK = 128
N_BRANCH = 3
BRANCH_W = 1024
D_FF = 2816
FFN_CONV_W = 3
FFN_PAD = 1
EPS = 1e-6

IN_WIDTHS = (D_RNN, D_RNN, ATT_HEADS * ATT_HEAD_DIM, ATT_KV_HEADS * ATT_HEAD_DIM, ATT_KV_HEADS * ATT_HEAD_DIM,
             RET_HEADS * RET_QK_DIM, RET_HEADS * RET_QK_DIM, RET_HEADS * RET_V_DIM, RET_HEADS * RET_V_DIM,
             N_BRANCH * D_MODEL)
IN_COLS = sum(IN_WIDTHS)

kernel_name = "hybrid_rglru_gqa_retention_dit"


def rmsnorm(x, w):
    xf = x.astype(jnp.float32)
    y = xf * lax.rsqrt(jnp.mean(xf * xf, axis=-1, keepdims=True) + EPS)
    return (y * w.astype(jnp.float32)).astype(x.dtype)


def modulate(u, shift, scale):
    return u * (1 + scale[:, None, :]) + shift[:, None, :]


def dwconv(x, w, b, pad_left, pad_right):
    out = lax.conv_general_dilated(x, w[:, None, :].astype(x.dtype), window_strides=(1,),
                                   padding=[(pad_left, pad_right)],
                                   dimension_numbers=("NWC", "WIO", "NWC"),
                                   feature_group_count=x.shape[-1])
    return out + b.astype(x.dtype)


def split_in(p):
    bounds = [int(v) for v in np.cumsum(IN_WIDTHS)[:-1]]
    return jnp.split(p, bounds, axis=-1)


def heads(t, n, d):
    return t.reshape(t.shape[0], t.shape[1], n, d)


def rope_tables(row, col, head_dim):
    n_freq = head_dim // 4
    inv = ROPE_THETA ** (-jnp.arange(n_freq, dtype=jnp.float32) / n_freq)
    ang = jnp.stack([row[:, None] * inv, col[:, None] * inv], axis=1)
    return jnp.cos(ang), jnp.sin(ang)


def apply_rope(x, cos, sin):
    Bn, L, H, hd = x.shape
    xr = x.astype(jnp.float32).reshape(Bn, L, H, 2, 2, hd // 4)
    x1, x2 = xr[..., 0, :], xr[..., 1, :]
    cc = cos[None, :, None]
    ss = sin[None, :, None]
    out = jnp.stack([x1 * cc - x2 * ss, x2 * cc + x1 * ss], axis=-2)
    return out.reshape(Bn, L, H, hd).astype(x.dtype)


def linear_scan(a, b, h0):
    def combine(l, r):
        return l[0] * r[0], r[0] * l[1] + r[1]
    a_cum, b_cum = lax.associative_scan(combine, (a, b), axis=1)
    return a_cum * h0[:, None, :] + b_cum


def rglru_dir(x, gate_w, gate_b, lam, h0):
    Bn, L, _ = x.shape
    xb = x.reshape(Bn, L, RNN_BLOCKS, RNN_BLOCK_W)
    g = jnp.einsum("blnc,gncd->gblnd", xb, gate_w.astype(jnp.float32)).reshape(2, Bn, L, D_RNN)
    g = g + gate_b.astype(jnp.float32)[:, None, None, :]
    r = jax.nn.sigmoid(g[0])
    i = jax.nn.sigmoid(g[1])
    log_a = -RG_C * r * jax.nn.softplus(-lam.astype(jnp.float32))
    a = jnp.exp(log_a)
    b = jnp.sqrt(-jnp.expm1(2.0 * log_a)) * (i * x)
    h = linear_scan(a, b, h0)
    return h, h[:, -1]


def rglru_bidir(x, gw, gb, lam, h0f, h0b):
    hf, lf = rglru_dir(x, gw[0], gb[0], lam[0], h0f)
    hb, lb = rglru_dir(x[:, ::-1], gw[1], gb[1], lam[1], h0b)
    return hf + hb[:, ::-1], lf, lb


def retention_dir(q, k, v, log_g, s0, include_diag):
    Bn, L, H, _ = q.shape
    dv = v.shape[-1]
    n = L // RET_CHUNK
    pos = jnp.arange(RET_CHUNK, dtype=jnp.float32)
    diff = pos[:, None] - pos[None, :]
    mask = diff >= 0 if include_diag else diff > 0
    intra = jnp.where(mask[None], jnp.exp(jnp.where(mask, diff, 0.0)[None] * log_g[:, None, None]), 0.0)
    q_dec = jnp.exp((pos + 1.0)[:, None] * log_g[None, :])
    k_dec = jnp.exp((RET_CHUNK - 1.0 - pos)[:, None] * log_g[None, :])
    c_dec = jnp.exp(RET_CHUNK * log_g)

    def to_chunks(t):
        return jnp.moveaxis(t.reshape(Bn, n, RET_CHUNK, H, t.shape[-1]), 1, 0)

    def step(s, qkv):
        qc, kc, vc = qkv
        att = jnp.einsum("bihd,bjhd->bhij", qc, kc) * intra[None]
        o = jnp.einsum("bhij,bjhv->bihv", att, vc) + jnp.einsum("bihd,bhdv->bihv", qc * q_dec[None, :, :, None], s)
        s = s * c_dec[None, :, None, None] + jnp.einsum("bjhd,bjhv->bhdv", kc * k_dec[None, :, :, None], vc)
        return s, o

    s_last, o = lax.scan(step, s0, (to_chunks(q), to_chunks(k), to_chunks(v)))
    return jnp.moveaxis(o, 0, 1).reshape(Bn, L, H, dv), s_last


def retention_bidir(q, k, v, log_g, s0f, s0b):
    of, sf = retention_dir(q, k, v, log_g[0], s0f, True)
    ob, sb = retention_dir(q[:, ::-1], k[:, ::-1], v[:, ::-1], log_g[1], s0b, False)
    return of + ob[:, ::-1], sf, sb


def retention_out(o, gate, norm_w):
    Bn, L, H, dv = o.shape
    mu = jnp.mean(o, axis=-1, keepdims=True)
    var = jnp.mean(jnp.square(o - mu), axis=-1, keepdims=True)
    y = ((o - mu) * lax.rsqrt(var + EPS)).reshape(Bn, L, H * dv) * norm_w.astype(jnp.float32)
    return jax.nn.silu(gate) * y.astype(gate.dtype)


def attend_block(qb, k, v):
    s = jnp.einsum("bqkgd,bnkd->bkgqn", qb, k, preferred_element_type=jnp.float32) * (ATT_HEAD_DIM ** -0.5)
    p = jax.nn.softmax(s, axis=-1).astype(v.dtype)
    return jnp.einsum("bkgqn,bnkd->bqkgd", p, v)


def latent_attention(q, k_all, v_all):
    Bn, L = q.shape[:2]
    nblk = L // Q_BLOCK
    qb = jnp.moveaxis(q.reshape(Bn, nblk, Q_BLOCK, ATT_KV_HEADS, ATT_GROUP, ATT_HEAD_DIM), 1, 0)
    o = lax.map(lambda blk: attend_block(blk, k_all, v_all), qb)
    return jnp.moveaxis(o, 0, 1).reshape(Bn, L, ATT_HEADS * ATT_HEAD_DIM)


def merge(bg, y_rnn, y_att, y_ret, w_branch, w_out):
    g = jax.nn.sigmoid(bg).reshape(bg.shape[0], bg.shape[1], N_BRANCH, D_MODEL)
    m = (g[..., 0, :] * (y_rnn @ w_branch[0]) + g[..., 1, :] * (y_att @ w_branch[1])
         + g[..., 2, :] * (y_ret @ w_branch[2]))
    return m @ w_out


def token_mixer(u_lat, u_ctx, rope_att, rope_ret, w_in, rnn_conv_w, rnn_conv_b, rglru_w, rglru_b, rglru_lam,
                q_norm_w, k_norm_w, log_g, ret_norm_w, w_branch, w_out, need_ctx_out):
    f32 = jnp.float32
    Bn, L, _ = u_lat.shape
    Lc = u_ctx.shape[1]
    rx_l, rg_l, aq_l, ak_l, av_l, rq_l, rk_l, rv_l, rgt_l, bg_l = split_in(u_lat @ w_in)
    rx_c, rg_c, aq_c, ak_c, av_c, rq_c, rk_c, rv_c, rgt_c, bg_c = split_in(u_ctx @ w_in)

    xr_c = dwconv(rx_c, rnn_conv_w, rnn_conv_b, RNN_PAD_L, RNN_PAD_R).astype(f32)
    xr_l = dwconv(rx_l, rnn_conv_w, rnn_conv_b, RNN_PAD_L, RNN_PAD_R).astype(f32)
    h0 = jnp.zeros((Bn, D_RNN), f32)
    hc, hcf, hcb = rglru_bidir(xr_c, rglru_w, rglru_b, rglru_lam, h0, h0)
    hl, _, _ = rglru_bidir(xr_l, rglru_w, rglru_b, rglru_lam, hcf, hcb)
    y_rnn_l = jax.nn.gelu(rg_l) * hl.astype(rg_l.dtype)

    q_l = apply_rope(rmsnorm(heads(aq_l, ATT_HEADS, ATT_HEAD_DIM), q_norm_w), *rope_att)
    k_l = apply_rope(rmsnorm(heads(ak_l, ATT_KV_HEADS, ATT_HEAD_DIM), k_norm_w), *rope_att)
    v_l = heads(av_l, ATT_KV_HEADS, ATT_HEAD_DIM)
    q_c = rmsnorm(heads(aq_c, ATT_HEADS, ATT_HEAD_DIM), q_norm_w)
    k_c = rmsnorm(heads(ak_c, ATT_KV_HEADS, ATT_HEAD_DIM), k_norm_w)
    v_c = heads(av_c, ATT_KV_HEADS, ATT_HEAD_DIM)
    k_all = jnp.concatenate([k_c, k_l], axis=1)
    v_all = jnp.concatenate([v_c, v_l], axis=1)
    y_att_l = latent_attention(q_l, k_all, v_all)

    def ret_qkv(rq, rk, rv, rope):
        q = heads(rq, RET_HEADS, RET_QK_DIM)
        k = heads(rk, RET_HEADS, RET_QK_DIM)
        if rope is not None:
            q = apply_rope(q, *rope)
            k = apply_rope(k, *rope)
        return q.astype(f32), k.astype(f32) * (RET_QK_DIM ** -0.5), heads(rv, RET_HEADS, RET_V_DIM).astype(f32)

    s0 = jnp.zeros((Bn, RET_HEADS, RET_QK_DIM, RET_V_DIM), f32)
    o_c, scf, scb = retention_bidir(*ret_qkv(rq_c, rk_c, rv_c, None), log_g, s0, s0)
    o_l, _, _ = retention_bidir(*ret_qkv(rq_l, rk_l, rv_l, rope_ret), log_g, scf, scb)
    y_ret_l = retention_out(o_l, rgt_l, ret_norm_w)

    out_l = merge(bg_l, y_rnn_l, y_att_l, y_ret_l, w_branch, w_out)
    if not need_ctx_out:
        return out_l, None
    y_rnn_c = jax.nn.gelu(rg_c) * hc.astype(rg_c.dtype)
    y_att_c = attend_block(q_c.reshape(Bn, Lc, ATT_KV_HEADS, ATT_GROUP, ATT_HEAD_DIM), k_c, v_c)
    y_att_c = y_att_c.reshape(Bn, Lc, ATT_HEADS * ATT_HEAD_DIM)
    y_ret_c = retention_out(o_c, rgt_c, ret_norm_w)
    out_c = merge(bg_c, y_rnn_c, y_att_c, y_ret_c, w_branch, w_out)
    return out_l, out_c


def channel_mixer(u, up, conv_w, conv_b, down):
    a, b = jnp.split(u @ up, 2, axis=-1)
    a = dwconv(a, conv_w, conv_b, FFN_PAD, FFN_PAD)
    return (jax.nn.silu(a) * b) @ down


def setup_inputs(seed: int = 0) -> dict:
    key = jax.random.key(seed)
    ks = jax.random.split(key, 32)
    f32 = jnp.float32

    def nrm(k, shape, scale):
        return jax.random.normal(k, shape, f32) * scale

    x = nrm(ks[0], (BATCH, SEQ, D_MODEL), 1.0)
    c = nrm(ks[1], (BATCH, D_MODEL), 1.0)
    ctx = nrm(ks[2], (BATCH, CTX_LEN, D_MODEL), 1.0)
    c_ctx = nrm(ks[3], (D_MODEL,), 1.0)
    mod_w = nrm(ks[4], (DEPTH, D_MODEL, 6 * D_MODEL), 0.5 * D_MODEL ** -0.5)
    mod_b = nrm(ks[5], (DEPTH, 6 * D_MODEL), 0.01)
    norm1_w = 1.0 + nrm(ks[6], (DEPTH, D_MODEL), 0.02)
    norm2_w = 1.0 + nrm(ks[7], (DEPTH, D_MODEL), 0.02)
    w_in = nrm(ks[8], (DEPTH, D_MODEL, IN_COLS), D_MODEL ** -0.5)
    rnn_conv_w = nrm(ks[9], (DEPTH, RNN_CONV_W, D_RNN), RNN_CONV_W ** -0.5)
    rnn_conv_b = nrm(ks[10], (DEPTH, D_RNN), 0.01)
    rglru_w = nrm(ks[11], (DEPTH, 2, 2, RNN_BLOCKS, RNN_BLOCK_W, RNN_BLOCK_W), RNN_BLOCK_W ** -0.5)
    rglru_b = nrm(ks[12], (DEPTH, 2, 2, D_RNN), 0.01)
    a_c = jax.random.uniform(ks[13], (DEPTH, 2, D_RNN), f32, 0.9, 0.999)
    a0 = a_c ** (1.0 / RG_C)
    rglru_lam = jnp.log(a0) - jnp.log1p(-a0)
    q_norm_w = 1.0 + nrm(ks[14], (DEPTH, ATT_HEAD_DIM), 0.02)
    k_norm_w = 1.0 + nrm(ks[15], (DEPTH, ATT_HEAD_DIM), 0.02)
    gamma = 1.0 - 2.0 ** (-5.0 - jnp.arange(RET_HEADS, dtype=f32))
    ret_decay = (jnp.log(gamma) - jnp.log1p(-gamma))[None, None, :] + nrm(ks[16], (DEPTH, 2, RET_HEADS), 0.01)
    ret_norm_w = 1.0 + nrm(ks[17], (DEPTH, RET_HEADS * RET_V_DIM), 0.02)
    w_branch = nrm(ks[18], (DEPTH, N_BRANCH, BRANCH_W, D_MODEL), BRANCH_W ** -0.5)
    w_out = nrm(ks[19], (DEPTH, D_MODEL, D_MODEL), D_MODEL ** -0.5)
    ffn_up = nrm(ks[20], (DEPTH, D_MODEL, 2 * D_FF), D_MODEL ** -0.5)
    ffn_conv_w = nrm(ks[21], (DEPTH, FFN_CONV_W, D_FF), FFN_CONV_W ** -0.5)
    ffn_conv_b = nrm(ks[22], (DEPTH, D_FF), 0.01)
    ffn_down = nrm(ks[23], (DEPTH, D_FF, D_MODEL), D_FF ** -0.5)
    final_norm_w = 1.0 + nrm(ks[24], (D_MODEL,), 0.02)
    return {"x": x, "c": c, "ctx": ctx, "c_ctx": c_ctx, "mod_w": mod_w, "mod_b": mod_b,
            "norm1_w": norm1_w, "norm2_w": norm2_w, "w_in": w_in, "rnn_conv_w": rnn_conv_w,
            "rnn_conv_b": rnn_conv_b, "rglru_w": rglru_w, "rglru_b": rglru_b, "rglru_lam": rglru_lam,
            "q_norm_w": q_norm_w, "k_norm_w": k_norm_w, "ret_decay": ret_decay, "ret_norm_w": ret_norm_w,
            "w_branch": w_branch, "w_out": w_out, "ffn_up": ffn_up, "ffn_conv_w": ffn_conv_w,
            "ffn_conv_b": ffn_conv_b, "ffn_down": ffn_down, "final_norm_w": final_norm_w}


def reference(x, c, ctx, c_ctx, mod_w, mod_b, norm1_w, norm2_w, w_in, rnn_conv_w, rnn_conv_b, rglru_w, rglru_b,
              rglru_lam, q_norm_w, k_norm_w, ret_decay, ret_norm_w, w_branch, w_out, ffn_up, ffn_conv_w,
              ffn_conv_b, ffn_down, final_norm_w):
    L = x.shape[1]
    ROWS = L // GRID_W
    row_idx = jnp.repeat(jnp.arange(ROWS, dtype=jnp.float32), GRID_W)
    col_idx = jnp.tile(jnp.arange(GRID_W, dtype=jnp.float32), ROWS)
    rope_att = rope_tables(row_idx, col_idx, ATT_HEAD_DIM)
    rope_ret = rope_tables(row_idx, col_idx, RET_QK_DIM)
    sc_lat = jax.nn.silu(c)
    sc_ctx = jax.nn.silu(c_ctx)[None]
    for l in range(DEPTH):
        last = l == DEPTH - 1
        m_lat = jnp.split(sc_lat @ mod_w[l] + mod_b[l], 6, axis=-1)
        m_ctx = jnp.split(sc_ctx @ mod_w[l] + mod_b[l], 6, axis=-1)
        u_lat = modulate(rmsnorm(x, norm1_w[l]), m_lat[0], m_lat[1])
        u_ctx = modulate(rmsnorm(ctx, norm1_w[l]), m_ctx[0], m_ctx[1])
        log_g = jax.nn.log_sigmoid(ret_decay[l].astype(jnp.float32))
        y_lat, y_ctx = token_mixer(u_lat, u_ctx, rope_att, rope_ret, w_in[l], rnn_conv_w[l], rnn_conv_b[l],
                                   rglru_w[l], rglru_b[l], rglru_lam[l], q_norm_w[l], k_norm_w[l], log_g,
                                   ret_norm_w[l], w_branch[l], w_out[l], not last)
        x = x + m_lat[2][:, None, :] * y_lat
        v_lat = modulate(rmsnorm(x, norm2_w[l]), m_lat[3], m_lat[4])
        x = x + m_lat[5][:, None, :] * channel_mixer(v_lat, ffn_up[l], ffn_conv_w[l], ffn_conv_b[l], ffn_down[l])
        if not last:
            ctx = ctx + m_ctx[2][:, None, :] * y_ctx
            v_ctx = modulate(rmsnorm(ctx, norm2_w[l]), m_ctx[3], m_ctx[4])
            ctx = ctx + m_ctx[5][:, None, :] * channel_mixer(v_ctx, ffn_up[l], ffn_conv_w[l], ffn_conv_b[l],
                                                              ffn_down[l])
    return rmsnorm(x, final_norm_w)
```

```python
import functools

import numpy as np
import jax
import jax.numpy as jnp
from jax import lax
from jax.experimental import pallas as pl
from jax.experimental.pallas import tpu as pltpu

F32 = jnp.float32
BF16 = jnp.bfloat16

D_MODEL = 1024
GRID_W = 64
D_RNN = 1024
RNN_BLOCKS = 8
RNN_BLOCK_W = D_RNN // RNN_BLOCKS
RNN_CONV_W = 4
RNN_PAD_L = 2
RG_C = 8.0
ATT_HEADS = 8
ATT_KV_HEADS = 2
ATT_HEAD_DIM = 128
ATT_GROUP = ATT_HEADS // ATT_KV_HEADS
ROPE_THETA = 10000.0
RET_HEADS = 4
RET_QK_DIM = 256
RET_V_DIM = 256
RET_CHUNK = 128
N_BRANCH = 3
D_FF = 2816
FFN_CONV_W = 3
EPS = 1e-6

COL_RX = 0
COL_RG = 1024
COL_AQ = 2048
COL_RQ = 3072
COL_RK = 4096
COL_RV = 5120
COL_RGT = 6144
COL_BG = 7168
COL_AK = 10240
COL_AV = 10496
IN_COLS = 10752
_REF_SEGMENTS = ((0, 1024), (1024, 1024), (2048, 1024), (3584, 1024), (4608, 1024), (5632, 1024),
                 (6656, 1024), (7680, 3072), (3072, 256), (3328, 256))

SUBLANES = 8
MOD_ROWS = 16
VMEM_LIMIT = 52 * 1024 * 1024


def _cparams(n_grid):
    return pltpu.CompilerParams(dimension_semantics=("arbitrary",) * n_grid, vmem_limit_bytes=VMEM_LIMIT)


def _sigmoid(x):
    return 1.0 / (1.0 + jnp.exp(-x))


def _silu(x):
    return x * _sigmoid(x)


def _gelu_tanh(x):
    return x * (0.5 * (1.0 + jnp.tanh(np.sqrt(2.0 / np.pi).astype(np.float32) * (x + 0.044715 * (x * x * x)))))


def _rms_rows(x):
    return x * lax.rsqrt(jnp.mean(x * x, axis=-1, keepdims=True) + EPS)


def _rope(x, cos, sin_signed, half):
    n = x.shape[-1]
    from_lo = pltpu.roll(x, half, axis=1)
    from_hi = pltpu.roll(x, n - half, axis=1)
    lane = lax.broadcasted_iota(jnp.int32, x.shape, 1)
    partner = jnp.where((lane & half) != 0, from_lo, from_hi)
    return x * cos + partner * sin_signed


def _mod_kernel(c_ref, w_ref, b_ref, o_ref):
    sc = _silu(c_ref[...])
    o_ref[...] = jnp.dot(sc.astype(BF16), w_ref[...].astype(BF16), preferred_element_type=F32) + b_ref[...]


def _modulation(c_all, mod_w, mod_b):
    depth, d, n = mod_w.shape
    tn = 1536
    return pl.pallas_call(
        _mod_kernel,
        grid=(depth, n // tn),
        in_specs=[pl.BlockSpec((MOD_ROWS, d), lambda l, j: (0, 0)),
                  pl.BlockSpec((None, d, tn), lambda l, j: (l, 0, j)),
                  pl.BlockSpec((None, 1, tn), lambda l, j: (l, 0, j))],
        out_specs=pl.BlockSpec((None, MOD_ROWS, tn), lambda l, j: (l, 0, j)),
        out_shape=jax.ShapeDtypeStruct((depth, MOD_ROWS, n), F32),
        compiler_params=_cparams(2),
        name="modulation",
    )(c_all, mod_w, mod_b.reshape(depth, 1, n))


def _inproj_kernel(x_ref, m_ref, nw_ref, w_ref, o_ref, u_ref):
    @pl.when(pl.program_id(1) == 0)
    def _():
        y = _rms_rows(x_ref[...]) * nw_ref[...]
        u_ref[...] = (y * (1.0 + m_ref[1:2, :]) + m_ref[0:1, :]).astype(BF16)

    o_ref[...] = jnp.dot(u_ref[...], w_ref[...], preferred_element_type=F32).astype(o_ref.dtype)


def _inproj(x2d, mods, mod_row, norm_w, w, tm):
    n_tok, d = x2d.shape
    n = w.shape[1]
    tn = 1536
    return pl.pallas_call(
        _inproj_kernel,
        grid=(n_tok // tm, n // tn),
        in_specs=[pl.BlockSpec((tm, d), lambda i, j: (i, 0)),
                  pl.BlockSpec((None, 6, d), lambda i, j: (mod_row(i), 0, 0)),
                  pl.BlockSpec((1, d), lambda i, j: (0, 0)),
                  pl.BlockSpec((d, tn), lambda i, j: (0, j))],
        out_specs=pl.BlockSpec((tm, tn), lambda i, j: (i, j)),
        out_shape=jax.ShapeDtypeStruct((n_tok, n), BF16),
        scratch_shapes=[pltpu.VMEM((tm, d), BF16)],
        compiler_params=_cparams(2),
        name="inproj",
    )(x2d, mods, norm_w.reshape(1, d), w)


RNN_CHUNK = 256


def _tile_scan(a, b, reverse):
    n = a.shape[0]
    r8 = lax.broadcasted_iota(jnp.int32, a.shape, 0) & (SUBLANES - 1)
    for s in (1, 2, 4):
        shift = n - s if reverse else s
        a_sh = pltpu.roll(a, shift, axis=0)
        b_sh = pltpu.roll(b, shift, axis=0)
        valid = (r8 < SUBLANES - s) if reverse else (r8 >= s)
        b = jnp.where(valid, a * b_sh + b, b)
        a = jnp.where(valid, a * a_sh, a)
    return a, b


def _rglru_kernel(xc_ref, xl_ref, gc_ref, gl_ref, cw_ref, cb_ref, gw_ref, gb_ref, lam_ref,
                  yc_ref, yl_ref, xs, af, hf, ab, hb, *, lc, ll):
    w = xs.shape[1]
    tc = RNN_CHUNK
    pad = SUBLANES
    zero_pad = jnp.zeros((pad, w), F32)
    c_base = pad
    l_base = lc + 3 * pad
    xs[0:pad, :] = zero_pad
    xs[c_base + lc:c_base + lc + 2 * pad, :] = jnp.zeros((2 * pad, w), F32)
    xs[l_base + ll:l_base + ll + pad, :] = zero_pad
    xs[c_base:c_base + lc, :] = xc_ref[...].astype(F32)
    xs[l_base:l_base + ll, :] = xl_ref[...].astype(F32)

    lam = lam_ref[...]
    sp = jnp.maximum(-lam, 0.0) + jnp.log(1.0 + jnp.exp(-jnp.abs(lam)))
    sp_f = sp[0:1, :]
    sp_b = sp[1:2, :]
    cw = cw_ref[...]
    cb = cb_ref[...]
    gw = gw_ref[...]
    gb = gb_ref[...]

    def gates_chunk(xs_base, out_base, t0):
        win = xs[pl.ds(pl.multiple_of(xs_base + t0 - pad, SUBLANES), tc + 2 * pad), :]
        nwin = tc + 2 * pad
        x = cb + jnp.zeros((tc, w), F32)
        for k in range(RNN_CONV_W):
            d = k - RNN_PAD_L
            sh = win if d == 0 else pltpu.roll(win, (-d) % nwin, axis=0)
            x = x + cw[k:k + 1, :] * sh[pad:pad + tc, :]
        g = jnp.dot(x.astype(BF16), gw, preferred_element_type=F32) + gb
        rows = pl.ds(pl.multiple_of(out_base + t0, SUBLANES), tc)
        for direction, (a_ref, h_ref, spd) in enumerate(((af, hf, sp_f), (ab, hb, sp_b))):
            r = _sigmoid(g[:, (2 * direction) * w:(2 * direction + 1) * w])
            i = _sigmoid(g[:, (2 * direction + 1) * w:(2 * direction + 2) * w])
            log_a = (-RG_C) * r * spd
            a = jnp.exp(log_a)
            b = jnp.sqrt(1.0 - jnp.exp(2.0 * log_a)) * (i * x)
            a, b = _tile_scan(a, b, reverse=(direction == 1))
            a_ref[rows, :] = a
            h_ref[rows, :] = b

    def ctx_body(c, carry):
        gates_chunk(c_base, 0, c * tc)
        return carry

    def lat_body(c, carry):
        gates_chunk(l_base, lc, c * tc)
        return carry

    lax.fori_loop(0, lc // tc, ctx_body, 0)
    lax.fori_loop(0, ll // tc, lat_body, 0)

    n_ct = lc // SUBLANES
    n_t = (lc + ll) // SUBLANES

    def carry_body(j, carry):
        h_f, h_b = carry
        rf = pl.ds(pl.multiple_of(j * SUBLANES, SUBLANES), SUBLANES)
        jb = jnp.where(j < n_ct, n_ct - 1 - j, n_t - 1 - (j - n_ct))
        rb = pl.ds(pl.multiple_of(jb * SUBLANES, SUBLANES), SUBLANES)
        t_f = af[rf, :] * h_f + hf[rf, :]
        t_b = ab[rb, :] * h_b + hb[rb, :]
        hf[rf, :] = t_f
        hb[rb, :] = t_b
        return (jnp.broadcast_to(t_f[SUBLANES - 1:SUBLANES, :], (SUBLANES, w)),
                jnp.broadcast_to(t_b[0:1, :], (SUBLANES, w)))

    zero_h = jnp.zeros((SUBLANES, w), F32)
    lax.fori_loop(0, n_t, carry_body, (zero_h, zero_h), unroll=8)

    def out_ctx(c, carry):
        rows = pl.ds(pl.multiple_of(c * tc, SUBLANES), tc)
        yc_ref[rows, :] = (_gelu_tanh(gc_ref[rows, :].astype(F32)) * (hf[rows, :] + hb[rows, :])).astype(yc_ref.dtype)
        return carry

    def out_lat(c, carry):
        rows = pl.ds(pl.multiple_of(c * tc, SUBLANES), tc)
        srows = pl.ds(pl.multiple_of(lc + c * tc, SUBLANES), tc)
        yl_ref[rows, :] = (_gelu_tanh(gl_ref[rows, :].astype(F32)) * (hf[srows, :] + hb[srows, :])).astype(yl_ref.dtype)
        return carry

    lax.fori_loop(0, lc // tc, out_ctx, 0)
    lax.fori_loop(0, ll // tc, out_lat, 0)


def _rglru(p_ctx, p_lat, conv_w, conv_b, gate_w, gate_b, lam, n_batch, lc, ll):
    w = RNN_BLOCK_W
    nb = RNN_BLOCKS
    s = lc + ll
    rx0 = COL_RX // w
    rg0 = COL_RG // w
    kern = functools.partial(_rglru_kernel, lc=lc, ll=ll)
    return pl.pallas_call(
        kern,
        grid=(n_batch, nb),
        in_specs=[pl.BlockSpec((lc, w), lambda b, c: (b, rx0 + c)),
                  pl.BlockSpec((ll, w), lambda b, c: (b, rx0 + c)),
                  pl.BlockSpec((lc, w), lambda b, c: (b, rg0 + c)),
                  pl.BlockSpec((ll, w), lambda b, c: (b, rg0 + c)),
                  pl.BlockSpec((RNN_CONV_W, w), lambda b, c: (0, c)),
                  pl.BlockSpec((1, w), lambda b, c: (0, c)),
                  pl.BlockSpec((None, w, 4 * w), lambda b, c: (c, 0, 0)),
                  pl.BlockSpec((None, 1, 4 * w), lambda b, c: (c, 0, 0)),
                  pl.BlockSpec((2, w), lambda b, c: (0, c))],
        out_specs=[pl.BlockSpec((lc, w), lambda b, c: (b, c)),
                   pl.BlockSpec((ll, w), lambda b, c: (b, c))],
        out_shape=[jax.ShapeDtypeStruct((n_batch * lc, D_RNN), BF16),
                   jax.ShapeDtypeStruct((n_batch * ll, D_RNN), BF16)],
        scratch_shapes=[pltpu.VMEM((s + 4 * SUBLANES, w), F32),
                        pltpu.VMEM((s, w), F32), pltpu.VMEM((s, w), F32),
                        pltpu.VMEM((s, w), F32), pltpu.VMEM((s, w), F32)],
        compiler_params=_cparams(2),
        name="rglru",
    )(p_ctx, p_lat, p_ctx, p_lat, conv_w, conv_b, gate_w, gate_b, lam)


ATT_TQ = 256


def _attn_lat_kernel(q_ref, kc_ref, vc_ref, kl_ref, vl_ref, cq_ref, sq_ref, ck_ref, sk_ref, qw_ref, kw_ref,
                     o_ref, k_s, *, lc):
    hd = ATT_HEAD_DIM
    half = hd // 4

    @pl.when(pl.program_id(2) == 0)
    def _():
        kw = kw_ref[...]
        k_s[0:lc, :] = (_rms_rows(kc_ref[...].astype(F32)) * kw).astype(BF16)
        kl = _rms_rows(kl_ref[...].astype(F32)) * kw
        k_s[lc:, :] = _rope(kl, ck_ref[...], sk_ref[...], half).astype(BF16)

    qw = qw_ref[...] * (hd ** -0.5)
    cq = cq_ref[...]
    sq = sq_ref[...]
    k_all = k_s[...]
    v_c = vc_ref[...]
    v_l = vl_ref[...]
    for g in range(ATT_GROUP):
        q = _rms_rows(q_ref[:, g * hd:(g + 1) * hd].astype(F32)) * qw
        q = _rope(q, cq, sq, half).astype(BF16)
        s = lax.dot_general(q, k_all, (((1,), (1,)), ((), ())), preferred_element_type=F32)
        m = jnp.max(s, axis=-1, keepdims=True)
        p = jnp.exp(s - m)
        denom = jnp.sum(p, axis=-1, keepdims=True)
        pb = p.astype(BF16)
        o = (jnp.dot(pb[:, 0:lc], v_c, preferred_element_type=F32)
             + jnp.dot(pb[:, lc:], v_l, preferred_element_type=F32))
        o_ref[:, g * hd:(g + 1) * hd] = (o / denom).astype(o_ref.dtype)


def _attn_ctx_kernel(q_ref, kc_ref, vc_ref, qw_ref, kw_ref, o_ref):
    hd = ATT_HEAD_DIM
    k = (_rms_rows(kc_ref[...].astype(F32)) * kw_ref[...]).astype(BF16)
    qw = qw_ref[...] * (hd ** -0.5)
    v_c = vc_ref[...]
    for g in range(ATT_GROUP):
        q = (_rms_rows(q_ref[:, g * hd:(g + 1) * hd].astype(F32)) * qw).astype(BF16)
        s = lax.dot_general(q, k, (((1,), (1,)), ((), ())), preferred_element_type=F32)
        m = jnp.max(s, axis=-1, keepdims=True)
        p = jnp.exp(s - m)
        denom = jnp.sum(p, axis=-1, keepdims=True)
        o = jnp.dot(p.astype(BF16), v_c, preferred_element_type=F32)
        o_ref[:, g * hd:(g + 1) * hd] = (o / denom).astype(o_ref.dtype)


def _attn_lat(p_ctx, p_lat, cos, sin, q_norm_w, k_norm_w, n_batch, lc, ll):
    hd = ATT_HEAD_DIM
    gw = ATT_GROUP * hd
    tq = ATT_TQ
    nq = ll // tq
    q0 = COL_AQ // gw
    k0 = COL_AK // hd
    v0 = COL_AV // hd
    kern = functools.partial(_attn_lat_kernel, lc=lc)
    return pl.pallas_call(
        kern,
        grid=(n_batch, ATT_KV_HEADS, nq),
        in_specs=[pl.BlockSpec((tq, gw), lambda b, h, i: (b * nq + i, q0 + h)),
                  pl.BlockSpec((lc, hd), lambda b, h, i: (b, k0 + h)),
                  pl.BlockSpec((lc, hd), lambda b, h, i: (b, v0 + h)),
                  pl.BlockSpec((ll, hd), lambda b, h, i: (b, k0 + h)),
                  pl.BlockSpec((ll, hd), lambda b, h, i: (b, v0 + h)),
                  pl.BlockSpec((tq, hd), lambda b, h, i: (i, 0)),
                  pl.BlockSpec((tq, hd), lambda b, h, i: (i, 0)),
                  pl.BlockSpec((ll, hd), lambda b, h, i: (0, 0)),
                  pl.BlockSpec((ll, hd), lambda b, h, i: (0, 0)),
                  pl.BlockSpec((1, hd), lambda b, h, i: (0, 0)),
                  pl.BlockSpec((1, hd), lambda b, h, i: (0, 0))],
        out_specs=pl.BlockSpec((tq, gw), lambda b, h, i: (b * nq + i, h)),
        out_shape=jax.ShapeDtypeStruct((n_batch * ll, ATT_HEADS * hd), BF16),
        scratch_shapes=[pltpu.VMEM((lc + ll, hd), BF16)],
        compiler_params=_cparams(3),
        name="attn_lat",
    )(p_lat, p_ctx, p_ctx, p_lat, p_lat, cos, sin, cos, sin, q_norm_w, k_norm_w)


def _attn_ctx(p_ctx, q_norm_w, k_norm_w, n_batch, lc):
    hd = ATT_HEAD_DIM
    gw = ATT_GROUP * hd
    q0 = COL_AQ // gw
    k0 = COL_AK // hd
    v0 = COL_AV // hd
    return pl.pallas_call(
        _attn_ctx_kernel,
        grid=(n_batch, ATT_KV_HEADS),
        in_specs=[pl.BlockSpec((lc, gw), lambda b, h: (b, q0 + h)),
                  pl.BlockSpec((lc, hd), lambda b, h: (b, k0 + h)),
                  pl.BlockSpec((lc, hd), lambda b, h: (b, v0 + h)),
                  pl.BlockSpec((1, hd), lambda b, h: (0, 0)),
                  pl.BlockSpec((1, hd), lambda b, h: (0, 0))],
        out_specs=pl.BlockSpec((lc, gw), lambda b, h: (b, h)),
        out_shape=jax.ShapeDtypeStruct((n_batch * lc, ATT_HEADS * hd), BF16),
        compiler_params=_cparams(2),
        name="attn_ctx",
    )(p_ctx, p_ctx, p_ctx, q_norm_w, k_norm_w)


def _ret_kernel(qc_ref, kc_ref, vc_ref, gc_ref, ql_ref, kl_ref, vl_ref, gl_ref, cos_ref, sin_ref, dec_ref,
                nw_ref, yc_ref, yl_ref, ob_s, sf_s, sb_s, *, lc, ll):
    c = RET_CHUNK
    dk = RET_QK_DIM
    dv = RET_V_DIM
    half = dk // 4
    dec = dec_ref[...]
    log_g = jnp.minimum(dec, 0.0) - jnp.log(1.0 + jnp.exp(-jnp.abs(dec)))
    lg_f = log_g[0:1, :]
    lg_b = log_g[1:2, :]
    pos = lax.broadcasted_iota(jnp.int32, (c, dk), 0).astype(F32)
    qd_f = jnp.exp((pos + 1.0) * lg_f)
    kd_f = jnp.exp((c - 1.0 - pos) * lg_f)
    cd_f = jnp.exp(float(c) * lg_f)
    qd_b = jnp.exp((c - pos) * lg_b)
    kd_b = jnp.exp(pos * lg_b)
    cd_b = jnp.exp(float(c) * lg_b)
    ii = lax.broadcasted_iota(jnp.int32, (c, c), 0)
    jj = lax.broadcasted_iota(jnp.int32, (c, c), 1)
    diff = (ii - jj).astype(F32)
    intra = jnp.where(ii >= jj, jnp.exp(jnp.maximum(diff, 0.0) * lg_f[:, 0:c]),
                      jnp.exp(jnp.maximum(-diff, 0.0) * lg_b[:, 0:c]))
    nw = nw_ref[...]
    k_scale = dk ** -0.5

    def qk_ctx(rows):
        return qc_ref[rows, :].astype(F32), kc_ref[rows, :].astype(F32) * k_scale

    def qk_lat(rows):
        cos = cos_ref[rows, :]
        sin = sin_ref[rows, :]
        q = _rope(ql_ref[rows, :].astype(F32), cos, sin, half)
        k = _rope(kl_ref[rows, :].astype(F32), cos, sin, half) * k_scale
        return q, k

    def state_update(s_ref, k_dec, v, c_dec):
        kv = lax.dot_general(k_dec.astype(BF16), v, (((0,), (0,)), ((), ())), preferred_element_type=F32)
        s_ref[...] = s_ref[...] * c_dec + kv

    def bwd_chunk(q, k, v, orows):
        ob_s[orows, :] = jnp.dot((q * qd_b).astype(BF16), sb_s[...].astype(BF16), preferred_element_type=F32)
        state_update(sb_s, k * kd_b, v, cd_b)

    def fwd_chunk(q, k, v, gate, orows):
        att = lax.dot_general(q.astype(BF16), k.astype(BF16), (((1,), (1,)), ((), ())),
                              preferred_element_type=F32) * intra
        o = (jnp.dot(att.astype(BF16), v, preferred_element_type=F32)
             + jnp.dot((q * qd_f).astype(BF16), sf_s[...].astype(BF16), preferred_element_type=F32)
             + ob_s[orows, :])
        state_update(sf_s, k * kd_f, v, cd_f)
        mu = jnp.mean(o, axis=-1, keepdims=True)
        oc = o - mu
        var = jnp.mean(oc * oc, axis=-1, keepdims=True)
        y = oc * lax.rsqrt(var + EPS) * nw
        return _silu(gate) * y

    sf_s[...] = jnp.zeros((dk, dv), F32)
    sb_s[...] = jnp.zeros((dk, dv), F32)

    for ci in reversed(range(lc // c)):
        rows = pl.ds(ci * c, c)
        q, k = qk_ctx(rows)
        bwd_chunk(q, k, vc_ref[rows, :], rows)

    n_l = ll // c

    def bwd_body(i, carry):
        ci = n_l - 1 - i
        rows = pl.ds(pl.multiple_of(ci * c, c), c)
        orows = pl.ds(pl.multiple_of(lc + ci * c, c), c)
        q, k = qk_lat(rows)
        bwd_chunk(q, k, vl_ref[rows, :], orows)
        return carry

    lax.fori_loop(0, n_l, bwd_body, 0)

    for ci in range(lc // c):
        rows = pl.ds(ci * c, c)
        q, k = qk_ctx(rows)
        yc_ref[rows, :] = fwd_chunk(q, k, vc_ref[rows, :], gc_ref[rows, :].astype(F32), rows).astype(yc_ref.dtype)

    def fwd_body(ci, carry):
        rows = pl.ds(pl.multiple_of(ci * c, c), c)
        orows = pl.ds(pl.multiple_of(lc + ci * c, c), c)
        q, k = qk_lat(rows)
        yl_ref[rows, :] = fwd_chunk(q, k, vl_ref[rows, :], gl_ref[rows, :].astype(F32), orows).astype(yl_ref.dtype)
        return carry

    lax.fori_loop(0, n_l, fwd_body, 0)


def _retention(p_ctx, p_lat, cos, sin, dec, norm_w, n_batch, lc, ll):
    dk = RET_QK_DIM
    q0 = COL_RQ // dk
    k0 = COL_RK // dk
    v0 = COL_RV // dk
    g0 = COL_RGT // dk
    kern = functools.partial(_ret_kernel, lc=lc, ll=ll)

    def col(off):
        return lambda b, h: (b, off + h)

    return pl.pallas_call(
        kern,
        grid=(n_batch, RET_HEADS),
        in_specs=[pl.BlockSpec((lc, dk), col(q0)), pl.BlockSpec((lc, dk), col(k0)),
                  pl.BlockSpec((lc, dk), col(v0)), pl.BlockSpec((lc, dk), col(g0)),
                  pl.BlockSpec((ll, dk), col(q0)), pl.BlockSpec((ll, dk), col(k0)),
                  pl.BlockSpec((ll, dk), col(v0)), pl.BlockSpec((ll, dk), col(g0)),
                  pl.BlockSpec((ll, dk), lambda b, h: (0, 0)),
                  pl.BlockSpec((ll, dk), lambda b, h: (0, 0)),
                  pl.BlockSpec((None, 2, dk), lambda b, h: (h, 0, 0)),
                  pl.BlockSpec((1, dk), lambda b, h: (0, h))],
        out_specs=[pl.BlockSpec((lc, dk), lambda b, h: (b, h)),
                   pl.BlockSpec((ll, dk), lambda b, h: (b, h))],
        out_shape=[jax.ShapeDtypeStruct((n_batch * lc, RET_HEADS * RET_V_DIM), BF16),
                   jax.ShapeDtypeStruct((n_batch * ll, RET_HEADS * RET_V_DIM), BF16)],
        scratch_shapes=[pltpu.VMEM((lc + ll, RET_V_DIM), F32),
                        pltpu.VMEM((dk, RET_V_DIM), F32), pltpu.VMEM((dk, RET_V_DIM), F32)],
        compiler_params=_cparams(2),
        name="retention",
    )(p_ctx, p_ctx, p_ctx, p_ctx, p_lat, p_lat, p_lat, p_lat, cos, sin, dec, norm_w)


def _merge_kernel(x_ref, yr_ref, ya_ref, yt_ref, g0_ref, g1_ref, g2_ref, wb_ref, wo_ref, m_ref, nw_ref,
                  xo_ref, vo_ref):
    m = (_sigmoid(g0_ref[...].astype(F32)) * jnp.dot(yr_ref[...], wb_ref[0], preferred_element_type=F32)
         + _sigmoid(g1_ref[...].astype(F32)) * jnp.dot(ya_ref[...], wb_ref[1], preferred_element_type=F32)
         + _sigmoid(g2_ref[...].astype(F32)) * jnp.dot(yt_ref[...], wb_ref[2], preferred_element_type=F32))
    out = jnp.dot(m.astype(BF16), wo_ref[...], preferred_element_type=F32)
    x1 = x_ref[...] + m_ref[2:3, :] * out
    xo_ref[...] = x1
    v = _rms_rows(x1) * nw_ref[...]
    vo_ref[...] = (v * (1.0 + m_ref[4:5, :]) + m_ref[3:4, :]).astype(vo_ref.dtype)


def _merge(x2d, y_rnn, y_att, y_ret, p, mods, mod_row, norm_w, w_branch, w_out, tm):
    n_tok, d = x2d.shape
    bg0 = COL_BG // d
    tok = lambda i: (i, 0)
    const2 = lambda i: (0, 0)
    return pl.pallas_call(
        _merge_kernel,
        grid=(n_tok // tm,),
        in_specs=[pl.BlockSpec((tm, d), tok), pl.BlockSpec((tm, d), tok), pl.BlockSpec((tm, d), tok),
                  pl.BlockSpec((tm, d), tok),
                  pl.BlockSpec((tm, d), lambda i: (i, bg0)),
                  pl.BlockSpec((tm, d), lambda i: (i, bg0 + 1)),
                  pl.BlockSpec((tm, d), lambda i: (i, bg0 + 2)),
                  pl.BlockSpec((N_BRANCH, d, d), lambda i: (0, 0, 0)),
                  pl.BlockSpec((d, d), const2),
                  pl.BlockSpec((None, 6, d), lambda i: (mod_row(i), 0, 0)),
                  pl.BlockSpec((1, d), const2)],
        out_specs=[pl.BlockSpec((tm, d), tok), pl.BlockSpec((tm, d), tok)],
        out_shape=[jax.ShapeDtypeStruct((n_tok, d), F32), jax.ShapeDtypeStruct((n_tok, d), BF16)],
        compiler_params=_cparams(1),
        name="merge",
    )(x2d, y_rnn, y_att, y_ret, p, p, p, w_branch, w_out, mods, norm_w.reshape(1, d))


def _ffn_up_kernel(v_ref, w_ref, o_ref):
    o_ref[...] = jnp.dot(v_ref[...], w_ref[...], preferred_element_type=F32).astype(o_ref.dtype)


def _ffn_up(v2d, w, tm):
    n_tok, d = v2d.shape
    n = w.shape[1]
    tn = D_FF
    return pl.pallas_call(
        _ffn_up_kernel,
        grid=(n_tok // tm, n // tn),
        in_specs=[pl.BlockSpec((tm, d), lambda i, j: (i, 0)),
                  pl.BlockSpec((d, tn), lambda i, j: (0, j))],
        out_specs=pl.BlockSpec((tm, tn), lambda i, j: (i, j)),
        out_shape=jax.ShapeDtypeStruct((n_tok, n), BF16),
        compiler_params=_cparams(2),
        name="ffn_up",
    )(v2d, w)


def _ffn_down_kernel(a_ref, b_ref, ap_ref, an_ref, cw_ref, cb_ref, wd_ref, x_ref, m_ref, fw_ref, o_ref,
                     *, tiles_per_seq, final):
    tm = a_ref.shape[0]
    ti = pl.program_id(0) % tiles_per_seq
    a = a_ref[...].astype(F32)
    prev_row = jnp.where(ti == 0, 0.0, ap_ref[SUBLANES - 1:SUBLANES, :].astype(F32))
    next_row = jnp.where(ti == tiles_per_seq - 1, 0.0, an_ref[0:1, :].astype(F32))
    row = lax.broadcasted_iota(jnp.int32, a.shape, 0)
    a_m1 = jnp.where(row == 0, prev_row, pltpu.roll(a, 1, axis=0))
    a_p1 = jnp.where(row == tm - 1, next_row, pltpu.roll(a, tm - 1, axis=0))
    cw = cw_ref[...]
    conv = cw[0:1, :] * a_m1 + cw[1:2, :] * a + cw[2:3, :] * a_p1 + cb_ref[...]
    h = (_silu(conv) * b_ref[...].astype(F32)).astype(BF16)
    out = jnp.dot(h, wd_ref[...], preferred_element_type=F32)
    x2 = x_ref[...] + m_ref[5:6, :] * out
    if final:
        x2 = _rms_rows(x2) * fw_ref[...]
    o_ref[...] = x2


def _ffn_down(h, conv_w, conv_b, w_down, x2d, mods, mod_row, final_w, tm, seq_len, final):
    n_tok, d = x2d.shape
    f = D_FF
    tiles_per_seq = seq_len // tm
    hb = tm // SUBLANES
    n_hblk = n_tok // SUBLANES
    kern = functools.partial(_ffn_down_kernel, tiles_per_seq=tiles_per_seq, final=final)
    return pl.pallas_call(
        kern,
        grid=(n_tok // tm,),
        in_specs=[pl.BlockSpec((tm, f), lambda i: (i, 0)),
                  pl.BlockSpec((tm, f), lambda i: (i, 1)),
                  pl.BlockSpec((SUBLANES, f), lambda i: (jnp.maximum(i * hb - 1, 0), 0)),
                  pl.BlockSpec((SUBLANES, f), lambda i: (jnp.minimum((i + 1) * hb, n_hblk - 1), 0)),
                  pl.BlockSpec((FFN_CONV_W, f), lambda i: (0, 0)),
                  pl.BlockSpec((1, f), lambda i: (0, 0)),
                  pl.BlockSpec((f, d), lambda i: (0, 0)),
                  pl.BlockSpec((tm, d), lambda i: (i, 0)),
                  pl.BlockSpec((None, 6, d), lambda i: (mod_row(i), 0, 0)),
                  pl.BlockSpec((1, d), lambda i: (0, 0))],
        out_specs=pl.BlockSpec((tm, d), lambda i: (i, 0)),
        out_shape=jax.ShapeDtypeStruct((n_tok, d), F32),
        compiler_params=_cparams(1),
        name="ffn_down",
    )(h, h, h, h, conv_w, conv_b.reshape(1, f), w_down, x2d, mods, final_w.reshape(1, d))


def _rope_tables(seq_len, head_dim):
    n_freq = head_dim // 4
    t = jnp.arange(seq_len, dtype=jnp.int32)
    row = (t // GRID_W).astype(F32)
    col = (t % GRID_W).astype(F32)
    inv = ROPE_THETA ** (-jnp.arange(n_freq, dtype=F32) / n_freq)
    ang = jnp.stack([row[:, None] * inv, col[:, None] * inv], axis=1)
    cos = jnp.cos(ang)
    sin = jnp.sin(ang)
    cos_t = jnp.stack([cos, cos], axis=2).reshape(seq_len, head_dim)
    sin_t = jnp.stack([-sin, sin], axis=2).reshape(seq_len, head_dim)
    return cos_t, sin_t


def kernel(x, c, ctx, c_ctx, mod_w, mod_b, norm1_w, norm2_w, w_in, rnn_conv_w, rnn_conv_b, rglru_w, rglru_b,
           rglru_lam, q_norm_w, k_norm_w, ret_decay, ret_norm_w, w_branch, w_out, ffn_up, ffn_conv_w,
           ffn_conv_b, ffn_down, final_norm_w):
    n_batch, ll, d = x.shape
    lc = ctx.shape[1]
    depth = mod_w.shape[0]
    assert d == D_MODEL and n_batch + 1 <= MOD_ROWS
    assert ll % 1024 == 0 and lc % RNN_CHUNK == 0 and lc % RET_CHUNK == 0

    cos_a, sin_a = _rope_tables(ll, ATT_HEAD_DIM)
    cos_r, sin_r = _rope_tables(ll, RET_QK_DIM)

    c_all = jnp.zeros((MOD_ROWS, d), F32).at[:n_batch].set(c).at[n_batch].set(c_ctx)
    mods = _modulation(c_all, mod_w, mod_b).reshape(depth, MOD_ROWS, 6, d)

    tm_lat = 1024
    tm_ctx = lc
    lat_tiles = ll // tm_lat
    lat_row_big = lambda i: i // lat_tiles
    ctx_row = lambda i: n_batch
    tm_mix = 512
    lat_row_mix = lambda i: i // (ll // tm_mix)

    x2 = x.reshape(n_batch * ll, d)
    cx2 = ctx.reshape(n_batch * lc, d)

    for l in range(depth):
        last = l == depth - 1
        w_in_p = jnp.concatenate([w_in[l][:, s:s + n] for s, n in _REF_SEGMENTS], axis=1).astype(BF16)
        gate_w = jnp.transpose(rglru_w[l], (2, 3, 0, 1, 4)).reshape(RNN_BLOCKS, RNN_BLOCK_W, 4 * RNN_BLOCK_W)
        gate_w = gate_w.astype(BF16)
        gate_b = jnp.transpose(rglru_b[l].reshape(2, 2, RNN_BLOCKS, RNN_BLOCK_W), (2, 0, 1, 3))
        gate_b = gate_b.reshape(RNN_BLOCKS, 1, 4 * RNN_BLOCK_W)
        dec = jnp.broadcast_to(jnp.transpose(ret_decay[l])[:, :, None], (RET_HEADS, 2, RET_QK_DIM))
        wb = w_branch[l].astype(BF16)
        wo = w_out[l].astype(BF16)
        w_up = ffn_up[l].astype(BF16)
        w_dn = ffn_down[l].astype(BF16)
        ml = mods[l]

        p_lat = _inproj(x2, ml, lat_row_big, norm1_w[l], w_in_p, tm_lat)
        p_ctx = _inproj(cx2, ml, ctx_row, norm1_w[l], w_in_p, tm_ctx)

        yr_c, yr_l = _rglru(p_ctx, p_lat, rnn_conv_w[l], rnn_conv_b[l].reshape(1, D_RNN), gate_w, gate_b,
                            rglru_lam[l], n_batch, lc, ll)
        ya_l = _attn_lat(p_ctx, p_lat, cos_a, sin_a, q_norm_w[l].reshape(1, -1), k_norm_w[l].reshape(1, -1),
                         n_batch, lc, ll)
        yt_c, yt_l = _retention(p_ctx, p_lat, cos_r, sin_r, dec, ret_norm_w[l].reshape(1, -1), n_batch, lc, ll)

        x2, v_lat = _merge(x2, yr_l, ya_l, yt_l, p_lat, ml, lat_row_mix, norm2_w[l], wb, wo, tm_mix)
        h_lat = _ffn_up(v_lat, w_up, tm_lat)
        x2 = _ffn_down(h_lat, ffn_conv_w[l], ffn_conv_b[l], w_dn, x2, ml, lat_row_mix, final_norm_w,
                       tm_mix, ll, final=last)

        if not last:
            ya_c = _attn_ctx(p_ctx, q_norm_w[l].reshape(1, -1), k_norm_w[l].reshape(1, -1), n_batch, lc)
            cx2, v_ctx = _merge(cx2, yr_c, ya_c, yt_c, p_ctx, ml, ctx_row, norm2_w[l], wb, wo, tm_ctx)
            h_ctx = _ffn_up(v_ctx, w_up, tm_ctx)
            cx2 = _ffn_down(h_ctx, ffn_conv_w[l], ffn_conv_b[l], w_dn, cx2, ml, ctx_row, final_norm_w,
                            tm_ctx, lc, final=False)

    return x2.reshape(n_batch, ll, d)
```

```python
import functools

import numpy as np
import jax
import jax.numpy as jnp
from jax import lax
from jax.experimental import pallas as pl
from jax.experimental.pallas import tpu as pltpu

F32 = jnp.float32
BF16 = jnp.bfloat16

D_MODEL = 1024
GRID_W = 64
D_RNN = 1024
RNN_BLOCKS = 8
RNN_BLOCK_W = D_RNN // RNN_BLOCKS
RNN_CONV_W = 4
RNN_PAD_L = 2
RG_C = 8.0
ATT_HEADS = 8
ATT_KV_HEADS = 2
ATT_HEAD_DIM = 128
ATT_GROUP = ATT_HEADS // ATT_KV_HEADS
ROPE_THETA = 10000.0
RET_HEADS = 4
RET_QK_DIM = 256
RET_V_DIM = 256
RET_CHUNK = 128
N_BRANCH = 3
D_FF = 2816
FFN_CONV_W = 3
EPS = 1e-6

COL_RX = 0
COL_RG = 1024
COL_AQ = 2048
COL_RQ = 3072
COL_RK = 4096
COL_RV = 5120
COL_RGT = 6144
COL_BG = 7168
COL_AK = 10240
COL_AV = 10496
IN_COLS = 10752
_REF_SEGMENTS = ((0, 1024), (1024, 1024), (2048, 1024), (3584, 1024), (4608, 1024), (5632, 1024),
                 (6656, 1024), (7680, 3072), (3072, 256), (3328, 256))

SUBLANES = 8
MOD_ROWS = 16
VMEM_LIMIT = 52 * 1024 * 1024


def _cparams(n_grid):
    return pltpu.CompilerParams(dimension_semantics=("arbitrary",) * n_grid, vmem_limit_bytes=VMEM_LIMIT)


def _sigmoid(x):
    return 1.0 / (1.0 + jnp.exp(-x))


def _silu(x):
    return x * _sigmoid(x)


def _gelu_tanh(x):
    return x * (0.5 * (1.0 + jnp.tanh(np.sqrt(2.0 / np.pi).astype(np.float32) * (x + 0.044715 * (x * x * x)))))


def _rms_rows(x):
    return x * lax.rsqrt(jnp.mean(x * x, axis=-1, keepdims=True) + EPS)


def _rope(x, cos, sin_signed, half):
    n = x.shape[-1]
    from_lo = pltpu.roll(x, half, axis=1)
    from_hi = pltpu.roll(x, n - half, axis=1)
    lane = lax.broadcasted_iota(jnp.int32, x.shape, 1)
    partner = jnp.where((lane & half) != 0, from_lo, from_hi)
    return x * cos + partner * sin_signed


def _mod_kernel(c_ref, w_ref, b_ref, o_ref):
    sc = _silu(c_ref[...])
    o_ref[...] = jnp.dot(sc.astype(BF16), w_ref[...].astype(BF16), preferred_element_type=F32) + b_ref[...]


def _modulation(c_all, mod_w, mod_b):
    depth, d, n = mod_w.shape
    tn = 1536
    return pl.pallas_call(
        _mod_kernel,
        grid=(depth, n // tn),
        in_specs=[pl.BlockSpec((MOD_ROWS, d), lambda l, j: (0, 0)),
                  pl.BlockSpec((None, d, tn), lambda l, j: (l, 0, j)),
                  pl.BlockSpec((None, 1, tn), lambda l, j: (l, 0, j))],
        out_specs=pl.BlockSpec((None, MOD_ROWS, tn), lambda l, j: (l, 0, j)),
        out_shape=jax.ShapeDtypeStruct((depth, MOD_ROWS, n), F32),
        compiler_params=_cparams(2),
        name="modulation",
    )(c_all, mod_w, mod_b.reshape(depth, 1, n))


def _inproj_kernel(x_ref, m_ref, nw_ref, w_ref, o_ref, u_ref):
    @pl.when(pl.program_id(1) == 0)
    def _():
        y = _rms_rows(x_ref[...]) * nw_ref[...]
        u_ref[...] = (y * (1.0 + m_ref[1:2, :]) + m_ref[0:1, :]).astype(BF16)

    o_ref[...] = jnp.dot(u_ref[...], w_ref[...], preferred_element_type=F32).astype(o_ref.dtype)


def _inproj(x2d, mods, mod_row, norm_w, w, tm):
    n_tok, d = x2d.shape
    n = w.shape[1]
    tn = 1536
    return pl.pallas_call(
        _inproj_kernel,
        grid=(n_tok // tm, n // tn),
        in_specs=[pl.BlockSpec((tm, d), lambda i, j: (i, 0)),
                  pl.BlockSpec((None, 6, d), lambda i, j: (mod_row(i), 0, 0)),
                  pl.BlockSpec((1, d), lambda i, j: (0, 0)),
                  pl.BlockSpec((d, tn), lambda i, j: (0, j))],
        out_specs=pl.BlockSpec((tm, tn), lambda i, j: (i, j)),
        out_shape=jax.ShapeDtypeStruct((n_tok, n), BF16),
        scratch_shapes=[pltpu.VMEM((tm, d), BF16)],
        compiler_params=_cparams(2),
        name="inproj",
    )(x2d, mods, norm_w.reshape(1, d), w)


RNN_CHUNK = 256


def _tile_scan(a, b, reverse):
    n = a.shape[0]
    r8 = lax.broadcasted_iota(jnp.int32, a.shape, 0) & (SUBLANES - 1)
    for s in (1, 2, 4):
        shift = n - s if reverse else s
        a_sh = pltpu.roll(a, shift, axis=0)
        b_sh = pltpu.roll(b, shift, axis=0)
        valid = (r8 < SUBLANES - s) if reverse else (r8 >= s)
        b = jnp.where(valid, a * b_sh + b, b)
        a = jnp.where(valid, a * a_sh, a)
    return a, b


def _rglru_kernel(xc_ref, xl_ref, gc_ref, gl_ref, cw_ref, cb_ref, gw_ref, gb_ref, lam_ref,
                  yc_ref, yl_ref, xs, af, hf, ab, hb, *, lc, ll):
    w = xs.shape[1]
    tc = RNN_CHUNK
    pad = SUBLANES
    zero_pad = jnp.zeros((pad, w), F32)
    c_base = pad
    l_base = lc + 3 * pad
    xs[0:pad, :] = zero_pad
    xs[c_base + lc:c_base + lc + 2 * pad, :] = jnp.zeros((2 * pad, w), F32)
    xs[l_base + ll:l_base + ll + pad, :] = zero_pad
    xs[c_base:c_base + lc, :] = xc_ref[...].astype(F32)
    xs[l_base:l_base + ll, :] = xl_ref[...].astype(F32)

    lam = lam_ref[...]
    sp = jnp.maximum(-lam, 0.0) + jnp.log(1.0 + jnp.exp(-jnp.abs(lam)))
    sp_f = sp[0:1, :]
    sp_b = sp[1:2, :]
    cw = cw_ref[...]
    cb = cb_ref[...]
    gw = gw_ref[...]
    gb = gb_ref[...]

    def gates_chunk(xs_base, out_base, t0):
        win = xs[pl.ds(pl.multiple_of(xs_base + t0 - pad, SUBLANES), tc + 2 * pad), :]
        nwin = tc + 2 * pad
        x = cb + jnp.zeros((tc, w), F32)
        for k in range(RNN_CONV_W):
            d = k - RNN_PAD_L
            sh = win if d == 0 else pltpu.roll(win, (-d) % nwin, axis=0)
            x = x + cw[k:k + 1, :] * sh[pad:pad + tc, :]
        g = jnp.dot(x.astype(BF16), gw, preferred_element_type=F32) + gb
        rows = pl.ds(pl.multiple_of(out_base + t0, SUBLANES), tc)
        for direction, (a_ref, h_ref, spd) in enumerate(((af, hf, sp_f), (ab, hb, sp_b))):
            r = _sigmoid(g[:, (2 * direction) * w:(2 * direction + 1) * w])
            i = _sigmoid(g[:, (2 * direction + 1) * w:(2 * direction + 2) * w])
            log_a = (-RG_C) * r * spd
            a = jnp.exp(log_a)
            b = jnp.sqrt(1.0 - jnp.exp(2.0 * log_a)) * (i * x)
            a, b = _tile_scan(a, b, reverse=(direction == 1))
            a_ref[rows, :] = a
            h_ref[rows, :] = b

    def ctx_body(c, carry):
        gates_chunk(c_base, 0, c * tc)
        return carry

    def lat_body(c, carry):
        gates_chunk(l_base, lc, c * tc)
        return carry

    lax.fori_loop(0, lc // tc, ctx_body, 0)
    lax.fori_loop(0, ll // tc, lat_body, 0)

    n_ct = lc // SUBLANES
    n_t = (lc + ll) // SUBLANES

    def carry_body(j, carry):
        h_f, h_b = carry
        rf = pl.ds(pl.multiple_of(j * SUBLANES, SUBLANES), SUBLANES)
        jb = jnp.where(j < n_ct, n_ct - 1 - j, n_t - 1 - (j - n_ct))
        rb = pl.ds(pl.multiple_of(jb * SUBLANES, SUBLANES), SUBLANES)
        t_f = af[rf, :] * h_f + hf[rf, :]
        t_b = ab[rb, :] * h_b + hb[rb, :]
        hf[rf, :] = t_f
        hb[rb, :] = t_b
        return (jnp.broadcast_to(t_f[SUBLANES - 1:SUBLANES, :], (SUBLANES, w)),
                jnp.broadcast_to(t_b[0:1, :], (SUBLANES, w)))

    zero_h = jnp.zeros((SUBLANES, w), F32)
    lax.fori_loop(0, n_t, carry_body, (zero_h, zero_h), unroll=8)

    def out_ctx(c, carry):
        rows = pl.ds(pl.multiple_of(c * tc, SUBLANES), tc)
        yc_ref[rows, :] = (_gelu_tanh(gc_ref[rows, :].astype(F32)) * (hf[rows, :] + hb[rows, :])).astype(yc_ref.dtype)
        return carry

    def out_lat(c, carry):
        rows = pl.ds(pl.multiple_of(c * tc, SUBLANES), tc)
        srows = pl.ds(pl.multiple_of(lc + c * tc, SUBLANES), tc)
        yl_ref[rows, :] = (_gelu_tanh(gl_ref[rows, :].astype(F32)) * (hf[srows, :] + hb[srows, :])).astype(yl_ref.dtype)
        return carry

    lax.fori_loop(0, lc // tc, out_ctx, 0)
    lax.fori_loop(0, ll // tc, out_lat, 0)


def _rglru(p_ctx, p_lat, conv_w, conv_b, gate_w, gate_b, lam, n_batch, lc, ll):
    w = RNN_BLOCK_W
    nb = RNN_BLOCKS
    s = lc + ll
    rx0 = COL_RX // w
    rg0 = COL_RG // w
    kern = functools.partial(_rglru_kernel, lc=lc, ll=ll)
    return pl.pallas_call(
        kern,
        grid=(n_batch, nb),
        in_specs=[pl.BlockSpec((lc, w), lambda b, c: (b, rx0 + c)),
                  pl.BlockSpec((ll, w), lambda b, c: (b, rx0 + c)),
                  pl.BlockSpec((lc, w), lambda b, c: (b, rg0 + c)),
                  pl.BlockSpec((ll, w), lambda b, c: (b, rg0 + c)),
                  pl.BlockSpec((RNN_CONV_W, w), lambda b, c: (0, c)),
                  pl.BlockSpec((1, w), lambda b, c: (0, c)),
                  pl.BlockSpec((None, w, 4 * w), lambda b, c: (c, 0, 0)),
                  pl.BlockSpec((None, 1, 4 * w), lambda b, c: (c, 0, 0)),
                  pl.BlockSpec((2, w), lambda b, c: (0, c))],
        out_specs=[pl.BlockSpec((lc, w), lambda b, c: (b, c)),
                   pl.BlockSpec((ll, w), lambda b, c: (b, c))],
        out_shape=[jax.ShapeDtypeStruct((n_batch * lc, D_RNN), BF16),
                   jax.ShapeDtypeStruct((n_batch * ll, D_RNN), BF16)],
        scratch_shapes=[pltpu.VMEM((s + 4 * SUBLANES, w), F32),
                        pltpu.VMEM((s, w), F32), pltpu.VMEM((s, w), F32),
                        pltpu.VMEM((s, w), F32), pltpu.VMEM((s, w), F32)],
        compiler_params=_cparams(2),
        name="rglru",
    )(p_ctx, p_lat, p_ctx, p_lat, conv_w, conv_b, gate_w, gate_b, lam)


ATT_TQ = 256
ATT_KEY_CHUNK = 1024
LOG2E = 1.4426950408889634
ATT_SAFE_SPREAD = 96.0


def _attn_lat_kernel(q_ref, kc_ref, vc_ref, kl_ref, vl_ref, cq_ref, sq_ref, ck_ref, sk_ref, qw_ref, kw_ref,
                     o_ref, k_s, v_s, q_s, p_s, kmax_s, *, lc):
    hd = ATT_HEAD_DIM
    half = hd // 4
    tq = q_ref.shape[0]
    s_len = k_s.shape[0]

    @pl.when(pl.program_id(2) == 0)
    def _():
        kw = kw_ref[...]
        kc = (_rms_rows(kc_ref[...].astype(F32)) * kw).astype(BF16)
        kl = _rope(_rms_rows(kl_ref[...].astype(F32)) * kw, ck_ref[...], sk_ref[...], half).astype(BF16)
        k_s[0:lc, :] = kc
        k_s[lc:, :] = kl
        v_s[0:lc, 0:hd] = vc_ref[...]
        v_s[lc:, 0:hd] = vl_ref[...]
        v_s[:, hd:] = jnp.ones((s_len, hd), BF16)
        kc32 = kc.astype(F32)
        kl32 = kl.astype(F32)
        kn_c = jnp.max(jnp.sqrt(jnp.sum(kc32 * kc32, axis=-1, keepdims=True)))
        kn_l = jnp.max(jnp.sqrt(jnp.sum(kl32 * kl32, axis=-1, keepdims=True)))
        kmax_s[0] = jnp.maximum(kn_c, kn_l)

    qw = qw_ref[...] * (hd ** -0.5 * LOG2E)
    cq = cq_ref[...]
    sq = sq_ref[...]
    for g in range(ATT_GROUP):
        q = _rms_rows(q_ref[:, g * hd:(g + 1) * hd].astype(F32)) * qw
        q_s[g * tq:(g + 1) * tq, :] = _rope(q, cq, sq, half).astype(BF16)
    q32 = q_s[...].astype(F32)
    qn = jnp.sqrt(jnp.sum(q32 * q32, axis=-1, keepdims=True))
    kmax = kmax_s[0]
    bound = qn * kmax
    safe = 2.0 * jnp.max(qn) * kmax < ATT_SAFE_SPREAD

    @pl.when(safe)
    def _():
        q_all = q_s[...]
        for c0 in range(0, s_len, ATT_KEY_CHUNK):
            n = min(ATT_KEY_CHUNK, s_len - c0)
            s = lax.dot_general(q_all, k_s[c0:c0 + n, :], (((1,), (1,)), ((), ())), preferred_element_type=F32)
            p_s[:, c0:c0 + n] = jnp.exp2(s - bound).astype(BF16)
        acc = jnp.dot(p_s[...], v_s[...], preferred_element_type=F32)
        for g in range(ATT_GROUP):
            a = acc[g * tq:(g + 1) * tq, :]
            o_ref[:, g * hd:(g + 1) * hd] = (a[:, 0:hd] / a[:, hd:]).astype(o_ref.dtype)

    @pl.when(jnp.logical_not(safe))
    def _():
        k_all = k_s[...]
        v_all = v_s[:, 0:hd]
        for g in range(ATT_GROUP):
            s = lax.dot_general(q_s[g * tq:(g + 1) * tq, :], k_all, (((1,), (1,)), ((), ())),
                                preferred_element_type=F32)
            p = jnp.exp2(s - jnp.max(s, axis=-1, keepdims=True))
            denom = jnp.sum(p, axis=-1, keepdims=True)
            o = jnp.dot(p.astype(BF16), v_all, preferred_element_type=F32)
            o_ref[:, g * hd:(g + 1) * hd] = (o / denom).astype(o_ref.dtype)


def _attn_ctx_kernel(q_ref, kc_ref, vc_ref, qw_ref, kw_ref, o_ref):
    hd = ATT_HEAD_DIM
    k = (_rms_rows(kc_ref[...].astype(F32)) * kw_ref[...]).astype(BF16)
    qw = qw_ref[...] * (hd ** -0.5)
    v_c = vc_ref[...]
    for g in range(ATT_GROUP):
        q = (_rms_rows(q_ref[:, g * hd:(g + 1) * hd].astype(F32)) * qw).astype(BF16)
        s = lax.dot_general(q, k, (((1,), (1,)), ((), ())), preferred_element_type=F32)
        m = jnp.max(s, axis=-1, keepdims=True)
        p = jnp.exp(s - m)
        denom = jnp.sum(p, axis=-1, keepdims=True)
        o = jnp.dot(p.astype(BF16), v_c, preferred_element_type=F32)
        o_ref[:, g * hd:(g + 1) * hd] = (o / denom).astype(o_ref.dtype)


def _attn_lat(p_ctx, p_lat, cos, sin, q_norm_w, k_norm_w, n_batch, lc, ll):
    hd = ATT_HEAD_DIM
    gw = ATT_GROUP * hd
    tq = ATT_TQ
    nq = ll // tq
    q0 = COL_AQ // gw
    k0 = COL_AK // hd
    v0 = COL_AV // hd
    kern = functools.partial(_attn_lat_kernel, lc=lc)
    return pl.pallas_call(
        kern,
        grid=(n_batch, ATT_KV_HEADS, nq),
        in_specs=[pl.BlockSpec((tq, gw), lambda b, h, i: (b * nq + i, q0 + h)),
                  pl.BlockSpec((lc, hd), lambda b, h, i: (b, k0 + h)),
                  pl.BlockSpec((lc, hd), lambda b, h, i: (b, v0 + h)),
                  pl.BlockSpec((ll, hd), lambda b, h, i: (b, k0 + h)),
                  pl.BlockSpec((ll, hd), lambda b, h, i: (b, v0 + h)),
                  pl.BlockSpec((tq, hd), lambda b, h, i: (i, 0)),
                  pl.BlockSpec((tq, hd), lambda b, h, i: (i, 0)),
                  pl.BlockSpec((ll, hd), lambda b, h, i: (0, 0)),
                  pl.BlockSpec((ll, hd), lambda b, h, i: (0, 0)),
                  pl.BlockSpec((1, hd), lambda b, h, i: (0, 0)),
                  pl.BlockSpec((1, hd), lambda b, h, i: (0, 0))],
        out_specs=pl.BlockSpec((tq, gw), lambda b, h, i: (b * nq + i, h)),
        out_shape=jax.ShapeDtypeStruct((n_batch * ll, ATT_HEADS * hd), BF16),
        scratch_shapes=[pltpu.VMEM((lc + ll, hd), BF16),
                        pltpu.VMEM((lc + ll, 2 * hd), BF16),
                        pltpu.VMEM((ATT_GROUP * tq, hd), BF16),
                        pltpu.VMEM((ATT_GROUP * tq, lc + ll), BF16),
                        pltpu.SMEM((1,), F32)],
        compiler_params=_cparams(3),
        name="attn_lat",
    )(p_lat, p_ctx, p_ctx, p_lat, p_lat, cos, sin, cos, sin, q_norm_w, k_norm_w)


def _attn_ctx(p_ctx, q_norm_w, k_norm_w, n_batch, lc):
    hd = ATT_HEAD_DIM
    gw = ATT_GROUP * hd
    q0 = COL_AQ // gw
    k0 = COL_AK // hd
    v0 = COL_AV // hd
    return pl.pallas_call(
        _attn_ctx_kernel,
        grid=(n_batch, ATT_KV_HEADS),
        in_specs=[pl.BlockSpec((lc, gw), lambda b, h: (b, q0 + h)),
                  pl.BlockSpec((lc, hd), lambda b, h: (b, k0 + h)),
                  pl.BlockSpec((lc, hd), lambda b, h: (b, v0 + h)),
                  pl.BlockSpec((1, hd), lambda b, h: (0, 0)),
                  pl.BlockSpec((1, hd), lambda b, h: (0, 0))],
        out_specs=pl.BlockSpec((lc, gw), lambda b, h: (b, h)),
        out_shape=jax.ShapeDtypeStruct((n_batch * lc, ATT_HEADS * hd), BF16),
        compiler_params=_cparams(2),
        name="attn_ctx",
    )(p_ctx, p_ctx, p_ctx, q_norm_w, k_norm_w)


def _ret_kernel(qc_ref, kc_ref, vc_ref, gc_ref, ql_ref, kl_ref, vl_ref, gl_ref, cos_ref, sin_ref, dec_ref,
                nw_ref, yc_ref, yl_ref, ob_s, sf_s, sb_s, *, lc, ll):
    c = RET_CHUNK
    dk = RET_QK_DIM
    dv = RET_V_DIM
    half = dk // 4
    dec = dec_ref[...]
    log_g = jnp.minimum(dec, 0.0) - jnp.log(1.0 + jnp.exp(-jnp.abs(dec)))
    lg_f = log_g[0:1, :]
    lg_b = log_g[1:2, :]
    pos = lax.broadcasted_iota(jnp.int32, (c, dk), 0).astype(F32)
    qd_f = jnp.exp((pos + 1.0) * lg_f)
    kd_f = jnp.exp((c - 1.0 - pos) * lg_f)
    cd_f = jnp.exp(float(c) * lg_f)
    qd_b = jnp.exp((c - pos) * lg_b)
    kd_b = jnp.exp(pos * lg_b)
    cd_b = jnp.exp(float(c) * lg_b)
    ii = lax.broadcasted_iota(jnp.int32, (c, c), 0)
    jj = lax.broadcasted_iota(jnp.int32, (c, c), 1)
    diff = (ii - jj).astype(F32)
    intra = jnp.where(ii >= jj, jnp.exp(jnp.maximum(diff, 0.0) * lg_f[:, 0:c]),
                      jnp.exp(jnp.maximum(-diff, 0.0) * lg_b[:, 0:c]))
    nw = nw_ref[...]
    k_scale = dk ** -0.5

    def qk_ctx(rows):
        return qc_ref[rows, :].astype(F32), kc_ref[rows, :].astype(F32) * k_scale

    def qk_lat(rows):
        cos = cos_ref[rows, :]
        sin = sin_ref[rows, :]
        q = _rope(ql_ref[rows, :].astype(F32), cos, sin, half)
        k = _rope(kl_ref[rows, :].astype(F32), cos, sin, half) * k_scale
        return q, k

    def state_update(s_ref, k_dec, v, c_dec):
        kv = lax.dot_general(k_dec.astype(BF16), v, (((0,), (0,)), ((), ())), preferred_element_type=F32)
        s_ref[...] = s_ref[...] * c_dec + kv

    def bwd_chunk(q, k, v, orows):
        ob_s[orows, :] = jnp.dot((q * qd_b).astype(BF16), sb_s[...].astype(BF16), preferred_element_type=F32)
        state_update(sb_s, k * kd_b, v, cd_b)

    def fwd_chunk(q, k, v, gate, orows):
        att = lax.dot_general(q.astype(BF16), k.astype(BF16), (((1,), (1,)), ((), ())),
                              preferred_element_type=F32) * intra
        o = (jnp.dot(att.astype(BF16), v, preferred_element_type=F32)
             + jnp.dot((q * qd_f).astype(BF16), sf_s[...].astype(BF16), preferred_element_type=F32)
             + ob_s[orows, :])
        state_update(sf_s, k * kd_f, v, cd_f)
        mu = jnp.mean(o, axis=-1, keepdims=True)
        oc = o - mu
        var = jnp.mean(oc * oc, axis=-1, keepdims=True)
        y = oc * lax.rsqrt(var + EPS) * nw
        return _silu(gate) * y

    sf_s[...] = jnp.zeros((dk, dv), F32)
    sb_s[...] = jnp.zeros((dk, dv), F32)

    for ci in reversed(range(lc // c)):
        rows = pl.ds(ci * c, c)
        q, k = qk_ctx(rows)
        bwd_chunk(q, k, vc_ref[rows, :], rows)

    n_l = ll // c

    def bwd_body(i, carry):
        ci = n_l - 1 - i
        rows = pl.ds(pl.multiple_of(ci * c, c), c)
        orows = pl.ds(pl.multiple_of(lc + ci * c, c), c)
        q, k = qk_lat(rows)
        bwd_chunk(q, k, vl_ref[rows, :], orows)
        return carry

    lax.fori_loop(0, n_l, bwd_body, 0, unroll=4)

    for ci in range(lc // c):
        rows = pl.ds(ci * c, c)
        q, k = qk_ctx(rows)
        yc_ref[rows, :] = fwd_chunk(q, k, vc_ref[rows, :], gc_ref[rows, :].astype(F32), rows).astype(yc_ref.dtype)

    def fwd_body(ci, carry):
        rows = pl.ds(pl.multiple_of(ci * c, c), c)
        orows = pl.ds(pl.multiple_of(lc + ci * c, c), c)
        q, k = qk_lat(rows)
        yl_ref[rows, :] = fwd_chunk(q, k, vl_ref[rows, :], gl_ref[rows, :].astype(F32), orows).astype(yl_ref.dtype)
        return carry

    lax.fori_loop(0, n_l, fwd_body, 0, unroll=4)


def _retention(p_ctx, p_lat, cos, sin, dec, norm_w, n_batch, lc, ll):
    dk = RET_QK_DIM
    q0 = COL_RQ // dk
    k0 = COL_RK // dk
    v0 = COL_RV // dk
    g0 = COL_RGT // dk
    kern = functools.partial(_ret_kernel, lc=lc, ll=ll)

    def col(off):
        return lambda b, h: (b, off + h)

    return pl.pallas_call(
        kern,
        grid=(n_batch, RET_HEADS),
        in_specs=[pl.BlockSpec((lc, dk), col(q0)), pl.BlockSpec((lc, dk), col(k0)),
                  pl.BlockSpec((lc, dk), col(v0)), pl.BlockSpec((lc, dk), col(g0)),
                  pl.BlockSpec((ll, dk), col(q0)), pl.BlockSpec((ll, dk), col(k0)),
                  pl.BlockSpec((ll, dk), col(v0)), pl.BlockSpec((ll, dk), col(g0)),
                  pl.BlockSpec((ll, dk), lambda b, h: (0, 0)),
                  pl.BlockSpec((ll, dk), lambda b, h: (0, 0)),
                  pl.BlockSpec((None, 2, dk), lambda b, h: (h, 0, 0)),
                  pl.BlockSpec((1, dk), lambda b, h: (0, h))],
        out_specs=[pl.BlockSpec((lc, dk), lambda b, h: (b, h)),
                   pl.BlockSpec((ll, dk), lambda b, h: (b, h))],
        out_shape=[jax.ShapeDtypeStruct((n_batch * lc, RET_HEADS * RET_V_DIM), BF16),
                   jax.ShapeDtypeStruct((n_batch * ll, RET_HEADS * RET_V_DIM), BF16)],
        scratch_shapes=[pltpu.VMEM((lc + ll, RET_V_DIM), F32),
                        pltpu.VMEM((dk, RET_V_DIM), F32), pltpu.VMEM((dk, RET_V_DIM), F32)],
        compiler_params=_cparams(2),
        name="retention",
    )(p_ctx, p_ctx, p_ctx, p_ctx, p_lat, p_lat, p_lat, p_lat, cos, sin, dec, norm_w)


def _merge_kernel(x_ref, yr_ref, ya_ref, yt_ref, g0_ref, g1_ref, g2_ref, wb_ref, wo_ref, m_ref, nw_ref,
                  xo_ref, vo_ref):
    m = (_sigmoid(g0_ref[...].astype(F32)) * jnp.dot(yr_ref[...], wb_ref[0], preferred_element_type=F32)
         + _sigmoid(g1_ref[...].astype(F32)) * jnp.dot(ya_ref[...], wb_ref[1], preferred_element_type=F32)
         + _sigmoid(g2_ref[...].astype(F32)) * jnp.dot(yt_ref[...], wb_ref[2], preferred_element_type=F32))
    out = jnp.dot(m.astype(BF16), wo_ref[...], preferred_element_type=F32)
    x1 = x_ref[...] + m_ref[2:3, :] * out
    xo_ref[...] = x1
    v = _rms_rows(x1) * nw_ref[...]
    vo_ref[...] = (v * (1.0 + m_ref[4:5, :]) + m_ref[3:4, :]).astype(vo_ref.dtype)


def _merge(x2d, y_rnn, y_att, y_ret, p, mods, mod_row, norm_w, w_branch, w_out, tm):
    n_tok, d = x2d.shape
    bg0 = COL_BG // d
    tok = lambda i: (i, 0)
    const2 = lambda i: (0, 0)
    return pl.pallas_call(
        _merge_kernel,
        grid=(n_tok // tm,),
        in_specs=[pl.BlockSpec((tm, d), tok), pl.BlockSpec((tm, d), tok), pl.BlockSpec((tm, d), tok),
                  pl.BlockSpec((tm, d), tok),
                  pl.BlockSpec((tm, d), lambda i: (i, bg0)),
                  pl.BlockSpec((tm, d), lambda i: (i, bg0 + 1)),
                  pl.BlockSpec((tm, d), lambda i: (i, bg0 + 2)),
                  pl.BlockSpec((N_BRANCH, d, d), lambda i: (0, 0, 0)),
                  pl.BlockSpec((d, d), const2),
                  pl.BlockSpec((None, 6, d), lambda i: (mod_row(i), 0, 0)),
                  pl.BlockSpec((1, d), const2)],
        out_specs=[pl.BlockSpec((tm, d), tok), pl.BlockSpec((tm, d), tok)],
        out_shape=[jax.ShapeDtypeStruct((n_tok, d), F32), jax.ShapeDtypeStruct((n_tok, d), BF16)],
        compiler_params=_cparams(1),
        name="merge",
    )(x2d, y_rnn, y_att, y_ret, p, p, p, w_branch, w_out, mods, norm_w.reshape(1, d))


def _ffn_up_kernel(v_ref, w_ref, o_ref):
    o_ref[...] = jnp.dot(v_ref[...], w_ref[...], preferred_element_type=F32).astype(o_ref.dtype)


def _ffn_up(v2d, w, tm):
    n_tok, d = v2d.shape
    n = w.shape[1]
    tn = D_FF
    return pl.pallas_call(
        _ffn_up_kernel,
        grid=(n_tok // tm, n // tn),
        in_specs=[pl.BlockSpec((tm, d), lambda i, j: (i, 0)),
                  pl.BlockSpec((d, tn), lambda i, j: (0, j))],
        out_specs=pl.BlockSpec((tm, tn), lambda i, j: (i, j)),
        out_shape=jax.ShapeDtypeStruct((n_tok, n), BF16),
        compiler_params=_cparams(2),
        name="ffn_up",
    )(v2d, w)


def _ffn_down_kernel(a_ref, b_ref, ap_ref, an_ref, cw_ref, cb_ref, wd_ref, x_ref, m_ref, fw_ref, o_ref,
                     *, tiles_per_seq, final):
    tm = a_ref.shape[0]
    ti = pl.program_id(0) % tiles_per_seq
    a = a_ref[...].astype(F32)
    prev_row = jnp.where(ti == 0, 0.0, ap_ref[SUBLANES - 1:SUBLANES, :].astype(F32))
    next_row = jnp.where(ti == tiles_per_seq - 1, 0.0, an_ref[0:1, :].astype(F32))
    row = lax.broadcasted_iota(jnp.int32, a.shape, 0)
    a_m1 = jnp.where(row == 0, prev_row, pltpu.roll(a, 1, axis=0))
    a_p1 = jnp.where(row == tm - 1, next_row, pltpu.roll(a, tm - 1, axis=0))
    cw = cw_ref[...]
    conv = cw[0:1, :] * a_m1 + cw[1:2, :] * a + cw[2:3, :] * a_p1 + cb_ref[...]
    h = (_silu(conv) * b_ref[...].astype(F32)).astype(BF16)
    out = jnp.dot(h, wd_ref[...], preferred_element_type=F32)
    x2 = x_ref[...] + m_ref[5:6, :] * out
    if final:
        x2 = _rms_rows(x2) * fw_ref[...]
    o_ref[...] = x2


def _ffn_down(h, conv_w, conv_b, w_down, x2d, mods, mod_row, final_w, tm, seq_len, final):
    n_tok, d = x2d.shape
    f = D_FF
    tiles_per_seq = seq_len // tm
    hb = tm // SUBLANES
    n_hblk = n_tok // SUBLANES
    kern = functools.partial(_ffn_down_kernel, tiles_per_seq=tiles_per_seq, final=final)
    return pl.pallas_call(
        kern,
        grid=(n_tok // tm,),
        in_specs=[pl.BlockSpec((tm, f), lambda i: (i, 0)),
                  pl.BlockSpec((tm, f), lambda i: (i, 1)),
                  pl.BlockSpec((SUBLANES, f), lambda i: (jnp.maximum(i * hb - 1, 0), 0)),
                  pl.BlockSpec((SUBLANES, f), lambda i: (jnp.minimum((i + 1) * hb, n_hblk - 1), 0)),
                  pl.BlockSpec((FFN_CONV_W, f), lambda i: (0, 0)),
                  pl.BlockSpec((1, f), lambda i: (0, 0)),
                  pl.BlockSpec((f, d), lambda i: (0, 0)),
                  pl.BlockSpec((tm, d), lambda i: (i, 0)),
                  pl.BlockSpec((None, 6, d), lambda i: (mod_row(i), 0, 0)),
                  pl.BlockSpec((1, d), lambda i: (0, 0))],
        out_specs=pl.BlockSpec((tm, d), lambda i: (i, 0)),
        out_shape=jax.ShapeDtypeStruct((n_tok, d), F32),
        compiler_params=_cparams(1),
        name="ffn_down",
    )(h, h, h, h, conv_w, conv_b.reshape(1, f), w_down, x2d, mods, final_w.reshape(1, d))


def _rope_tables(seq_len, head_dim):
    n_freq = head_dim // 4
    t = jnp.arange(seq_len, dtype=jnp.int32)
    row = (t // GRID_W).astype(F32)
    col = (t % GRID_W).astype(F32)
    inv = ROPE_THETA ** (-jnp.arange(n_freq, dtype=F32) / n_freq)
    ang = jnp.stack([row[:, None] * inv, col[:, None] * inv], axis=1)
    cos = jnp.cos(ang)
    sin = jnp.sin(ang)
    cos_t = jnp.stack([cos, cos], axis=2).reshape(seq_len, head_dim)
    sin_t = jnp.stack([-sin, sin], axis=2).reshape(seq_len, head_dim)
    return cos_t, sin_t


def kernel(x, c, ctx, c_ctx, mod_w, mod_b, norm1_w, norm2_w, w_in, rnn_conv_w, rnn_conv_b, rglru_w, rglru_b,
           rglru_lam, q_norm_w, k_norm_w, ret_decay, ret_norm_w, w_branch, w_out, ffn_up, ffn_conv_w,
           ffn_conv_b, ffn_down, final_norm_w):
    n_batch, ll, d = x.shape
    lc = ctx.shape[1]
    depth = mod_w.shape[0]
    assert d == D_MODEL and n_batch + 1 <= MOD_ROWS
    assert ll % 1024 == 0 and lc % RNN_CHUNK == 0 and lc % RET_CHUNK == 0

    cos_a, sin_a = _rope_tables(ll, ATT_HEAD_DIM)
    cos_r, sin_r = _rope_tables(ll, RET_QK_DIM)

    c_all = jnp.zeros((MOD_ROWS, d), F32).at[:n_batch].set(c).at[n_batch].set(c_ctx)
    mods = _modulation(c_all, mod_w, mod_b).reshape(depth, MOD_ROWS, 6, d)

    tm_lat = 1024
    tm_ctx = lc
    lat_tiles = ll // tm_lat
    lat_row_big = lambda i: i // lat_tiles
    ctx_row = lambda i: n_batch
    tm_mix = 512
    lat_row_mix = lambda i: i // (ll // tm_mix)

    x2 = x.reshape(n_batch * ll, d)
    cx2 = ctx.reshape(n_batch * lc, d)

    for l in range(depth):
        last = l == depth - 1
        w_in_p = jnp.concatenate([w_in[l][:, s:s + n] for s, n in _REF_SEGMENTS], axis=1).astype(BF16)
        gate_w = jnp.transpose(rglru_w[l], (2, 3, 0, 1, 4)).reshape(RNN_BLOCKS, RNN_BLOCK_W, 4 * RNN_BLOCK_W)
        gate_w = gate_w.astype(BF16)
        gate_b = jnp.transpose(rglru_b[l].reshape(2, 2, RNN_BLOCKS, RNN_BLOCK_W), (2, 0, 1, 3))
        gate_b = gate_b.reshape(RNN_BLOCKS, 1, 4 * RNN_BLOCK_W)
        dec = jnp.broadcast_to(jnp.transpose(ret_decay[l])[:, :, None], (RET_HEADS, 2, RET_QK_DIM))
        wb = w_branch[l].astype(BF16)
        wo = w_out[l].astype(BF16)
        w_up = ffn_up[l].astype(BF16)
        w_dn = ffn_down[l].astype(BF16)
        ml = mods[l]

        p_lat = _inproj(x2, ml, lat_row_big, norm1_w[l], w_in_p, tm_lat)
        p_ctx = _inproj(cx2, ml, ctx_row, norm1_w[l], w_in_p, tm_ctx)

        yr_c, yr_l = _rglru(p_ctx, p_lat, rnn_conv_w[l], rnn_conv_b[l].reshape(1, D_RNN), gate_w, gate_b,
                            rglru_lam[l], n_batch, lc, ll)
        ya_l = _attn_lat(p_ctx, p_lat, cos_a, sin_a, q_norm_w[l].reshape(1, -1), k_norm_w[l].reshape(1, -1),
                         n_batch, lc, ll)
        yt_c, yt_l = _retention(p_ctx, p_lat, cos_r, sin_r, dec, ret_norm_w[l].reshape(1, -1), n_batch, lc, ll)

        x2, v_lat = _merge(x2, yr_l, ya_l, yt_l, p_lat, ml, lat_row_mix, norm2_w[l], wb, wo, tm_mix)
        h_lat = _ffn_up(v_lat, w_up, tm_lat)
        x2 = _ffn_down(h_lat, ffn_conv_w[l], ffn_conv_b[l], w_dn, x2, ml, lat_row_mix, final_norm_w,
                       tm_mix, ll, final=last)

        if not last:
            ya_c = _attn_ctx(p_ctx, q_norm_w[l].reshape(1, -1), k_norm_w[l].reshape(1, -1), n_batch, lc)
            cx2, v_ctx = _merge(cx2, yr_c, ya_c, yt_c, p_ctx, ml, ctx_row, norm2_w[l], wb, wo, tm_ctx)
            h_ctx = _ffn_up(v_ctx, w_up, tm_ctx)
            cx2 = _ffn_down(h_ctx, ffn_conv_w[l], ffn_conv_b[l], w_dn, cx2, ml, ctx_row, final_norm_w,
                            tm_ctx, lc, final=False)

    return x2.reshape(n_batch, ll, d)
```

```python
import functools

import numpy as np
import jax
import jax.numpy as jnp
from jax import lax
from jax.experimental import pallas as pl
from jax.experimental.pallas import tpu as pltpu

F32 = jnp.float32
BF16 = jnp.bfloat16

D_MODEL = 1024
GRID_W = 64
D_RNN = 1024
RNN_BLOCKS = 8
RNN_BLOCK_W = D_RNN // RNN_BLOCKS
RNN_CONV_W = 4
RNN_PAD_L = 2
RG_C = 8.0
ATT_HEADS = 8
ATT_KV_HEADS = 2
ATT_HEAD_DIM = 128
ATT_GROUP = ATT_HEADS // ATT_KV_HEADS
ROPE_THETA = 10000.0
RET_HEADS = 4
RET_QK_DIM = 256
RET_V_DIM = 256
RET_CHUNK = 128
N_BRANCH = 3
D_FF = 2816
FFN_CONV_W = 3
EPS = 1e-6

COL_RX = 0
COL_RG = 1024
COL_AQ = 2048
COL_RQ = 3072
COL_RK = 4096
COL_RV = 5120
COL_RGT = 6144
COL_BG = 7168
COL_AK = 10240
COL_AV = 10496
IN_COLS = 10752
_REF_SEGMENTS = ((0, 1024), (1024, 1024), (2048, 1024), (3584, 1024), (4608, 1024), (5632, 1024),
                 (6656, 1024), (7680, 3072), (3072, 256), (3328, 256))

SUBLANES = 8
LANES = 128
MOD_ROWS = 16
VMEM_LIMIT = 52 * 1024 * 1024


def _cparams(n_grid):
    return pltpu.CompilerParams(dimension_semantics=("arbitrary",) * n_grid, vmem_limit_bytes=VMEM_LIMIT)


def _sigmoid(x):
    return 1.0 / (1.0 + jnp.exp(-x))


def _silu(x):
    return x * _sigmoid(x)


def _gelu_tanh(x):
    return x * (0.5 * (1.0 + jnp.tanh(np.sqrt(2.0 / np.pi).astype(np.float32) * (x + 0.044715 * (x * x * x)))))


def _rms_rows(x):
    return x * lax.rsqrt(jnp.mean(x * x, axis=-1, keepdims=True) + EPS)


def _rope(x, cos, sin_signed, half):
    n = x.shape[-1]
    from_lo = pltpu.roll(x, half, axis=1)
    from_hi = pltpu.roll(x, n - half, axis=1)
    lane = lax.broadcasted_iota(jnp.int32, x.shape, 1)
    partner = jnp.where((lane & half) != 0, from_lo, from_hi)
    return x * cos + partner * sin_signed


def _mod_kernel(c_ref, w_ref, b_ref, o_ref):
    sc = _silu(c_ref[...])
    o_ref[...] = jnp.dot(sc.astype(BF16), w_ref[...].astype(BF16), preferred_element_type=F32) + b_ref[...]


def _modulation(c_all, mod_w, mod_b):
    depth, d, n = mod_w.shape
    tn = 1536
    return pl.pallas_call(
        _mod_kernel,
        grid=(depth, n // tn),
        in_specs=[pl.BlockSpec((MOD_ROWS, d), lambda l, j: (0, 0)),
                  pl.BlockSpec((None, d, tn), lambda l, j: (l, 0, j)),
                  pl.BlockSpec((None, 1, tn), lambda l, j: (l, 0, j))],
        out_specs=pl.BlockSpec((None, MOD_ROWS, tn), lambda l, j: (l, 0, j)),
        out_shape=jax.ShapeDtypeStruct((depth, MOD_ROWS, n), F32),
        compiler_params=_cparams(2),
        name="modulation",
    )(c_all, mod_w, mod_b.reshape(depth, 1, n))


def _inproj_kernel(x_ref, m_ref, nw_ref, w_ref, o_ref, u_ref):
    @pl.when(pl.program_id(1) == 0)
    def _():
        y = _rms_rows(x_ref[...]) * nw_ref[...]
        u_ref[...] = (y * (1.0 + m_ref[1:2, :]) + m_ref[0:1, :]).astype(BF16)

    o_ref[...] = jnp.dot(u_ref[...], w_ref[...], preferred_element_type=F32).astype(o_ref.dtype)


def _inproj(x2d, mods, mod_row, norm_w, w, tm):
    n_tok, d = x2d.shape
    n = w.shape[1]
    tn = n // 3
    return pl.pallas_call(
        _inproj_kernel,
        grid=(n_tok // tm, n // tn),
        in_specs=[pl.BlockSpec((tm, d), lambda i, j: (i, 0)),
                  pl.BlockSpec((None, 6, d), lambda i, j: (mod_row(i), 0, 0)),
                  pl.BlockSpec((1, d), lambda i, j: (0, 0)),
                  pl.BlockSpec((d, tn), lambda i, j: (0, j))],
        out_specs=pl.BlockSpec((tm, tn), lambda i, j: (i, j)),
        out_shape=jax.ShapeDtypeStruct((n_tok, n), BF16),
        scratch_shapes=[pltpu.VMEM((tm, d), BF16)],
        compiler_params=_cparams(2),
        name="inproj",
    )(x2d, mods, norm_w.reshape(1, d), w)


RNN_CHUNK = 256


RNN_SEGMENTS = 32
RNN_SEG_VREGS = RNN_SEGMENTS // SUBLANES
TINY = 1e-37


def _seg_pitch(n_rows):
    p = -(-n_rows // RNN_SEGMENTS)
    while p % 8 != 4:
        p += 1
    return p


def _rglru_kernel(xc_ref, xl_ref, gc_ref, gl_ref, cw_ref, cb_ref, gw_ref, gb_ref, lam_ref,
                  yc_ref, yl_ref, xs, af, hf, ab, hb, of, ob, tot, cin, *, lc, ll, pitch):
    w = xs.shape[1]
    tc = RNN_CHUNK
    pad = SUBLANES
    s_len = lc + ll
    n_rows = RNN_SEGMENTS * pitch
    for ref in (af, hf, ab, hb):
        ref[s_len:n_rows, :] = jnp.zeros((n_rows - s_len, w), F32)
    zero_pad = jnp.zeros((pad, w), F32)
    c_base = pad
    l_base = lc + 3 * pad
    xs[0:pad, :] = zero_pad
    xs[c_base + lc:c_base + lc + 2 * pad, :] = jnp.zeros((2 * pad, w), F32)
    xs[l_base + ll:l_base + ll + pad, :] = zero_pad
    xs[c_base:c_base + lc, :] = xc_ref[...].astype(F32)
    xs[l_base:l_base + ll, :] = xl_ref[...].astype(F32)

    lam = lam_ref[...]
    sp = jnp.maximum(-lam, 0.0) + jnp.log(1.0 + jnp.exp(-jnp.abs(lam)))
    half_rate_f = sp[0:1, :] * (-0.5 * RG_C * LOG2E)
    half_rate_b = sp[1:2, :] * (-0.5 * RG_C * LOG2E)
    cw = cw_ref[...]
    cb = cb_ref[...]
    gw_half = (gw_ref[...].astype(F32) * 0.5).astype(BF16)
    gb_half = gb_ref[...] * 0.5

    def gates_chunk(xs_base, base_f, base_b, t0):
        x = cb + jnp.zeros((tc, w), F32)
        for k in range(RNN_CONV_W):
            x = x + cw[k:k + 1, :] * xs[pl.ds(xs_base + t0 + (k - RNN_PAD_L), tc), :]
        g = jnp.dot(x.astype(BF16), gw_half, preferred_element_type=F32) + gb_half
        x_half = 0.5 * x
        for direction, (a_ref, b_ref, half_rate, base) in enumerate(
                ((af, hf, half_rate_f, base_f), (ab, hb, half_rate_b, base_b))):
            t_r = jnp.tanh(g[:, (2 * direction) * w:(2 * direction + 1) * w])
            t_i = jnp.tanh(g[:, (2 * direction + 1) * w:(2 * direction + 2) * w])
            a = jnp.exp2(t_r * half_rate + half_rate)
            z = 1.0 - a * a
            rows = pl.ds(pl.multiple_of(base + t0, SUBLANES), tc)
            a_ref[rows, :] = a
            b_ref[rows, :] = (z * lax.rsqrt(jnp.maximum(z, TINY))) * (t_i * x_half + x_half)

    def ctx_body(c, carry):
        gates_chunk(c_base, 0, ll, c * tc)
        return carry

    def lat_body(c, carry):
        gates_chunk(l_base, lc, 0, c * tc)
        return carry

    lax.fori_loop(0, lc // tc, ctx_body, 0)
    lax.fori_loop(0, ll // tc, lat_body, 0, unroll=2)

    def seg_rows(m, k):
        return pl.ds(m * SUBLANES * pitch + k, SUBLANES, stride=pitch)

    nv = RNN_SEG_VREGS
    ones = jnp.ones((SUBLANES, w), F32)
    zeros = jnp.zeros((SUBLANES, w), F32)

    def totals_body(k, carry):
        pf, tf, pb, tb = carry
        kb = pitch - 1 - k
        npf, ntf, npb, ntb = [], [], [], []
        for m in range(nv):
            a = af[seg_rows(m, k), :]
            npf.append(a * pf[m])
            ntf.append(a * tf[m] + hf[seg_rows(m, k), :])
            a = ab[seg_rows(m, kb), :]
            npb.append(a * pb[m])
            ntb.append(a * tb[m] + hb[seg_rows(m, kb), :])
        return tuple(npf), tuple(ntf), tuple(npb), tuple(ntb)

    pf, tf, pb, tb = lax.fori_loop(0, pitch, totals_body,
                                   ((ones,) * nv, (zeros,) * nv, (ones,) * nv, (zeros,) * nv), unroll=2)
    for m in range(nv):
        rows = slice(m * SUBLANES, (m + 1) * SUBLANES)
        tot[0, rows, :] = pf[m]
        tot[1, rows, :] = tf[m]
        tot[2, rows, :] = pb[m]
        tot[3, rows, :] = tb[m]

    c = jnp.zeros((1, w), F32)
    for j in range(RNN_SEGMENTS):
        cin[0, j:j + 1, :] = c
        c = tot[0, j:j + 1, :] * c + tot[1, j:j + 1, :]
    c = jnp.zeros((1, w), F32)
    for j in reversed(range(RNN_SEGMENTS)):
        cin[1, j:j + 1, :] = c
        c = tot[2, j:j + 1, :] * c + tot[3, j:j + 1, :]

    def scan_body(k, carry):
        h_f, h_b = carry
        kb = pitch - 1 - k
        nf, nb_ = [], []
        for m in range(nv):
            h = af[seg_rows(m, k), :] * h_f[m] + hf[seg_rows(m, k), :]
            of[seg_rows(m, k), :] = h
            nf.append(h)
            h = ab[seg_rows(m, kb), :] * h_b[m] + hb[seg_rows(m, kb), :]
            ob[seg_rows(m, kb), :] = h
            nb_.append(h)
        return tuple(nf), tuple(nb_)

    h0_f = tuple(cin[0, m * SUBLANES:(m + 1) * SUBLANES, :] for m in range(nv))
    h0_b = tuple(cin[1, m * SUBLANES:(m + 1) * SUBLANES, :] for m in range(nv))
    lax.fori_loop(0, pitch, scan_body, (h0_f, h0_b), unroll=2)

    def out_ctx(c, carry):
        rows = pl.ds(pl.multiple_of(c * tc, SUBLANES), tc)
        brows = pl.ds(pl.multiple_of(ll + c * tc, SUBLANES), tc)
        yc_ref[rows, :] = (_gelu_tanh(gc_ref[rows, :].astype(F32)) * (of[rows, :] + ob[brows, :])).astype(yc_ref.dtype)
        return carry

    def out_lat(c, carry):
        rows = pl.ds(pl.multiple_of(c * tc, SUBLANES), tc)
        frows = pl.ds(pl.multiple_of(lc + c * tc, SUBLANES), tc)
        yl_ref[rows, :] = (_gelu_tanh(gl_ref[rows, :].astype(F32)) * (of[frows, :] + ob[rows, :])).astype(yl_ref.dtype)
        return carry

    lax.fori_loop(0, lc // tc, out_ctx, 0)
    lax.fori_loop(0, ll // tc, out_lat, 0)


def _rglru(p_ctx, p_lat, conv_w, conv_b, gate_w, gate_b, lam, n_batch, lc, ll):
    w = RNN_BLOCK_W
    nb = RNN_BLOCKS
    s = lc + ll
    rx0 = COL_RX // w
    rg0 = COL_RG // w
    pitch = _seg_pitch(s)
    n_rows = RNN_SEGMENTS * pitch
    kern = functools.partial(_rglru_kernel, lc=lc, ll=ll, pitch=pitch)
    return pl.pallas_call(
        kern,
        grid=(n_batch, nb),
        in_specs=[pl.BlockSpec((lc, w), lambda b, c: (b, rx0 + c)),
                  pl.BlockSpec((ll, w), lambda b, c: (b, rx0 + c)),
                  pl.BlockSpec((lc, w), lambda b, c: (b, rg0 + c)),
                  pl.BlockSpec((ll, w), lambda b, c: (b, rg0 + c)),
                  pl.BlockSpec((RNN_CONV_W, w), lambda b, c: (0, c)),
                  pl.BlockSpec((1, w), lambda b, c: (0, c)),
                  pl.BlockSpec((None, w, 4 * w), lambda b, c: (c, 0, 0)),
                  pl.BlockSpec((None, 1, 4 * w), lambda b, c: (c, 0, 0)),
                  pl.BlockSpec((2, w), lambda b, c: (0, c))],
        out_specs=[pl.BlockSpec((lc, w), lambda b, c: (b, c)),
                   pl.BlockSpec((ll, w), lambda b, c: (b, c))],
        out_shape=[jax.ShapeDtypeStruct((n_batch * lc, D_RNN), BF16),
                   jax.ShapeDtypeStruct((n_batch * ll, D_RNN), BF16)],
        scratch_shapes=[pltpu.VMEM((s + 4 * SUBLANES, w), F32),
                        pltpu.VMEM((n_rows, w), F32), pltpu.VMEM((n_rows, w), F32),
                        pltpu.VMEM((n_rows, w), F32), pltpu.VMEM((n_rows, w), F32),
                        pltpu.VMEM((n_rows, w), F32), pltpu.VMEM((n_rows, w), F32),
                        pltpu.VMEM((4, RNN_SEGMENTS, w), F32), pltpu.VMEM((2, RNN_SEGMENTS, w), F32)],
        compiler_params=_cparams(2),
        name="rglru",
    )(p_ctx, p_lat, p_ctx, p_lat, conv_w, conv_b, gate_w, gate_b, lam)


ATT_TQ = 256
ATT_KEY_CHUNK = 1024
LOG2E = 1.4426950408889634
ATT_SAFE_SPREAD = 96.0


def _attn_lat_kernel(q_ref, qn_ref, kc_ref, vc_ref, kl_ref, vl_ref, cq_ref, sq_ref, cqn_ref, sqn_ref, ck_ref, sk_ref,
                     qw_ref, kw_ref, o_ref, k_s, v_s, q_cur, q_nxt, qnorm_cur, qnorm_nxt, p_s, sc_s, *, lc):
    hd = ATT_HEAD_DIM
    half = hd // 4
    tq = q_ref.shape[0]
    s_len = k_s.shape[0]
    i = pl.program_id(2)
    qw = qw_ref[...] * (hd ** -0.5 * LOG2E)

    def prep_q(src_ref, cos_ref, sin_ref, q_dst, qnorm_dst, sc_idx):
        cos = cos_ref[...]
        sin = sin_ref[...]
        qmax = None
        for g in range(ATT_GROUP):
            rows = slice(g * tq, (g + 1) * tq)
            q = _rms_rows(src_ref[:, g * hd:(g + 1) * hd].astype(F32)) * qw
            q = _rope(q, cos, sin, half).astype(BF16)
            q_dst[rows, :] = q
            q32 = q.astype(F32)
            qn = jnp.sqrt(jnp.sum(q32 * q32, axis=-1, keepdims=True))
            qnorm_dst[rows, :] = jnp.broadcast_to(qn, (tq, hd))
            qmax = jnp.max(qn) if qmax is None else jnp.maximum(qmax, jnp.max(qn))
        sc_s[sc_idx] = qmax

    @pl.when(i == 0)
    def _():
        prep_q(q_ref, cq_ref, sq_ref, q_cur, qnorm_cur, 1)
        kw = kw_ref[...]
        kc = (_rms_rows(kc_ref[...].astype(F32)) * kw).astype(BF16)
        kl = _rope(_rms_rows(kl_ref[...].astype(F32)) * kw, ck_ref[...], sk_ref[...], half).astype(BF16)
        k_s[0:lc, :] = kc
        k_s[lc:, :] = kl
        v_s[0:lc, 0:hd] = vc_ref[...]
        v_s[lc:, 0:hd] = vl_ref[...]
        v_s[:, hd:] = jnp.ones((s_len, hd), BF16)
        kc32 = kc.astype(F32)
        kl32 = kl.astype(F32)
        kn_c = jnp.max(jnp.sqrt(jnp.sum(kc32 * kc32, axis=-1, keepdims=True)))
        kn_l = jnp.max(jnp.sqrt(jnp.sum(kl32 * kl32, axis=-1, keepdims=True)))
        sc_s[0] = jnp.maximum(kn_c, kn_l)

    kmax = sc_s[0]
    safe = 2.0 * sc_s[1] * kmax < ATT_SAFE_SPREAD

    @pl.when(safe)
    def _():
        prep_q(qn_ref, cqn_ref, sqn_ref, q_nxt, qnorm_nxt, 2)
        q_all = q_cur[...]
        bound = qnorm_cur[...] * kmax
        for c0 in range(0, s_len, ATT_KEY_CHUNK):
            n = min(ATT_KEY_CHUNK, s_len - c0)
            s = lax.dot_general(q_all, k_s[c0:c0 + n, :], (((1,), (1,)), ((), ())), preferred_element_type=F32)
            p_s[:, c0:c0 + n] = jnp.exp2(s - pltpu.repeat(bound, n // hd, axis=1)).astype(BF16)
        acc = jnp.dot(p_s[...], v_s[...], preferred_element_type=F32)
        for g in range(ATT_GROUP):
            a = acc[g * tq:(g + 1) * tq, :]
            o_ref[:, g * hd:(g + 1) * hd] = (a[:, 0:hd] / a[:, hd:]).astype(o_ref.dtype)

    @pl.when(jnp.logical_not(safe))
    def _():
        prep_q(qn_ref, cqn_ref, sqn_ref, q_nxt, qnorm_nxt, 2)
        k_all = k_s[...]
        v_all = v_s[:, 0:hd]
        for g in range(ATT_GROUP):
            s = lax.dot_general(q_cur[g * tq:(g + 1) * tq, :], k_all, (((1,), (1,)), ((), ())),
                                preferred_element_type=F32)
            p = jnp.exp2(s - jnp.max(s, axis=-1, keepdims=True))
            denom = jnp.sum(p, axis=-1, keepdims=True)
            o = jnp.dot(p.astype(BF16), v_all, preferred_element_type=F32)
            o_ref[:, g * hd:(g + 1) * hd] = (o / denom).astype(o_ref.dtype)

    q_cur[...] = q_nxt[...]
    qnorm_cur[...] = qnorm_nxt[...]
    sc_s[1] = sc_s[2]


def _attn_ctx_kernel(q_ref, kc_ref, vc_ref, qw_ref, kw_ref, o_ref):
    hd = ATT_HEAD_DIM
    k = (_rms_rows(kc_ref[...].astype(F32)) * kw_ref[...]).astype(BF16)
    qw = qw_ref[...] * (hd ** -0.5)
    v_c = vc_ref[...]
    for g in range(ATT_GROUP):
        q = (_rms_rows(q_ref[:, g * hd:(g + 1) * hd].astype(F32)) * qw).astype(BF16)
        s = lax.dot_general(q, k, (((1,), (1,)), ((), ())), preferred_element_type=F32)
        m = jnp.max(s, axis=-1, keepdims=True)
        p = jnp.exp(s - m)
        denom = jnp.sum(p, axis=-1, keepdims=True)
        o = jnp.dot(p.astype(BF16), v_c, preferred_element_type=F32)
        o_ref[:, g * hd:(g + 1) * hd] = (o / denom).astype(o_ref.dtype)


def _attn_lat(p_ctx, p_lat, cos, sin, q_norm_w, k_norm_w, n_batch, lc, ll):
    hd = ATT_HEAD_DIM
    gw = ATT_GROUP * hd
    tq = ATT_TQ
    nq = ll // tq
    q0 = COL_AQ // gw
    k0 = COL_AK // hd
    v0 = COL_AV // hd
    kern = functools.partial(_attn_lat_kernel, lc=lc)

    def nxt(i):
        return jnp.minimum(i + 1, nq - 1)

    return pl.pallas_call(
        kern,
        grid=(n_batch, ATT_KV_HEADS, nq),
        in_specs=[pl.BlockSpec((tq, gw), lambda b, h, i: (b * nq + i, q0 + h)),
                  pl.BlockSpec((tq, gw), lambda b, h, i: (b * nq + nxt(i), q0 + h)),
                  pl.BlockSpec((lc, hd), lambda b, h, i: (b, k0 + h)),
                  pl.BlockSpec((lc, hd), lambda b, h, i: (b, v0 + h)),
                  pl.BlockSpec((ll, hd), lambda b, h, i: (b, k0 + h)),
                  pl.BlockSpec((ll, hd), lambda b, h, i: (b, v0 + h)),
                  pl.BlockSpec((tq, hd), lambda b, h, i: (i, 0)),
                  pl.BlockSpec((tq, hd), lambda b, h, i: (i, 0)),
                  pl.BlockSpec((tq, hd), lambda b, h, i: (nxt(i), 0)),
                  pl.BlockSpec((tq, hd), lambda b, h, i: (nxt(i), 0)),
                  pl.BlockSpec((ll, hd), lambda b, h, i: (0, 0)),
                  pl.BlockSpec((ll, hd), lambda b, h, i: (0, 0)),
                  pl.BlockSpec((1, hd), lambda b, h, i: (0, 0)),
                  pl.BlockSpec((1, hd), lambda b, h, i: (0, 0))],
        out_specs=pl.BlockSpec((tq, gw), lambda b, h, i: (b * nq + i, h)),
        out_shape=jax.ShapeDtypeStruct((n_batch * ll, ATT_HEADS * hd), BF16),
        scratch_shapes=[pltpu.VMEM((lc + ll, hd), BF16),
                        pltpu.VMEM((lc + ll, 2 * hd), BF16),
                        pltpu.VMEM((ATT_GROUP * tq, hd), BF16), pltpu.VMEM((ATT_GROUP * tq, hd), BF16),
                        pltpu.VMEM((ATT_GROUP * tq, hd), F32), pltpu.VMEM((ATT_GROUP * tq, hd), F32),
                        pltpu.VMEM((ATT_GROUP * tq, lc + ll), BF16),
                        pltpu.SMEM((3,), F32)],
        compiler_params=_cparams(3),
        name="attn_lat",
    )(p_lat, p_lat, p_ctx, p_ctx, p_lat, p_lat, cos, sin, cos, sin, cos, sin, q_norm_w, k_norm_w)


def _attn_ctx(p_ctx, q_norm_w, k_norm_w, n_batch, lc):
    hd = ATT_HEAD_DIM
    gw = ATT_GROUP * hd
    q0 = COL_AQ // gw
    k0 = COL_AK // hd
    v0 = COL_AV // hd
    return pl.pallas_call(
        _attn_ctx_kernel,
        grid=(n_batch, ATT_KV_HEADS),
        in_specs=[pl.BlockSpec((lc, gw), lambda b, h: (b, q0 + h)),
                  pl.BlockSpec((lc, hd), lambda b, h: (b, k0 + h)),
                  pl.BlockSpec((lc, hd), lambda b, h: (b, v0 + h)),
                  pl.BlockSpec((1, hd), lambda b, h: (0, 0)),
                  pl.BlockSpec((1, hd), lambda b, h: (0, 0))],
        out_specs=pl.BlockSpec((lc, gw), lambda b, h: (b, h)),
        out_shape=jax.ShapeDtypeStruct((n_batch * lc, ATT_HEADS * hd), BF16),
        compiler_params=_cparams(2),
        name="attn_ctx",
    )(p_ctx, p_ctx, p_ctx, q_norm_w, k_norm_w)


def _ret_kernel(qc_ref, kc_ref, vc_ref, gc_ref, ql_ref, kl_ref, vl_ref, gl_ref, cos_ref, sin_ref, dec_ref,
                nw_ref, yc_ref, yl_ref, ob_s, sf_s, sb_s, *, lc, ll):
    c = RET_CHUNK
    dk = RET_QK_DIM
    dv = RET_V_DIM
    half = dk // 4
    dec = dec_ref[...]
    log_g = jnp.minimum(dec, 0.0) - jnp.log(1.0 + jnp.exp(-jnp.abs(dec)))
    lg_f = log_g[0:1, :]
    lg_b = log_g[1:2, :]
    pos = lax.broadcasted_iota(jnp.int32, (c, dk), 0).astype(F32)
    qd_f = jnp.exp((pos + 1.0) * lg_f)
    kd_f = jnp.exp((c - 1.0 - pos) * lg_f)
    cd_f = jnp.exp(float(c) * lg_f)
    qd_b = jnp.exp((c - pos) * lg_b)
    kd_b = jnp.exp(pos * lg_b)
    cd_b = jnp.exp(float(c) * lg_b)
    ii = lax.broadcasted_iota(jnp.int32, (c, c), 0)
    jj = lax.broadcasted_iota(jnp.int32, (c, c), 1)
    diff = (ii - jj).astype(F32)
    intra = jnp.where(ii >= jj, jnp.exp(jnp.maximum(diff, 0.0) * lg_f[:, 0:c]),
                      jnp.exp(jnp.maximum(-diff, 0.0) * lg_b[:, 0:c]))
    nw = nw_ref[...]
    k_scale = dk ** -0.5

    def qk_ctx(rows):
        return qc_ref[rows, :].astype(F32), kc_ref[rows, :].astype(F32) * k_scale

    def qk_lat(rows):
        cos = cos_ref[rows, :]
        sin = sin_ref[rows, :]
        q = _rope(ql_ref[rows, :].astype(F32), cos, sin, half)
        k = _rope(kl_ref[rows, :].astype(F32), cos, sin, half) * k_scale
        return q, k

    def state_update(s_ref, k_dec, v, c_dec):
        kv = lax.dot_general(k_dec.astype(BF16), v, (((0,), (0,)), ((), ())), preferred_element_type=F32)
        s_ref[...] = s_ref[...] * c_dec + kv

    def bwd_chunk(q, k, v, orows):
        ob_s[orows, :] = jnp.dot((q * qd_b).astype(BF16), sb_s[...].astype(BF16), preferred_element_type=F32)
        state_update(sb_s, k * kd_b, v, cd_b)

    def fwd_chunk(q, k, v, gate, orows):
        att = lax.dot_general(q.astype(BF16), k.astype(BF16), (((1,), (1,)), ((), ())),
                              preferred_element_type=F32) * intra
        o = (jnp.dot(att.astype(BF16), v, preferred_element_type=F32)
             + jnp.dot((q * qd_f).astype(BF16), sf_s[...].astype(BF16), preferred_element_type=F32)
             + ob_s[orows, :])
        state_update(sf_s, k * kd_f, v, cd_f)
        mu = jnp.mean(o, axis=-1, keepdims=True)
        oc = o - mu
        var = jnp.mean(oc * oc, axis=-1, keepdims=True)
        y = oc * lax.rsqrt(var + EPS) * nw
        return _silu(gate) * y

    sf_s[...] = jnp.zeros((dk, dv), F32)
    sb_s[...] = jnp.zeros((dk, dv), F32)

    for ci in reversed(range(lc // c)):
        rows = pl.ds(ci * c, c)
        q, k = qk_ctx(rows)
        bwd_chunk(q, k, vc_ref[rows, :], rows)

    n_l = ll // c

    def bwd_body(i, carry):
        ci = n_l - 1 - i
        rows = pl.ds(pl.multiple_of(ci * c, c), c)
        orows = pl.ds(pl.multiple_of(lc + ci * c, c), c)
        q, k = qk_lat(rows)
        bwd_chunk(q, k, vl_ref[rows, :], orows)
        return carry

    lax.fori_loop(0, n_l, bwd_body, 0, unroll=4)

    for ci in range(lc // c):
        rows = pl.ds(ci * c, c)
        q, k = qk_ctx(rows)
        yc_ref[rows, :] = fwd_chunk(q, k, vc_ref[rows, :], gc_ref[rows, :].astype(F32), rows).astype(yc_ref.dtype)

    def fwd_body(ci, carry):
        rows = pl.ds(pl.multiple_of(ci * c, c), c)
        orows = pl.ds(pl.multiple_of(lc + ci * c, c), c)
        q, k = qk_lat(rows)
        yl_ref[rows, :] = fwd_chunk(q, k, vl_ref[rows, :], gl_ref[rows, :].astype(F32), orows).astype(yl_ref.dtype)
        return carry

    lax.fori_loop(0, n_l, fwd_body, 0, unroll=4)


def _retention(p_ctx, p_lat, cos, sin, dec, norm_w, n_batch, lc, ll):
    dk = RET_QK_DIM
    q0 = COL_RQ // dk
    k0 = COL_RK // dk
    v0 = COL_RV // dk
    g0 = COL_RGT // dk
    kern = functools.partial(_ret_kernel, lc=lc, ll=ll)

    def col(off):
        return lambda b, h: (b, off + h)

    return pl.pallas_call(
        kern,
        grid=(n_batch, RET_HEADS),
        in_specs=[pl.BlockSpec((lc, dk), col(q0)), pl.BlockSpec((lc, dk), col(k0)),
                  pl.BlockSpec((lc, dk), col(v0)), pl.BlockSpec((lc, dk), col(g0)),
                  pl.BlockSpec((ll, dk), col(q0)), pl.BlockSpec((ll, dk), col(k0)),
                  pl.BlockSpec((ll, dk), col(v0)), pl.BlockSpec((ll, dk), col(g0)),
                  pl.BlockSpec((ll, dk), lambda b, h: (0, 0)),
                  pl.BlockSpec((ll, dk), lambda b, h: (0, 0)),
                  pl.BlockSpec((None, 2, dk), lambda b, h: (h, 0, 0)),
                  pl.BlockSpec((1, dk), lambda b, h: (0, h))],
        out_specs=[pl.BlockSpec((lc, dk), lambda b, h: (b, h)),
                   pl.BlockSpec((ll, dk), lambda b, h: (b, h))],
        out_shape=[jax.ShapeDtypeStruct((n_batch * lc, RET_HEADS * RET_V_DIM), BF16),
                   jax.ShapeDtypeStruct((n_batch * ll, RET_HEADS * RET_V_DIM), BF16)],
        scratch_shapes=[pltpu.VMEM((lc + ll, RET_V_DIM), F32),
                        pltpu.VMEM((dk, RET_V_DIM), F32), pltpu.VMEM((dk, RET_V_DIM), F32)],
        compiler_params=_cparams(2),
        name="retention",
    )(p_ctx, p_ctx, p_ctx, p_ctx, p_lat, p_lat, p_lat, p_lat, cos, sin, dec, norm_w)


def _merge_kernel(x_ref, yr_ref, ya_ref, yt_ref, g0_ref, g1_ref, g2_ref, wb_ref, wo_ref, m_ref, nw_ref,
                  xo_ref, vo_ref):
    m = (_sigmoid(g0_ref[...].astype(F32)) * jnp.dot(yr_ref[...], wb_ref[0], preferred_element_type=F32)
         + _sigmoid(g1_ref[...].astype(F32)) * jnp.dot(ya_ref[...], wb_ref[1], preferred_element_type=F32)
         + _sigmoid(g2_ref[...].astype(F32)) * jnp.dot(yt_ref[...], wb_ref[2], preferred_element_type=F32))
    out = jnp.dot(m.astype(BF16), wo_ref[...], preferred_element_type=F32)
    x1 = x_ref[...] + m_ref[2:3, :] * out
    xo_ref[...] = x1
    v = _rms_rows(x1) * nw_ref[...]
    vo_ref[...] = (v * (1.0 + m_ref[4:5, :]) + m_ref[3:4, :]).astype(vo_ref.dtype)


def _merge(x2d, y_rnn, y_att, y_ret, p, mods, mod_row, norm_w, w_branch, w_out, tm):
    n_tok, d = x2d.shape
    bg0 = COL_BG // d
    tok = lambda i: (i, 0)
    const2 = lambda i: (0, 0)
    return pl.pallas_call(
        _merge_kernel,
        grid=(n_tok // tm,),
        in_specs=[pl.BlockSpec((tm, d), tok), pl.BlockSpec((tm, d), tok), pl.BlockSpec((tm, d), tok),
                  pl.BlockSpec((tm, d), tok),
                  pl.BlockSpec((tm, d), lambda i: (i, bg0)),
                  pl.BlockSpec((tm, d), lambda i: (i, bg0 + 1)),
                  pl.BlockSpec((tm, d), lambda i: (i, bg0 + 2)),
                  pl.BlockSpec((N_BRANCH, d, d), lambda i: (0, 0, 0)),
                  pl.BlockSpec((d, d), const2),
                  pl.BlockSpec((None, 6, d), lambda i: (mod_row(i), 0, 0)),
                  pl.BlockSpec((1, d), const2)],
        out_specs=[pl.BlockSpec((tm, d), tok), pl.BlockSpec((tm, d), tok)],
        out_shape=[jax.ShapeDtypeStruct((n_tok, d), F32), jax.ShapeDtypeStruct((n_tok, d), BF16)],
        compiler_params=_cparams(1),
        name="merge",
    )(x2d, y_rnn, y_att, y_ret, p, p, p, w_branch, w_out, mods, norm_w.reshape(1, d))


def _ffn_up_kernel(v_ref, w_ref, o_ref):
    o_ref[...] = jnp.dot(v_ref[...], w_ref[...], preferred_element_type=F32).astype(o_ref.dtype)


def _ffn_up(v2d, w, tm):
    n_tok, d = v2d.shape
    n = w.shape[1]
    tn = D_FF
    return pl.pallas_call(
        _ffn_up_kernel,
        grid=(n_tok // tm, n // tn),
        in_specs=[pl.BlockSpec((tm, d), lambda i, j: (i, 0)),
                  pl.BlockSpec((d, tn), lambda i, j: (0, j))],
        out_specs=pl.BlockSpec((tm, tn), lambda i, j: (i, j)),
        out_shape=jax.ShapeDtypeStruct((n_tok, n), BF16),
        compiler_params=_cparams(2),
        name="ffn_up",
    )(v2d, w)


def _ffn_down_kernel(a_ref, b_ref, ap_ref, an_ref, cw_ref, cb_ref, wd_ref, x_ref, m_ref, fw_ref, o_ref, a_s,
                     *, tiles_per_seq, final):
    tm = a_ref.shape[0]
    ti = pl.program_id(0) % tiles_per_seq
    a = a_ref[...].astype(F32)
    prev_row = jnp.where(ti == 0, 0.0, ap_ref[SUBLANES - 1:SUBLANES, :].astype(F32))
    next_row = jnp.where(ti == tiles_per_seq - 1, 0.0, an_ref[0:1, :].astype(F32))
    row0 = jnp.minimum(pl.program_id(0), 0)
    n_slab = a_s.shape[0]
    m1, p1 = [], []
    for j in range(n_slab):
        lanes = slice(j * LANES, (j + 1) * LANES)
        a_s[j, SUBLANES:SUBLANES + tm, :] = a[:, lanes]
        a_s[j, SUBLANES - 1:SUBLANES, :] = prev_row[:, lanes]
        a_s[j, SUBLANES + tm:SUBLANES + tm + 1, :] = next_row[:, lanes]
        m1.append(a_s[j, pl.ds(row0 + (SUBLANES - 1), tm), :])
        p1.append(a_s[j, pl.ds(row0 + (SUBLANES + 1), tm), :])
    a_m1 = jnp.concatenate(m1, axis=1)
    a_p1 = jnp.concatenate(p1, axis=1)
    cw = cw_ref[...] * 0.5
    half_conv = cw[0:1, :] * a_m1 + cw[1:2, :] * a + cw[2:3, :] * a_p1 + cb_ref[...] * 0.5
    h = ((half_conv * (jnp.tanh(half_conv) + 1.0)) * b_ref[...].astype(F32)).astype(BF16)
    out = jnp.dot(h, wd_ref[...], preferred_element_type=F32)
    x2 = x_ref[...] + m_ref[5:6, :] * out
    if final:
        x2 = _rms_rows(x2) * fw_ref[...]
    o_ref[...] = x2


def _ffn_down(h, conv_w, conv_b, w_down, x2d, mods, mod_row, final_w, tm, seq_len, final):
    n_tok, d = x2d.shape
    f = D_FF
    tiles_per_seq = seq_len // tm
    hb = tm // SUBLANES
    n_hblk = n_tok // SUBLANES
    kern = functools.partial(_ffn_down_kernel, tiles_per_seq=tiles_per_seq, final=final)
    return pl.pallas_call(
        kern,
        grid=(n_tok // tm,),
        in_specs=[pl.BlockSpec((tm, f), lambda i: (i, 0)),
                  pl.BlockSpec((tm, f), lambda i: (i, 1)),
                  pl.BlockSpec((SUBLANES, f), lambda i: (jnp.maximum(i * hb - 1, 0), 0)),
                  pl.BlockSpec((SUBLANES, f), lambda i: (jnp.minimum((i + 1) * hb, n_hblk - 1), 0)),
                  pl.BlockSpec((FFN_CONV_W, f), lambda i: (0, 0)),
                  pl.BlockSpec((1, f), lambda i: (0, 0)),
                  pl.BlockSpec((f, d), lambda i: (0, 0)),
                  pl.BlockSpec((tm, d), lambda i: (i, 0)),
                  pl.BlockSpec((None, 6, d), lambda i: (mod_row(i), 0, 0)),
                  pl.BlockSpec((1, d), lambda i: (0, 0))],
        out_specs=pl.BlockSpec((tm, d), lambda i: (i, 0)),
        out_shape=jax.ShapeDtypeStruct((n_tok, d), F32),
        scratch_shapes=[pltpu.VMEM((f // LANES, tm + 2 * SUBLANES, LANES), F32)],
        compiler_params=_cparams(1),
        name="ffn_down",
    )(h, h, h, h, conv_w, conv_b.reshape(1, f), w_down, x2d, mods, final_w.reshape(1, d))


def _rope_tables(seq_len, head_dim):
    n_freq = head_dim // 4
    t = jnp.arange(seq_len, dtype=jnp.int32)
    row = (t // GRID_W).astype(F32)
    col = (t % GRID_W).astype(F32)
    inv = ROPE_THETA ** (-jnp.arange(n_freq, dtype=F32) / n_freq)
    ang = jnp.stack([row[:, None] * inv, col[:, None] * inv], axis=1)
    cos = jnp.cos(ang)
    sin = jnp.sin(ang)
    cos_t = jnp.stack([cos, cos], axis=2).reshape(seq_len, head_dim)
    sin_t = jnp.stack([-sin, sin], axis=2).reshape(seq_len, head_dim)
    return cos_t, sin_t


def kernel(x, c, ctx, c_ctx, mod_w, mod_b, norm1_w, norm2_w, w_in, rnn_conv_w, rnn_conv_b, rglru_w, rglru_b,
           rglru_lam, q_norm_w, k_norm_w, ret_decay, ret_norm_w, w_branch, w_out, ffn_up, ffn_conv_w,
           ffn_conv_b, ffn_down, final_norm_w):
    n_batch, ll, d = x.shape
    lc = ctx.shape[1]
    depth = mod_w.shape[0]
    assert d == D_MODEL and n_batch + 1 <= MOD_ROWS
    assert ll % 1024 == 0 and lc % RNN_CHUNK == 0 and lc % RET_CHUNK == 0

    cos_a, sin_a = _rope_tables(ll, ATT_HEAD_DIM)
    cos_r, sin_r = _rope_tables(ll, RET_QK_DIM)

    c_all = jnp.zeros((MOD_ROWS, d), F32).at[:n_batch].set(c).at[n_batch].set(c_ctx)
    mods = _modulation(c_all, mod_w, mod_b).reshape(depth, MOD_ROWS, 6, d)

    tm_lat = 1024
    tm_ctx = lc
    lat_tiles = ll // tm_lat
    lat_row_big = lambda i: i // lat_tiles
    ctx_row = lambda i: n_batch
    tm_mix = 512
    lat_row_mix = lambda i: i // (ll // tm_mix)

    x2 = x.reshape(n_batch * ll, d)
    cx2 = ctx.reshape(n_batch * lc, d)

    for l in range(depth):
        last = l == depth - 1
        w_in_p = jnp.concatenate([w_in[l][:, s:s + n] for s, n in _REF_SEGMENTS], axis=1).astype(BF16)
        gate_w = jnp.transpose(rglru_w[l], (2, 3, 0, 1, 4)).reshape(RNN_BLOCKS, RNN_BLOCK_W, 4 * RNN_BLOCK_W)
        gate_w = gate_w.astype(BF16)
        gate_b = jnp.transpose(rglru_b[l].reshape(2, 2, RNN_BLOCKS, RNN_BLOCK_W), (2, 0, 1, 3))
        gate_b = gate_b.reshape(RNN_BLOCKS, 1, 4 * RNN_BLOCK_W)
        dec = jnp.broadcast_to(jnp.transpose(ret_decay[l])[:, :, None], (RET_HEADS, 2, RET_QK_DIM))
        wb = w_branch[l].astype(BF16)
        wo = w_out[l].astype(BF16)
        w_up = ffn_up[l].astype(BF16)
        w_dn = ffn_down[l].astype(BF16)
        ml = mods[l]

        p_lat = _inproj(x2, ml, lat_row_big, norm1_w[l], w_in_p, tm_lat)
        p_ctx = _inproj(cx2, ml, ctx_row, norm1_w[l], w_in_p, tm_ctx)

        yr_c, yr_l = _rglru(p_ctx, p_lat, rnn_conv_w[l], rnn_conv_b[l].reshape(1, D_RNN), gate_w, gate_b,
                            rglru_lam[l], n_batch, lc, ll)
        ya_l = _attn_lat(p_ctx, p_lat, cos_a, sin_a, q_norm_w[l].reshape(1, -1), k_norm_w[l].reshape(1, -1),
                         n_batch, lc, ll)
        yt_c, yt_l = _retention(p_ctx, p_lat, cos_r, sin_r, dec, ret_norm_w[l].reshape(1, -1), n_batch, lc, ll)

        x2, v_lat = _merge(x2, yr_l, ya_l, yt_l, p_lat, ml, lat_row_mix, norm2_w[l], wb, wo, tm_mix)
        h_lat = _ffn_up(v_lat, w_up, tm_lat)
        x2 = _ffn_down(h_lat, ffn_conv_w[l], ffn_conv_b[l], w_dn, x2, ml, lat_row_mix, final_norm_w,
                       tm_mix, ll, final=last)

        if not last:
            ya_c = _attn_ctx(p_ctx, q_norm_w[l].reshape(1, -1), k_norm_w[l].reshape(1, -1), n_batch, lc)
            cx2, v_ctx = _merge(cx2, yr_c, ya_c, yt_c, p_ctx, ml, ctx_row, norm2_w[l], wb, wo, tm_ctx)
            h_ctx = _ffn_up(v_ctx, w_up, tm_ctx)
            cx2 = _ffn_down(h_ctx, ffn_conv_w[l], ffn_conv_b[l], w_dn, cx2, ml, ctx_row, final_norm_w,
                            tm_ctx, lc, final=False)

    return x2.reshape(n_batch, ll, d)
```

```python
import functools

import numpy as np
import jax
import jax.numpy as jnp
from jax import lax
from jax.experimental import pallas as pl
from jax.experimental.pallas import tpu as pltpu

F32 = jnp.float32
BF16 = jnp.bfloat16

D_MODEL = 1024
GRID_W = 64
D_RNN = 1024
RNN_BLOCKS = 8
RNN_BLOCK_W = D_RNN // RNN_BLOCKS
RNN_CONV_W = 4
RNN_PAD_L = 2
RG_C = 8.0
ATT_HEADS = 8
ATT_KV_HEADS = 2
ATT_HEAD_DIM = 128
ATT_GROUP = ATT_HEADS // ATT_KV_HEADS
ROPE_THETA = 10000.0
RET_HEADS = 4
RET_QK_DIM = 256
RET_V_DIM = 256
RET_CHUNK = 128
N_BRANCH = 3
D_FF = 2816
FFN_CONV_W = 3
EPS = 1e-6

COL_RX = 0
COL_RG = 1024
COL_AQ = 2048
COL_RQ = 3072
COL_RK = 4096
COL_RV = 5120
COL_RGT = 6144
COL_BG = 7168
COL_AK = 10240
COL_AV = 10496
IN_COLS = 10752
_REF_SEGMENTS = ((0, 1024), (1024, 1024), (2048, 1024), (3584, 1024), (4608, 1024), (5632, 1024),
                 (6656, 1024), (7680, 3072), (3072, 256), (3328, 256))

SUBLANES = 8
LANES = 128
MOD_ROWS = 16
VMEM_LIMIT = 52 * 1024 * 1024


def _cparams(n_grid):
    return pltpu.CompilerParams(dimension_semantics=("arbitrary",) * n_grid, vmem_limit_bytes=VMEM_LIMIT)


def _sigmoid(x):
    return 0.5 * jnp.tanh(0.5 * x) + 0.5


def _silu(x):
    return x * _sigmoid(x)


def _gelu_tanh(x):
    return x * (0.5 * (1.0 + jnp.tanh(np.sqrt(2.0 / np.pi).astype(np.float32) * (x + 0.044715 * (x * x * x)))))


def _rms_rows(x):
    return x * lax.rsqrt(jnp.mean(x * x, axis=-1, keepdims=True) + EPS)


def _rope(x, cos, sin_signed, half):
    n = x.shape[-1]
    from_lo = pltpu.roll(x, half, axis=1)
    from_hi = pltpu.roll(x, n - half, axis=1)
    lane = lax.broadcasted_iota(jnp.int32, x.shape, 1)
    partner = jnp.where((lane & half) != 0, from_lo, from_hi)
    return x * cos + partner * sin_signed


def _mod_kernel(c_ref, w_ref, b_ref, o_ref):
    sc = _silu(c_ref[...])
    o_ref[...] = jnp.dot(sc.astype(BF16), w_ref[...].astype(BF16), preferred_element_type=F32) + b_ref[...]


def _modulation(c_all, mod_w, mod_b):
    depth, d, n = mod_w.shape
    tn = 1536
    return pl.pallas_call(
        _mod_kernel,
        grid=(depth, n // tn),
        in_specs=[pl.BlockSpec((MOD_ROWS, d), lambda l, j: (0, 0)),
                  pl.BlockSpec((None, d, tn), lambda l, j: (l, 0, j)),
                  pl.BlockSpec((None, 1, tn), lambda l, j: (l, 0, j))],
        out_specs=pl.BlockSpec((None, MOD_ROWS, tn), lambda l, j: (l, 0, j)),
        out_shape=jax.ShapeDtypeStruct((depth, MOD_ROWS, n), F32),
        compiler_params=_cparams(2),
        name="modulation",
    )(c_all, mod_w, mod_b.reshape(depth, 1, n))


def _inproj_kernel(x_ref, m_ref, nw_ref, w_ref, o_ref, u_ref):
    @pl.when(pl.program_id(1) == 0)
    def _():
        y = _rms_rows(x_ref[...]) * nw_ref[...]
        u_ref[...] = (y * (1.0 + m_ref[1:2, :]) + m_ref[0:1, :]).astype(BF16)

    o_ref[...] = jnp.dot(u_ref[...], w_ref[...], preferred_element_type=F32).astype(o_ref.dtype)


def _inproj(x2d, mods, mod_row, norm_w, w, tm):
    n_tok, d = x2d.shape
    n = w.shape[1]
    tn = n // 3
    return pl.pallas_call(
        _inproj_kernel,
        grid=(n_tok // tm, n // tn),
        in_specs=[pl.BlockSpec((tm, d), lambda i, j: (i, 0)),
                  pl.BlockSpec((None, 6, d), lambda i, j: (mod_row(i), 0, 0)),
                  pl.BlockSpec((1, d), lambda i, j: (0, 0)),
                  pl.BlockSpec((d, tn), lambda i, j: (0, j))],
        out_specs=pl.BlockSpec((tm, tn), lambda i, j: (i, j)),
        out_shape=jax.ShapeDtypeStruct((n_tok, n), BF16),
        scratch_shapes=[pltpu.VMEM((tm, d), BF16)],
        compiler_params=_cparams(2),
        name="inproj",
    )(x2d, mods, norm_w.reshape(1, d), w)


RNN_CHUNK = 256


RNN_SEGMENTS = 32
RNN_SEG_VREGS = RNN_SEGMENTS // SUBLANES
TINY = 1e-37


def _seg_pitch(n_rows):
    p = -(-n_rows // RNN_SEGMENTS)
    while p % 8 != 4:
        p += 1
    return p


def _rglru_kernel(xc_ref, xl_ref, gc_ref, gl_ref, cw_ref, cb_ref, gw_ref, gb_ref, lam_ref,
                  yc_ref, yl_ref, xs, af, hf, ab, hb, of, ob, tot, cin, *, lc, ll, pitch):
    w = xs.shape[1]
    tc = RNN_CHUNK
    pad = SUBLANES
    s_len = lc + ll
    n_rows = RNN_SEGMENTS * pitch
    for ref in (af, hf, ab, hb):
        ref[s_len:n_rows, :] = jnp.zeros((n_rows - s_len, w), F32)
    zero_pad = jnp.zeros((pad, w), F32)
    c_base = pad
    l_base = lc + 3 * pad
    xs[0:pad, :] = zero_pad
    xs[c_base + lc:c_base + lc + 2 * pad, :] = jnp.zeros((2 * pad, w), F32)
    xs[l_base + ll:l_base + ll + pad, :] = zero_pad
    xs[c_base:c_base + lc, :] = xc_ref[...].astype(F32)
    xs[l_base:l_base + ll, :] = xl_ref[...].astype(F32)

    lam = lam_ref[...]
    sp = jnp.maximum(-lam, 0.0) + jnp.log(1.0 + jnp.exp(-jnp.abs(lam)))
    half_rate_f = sp[0:1, :] * (-0.5 * RG_C * LOG2E)
    half_rate_b = sp[1:2, :] * (-0.5 * RG_C * LOG2E)
    cw = cw_ref[...]
    cb = cb_ref[...]
    gw_half = (gw_ref[...].astype(F32) * 0.5).astype(BF16)
    gb_half = gb_ref[...] * 0.5

    def gates_chunk(xs_base, base_f, base_b, t0):
        x = cb + jnp.zeros((tc, w), F32)
        for k in range(RNN_CONV_W):
            x = x + cw[k:k + 1, :] * xs[pl.ds(xs_base + t0 + (k - RNN_PAD_L), tc), :]
        g = jnp.dot(x.astype(BF16), gw_half, preferred_element_type=F32) + gb_half
        x_half = 0.5 * x
        for direction, (a_ref, b_ref, half_rate, base) in enumerate(
                ((af, hf, half_rate_f, base_f), (ab, hb, half_rate_b, base_b))):
            t_r = jnp.tanh(g[:, (2 * direction) * w:(2 * direction + 1) * w])
            t_i = jnp.tanh(g[:, (2 * direction + 1) * w:(2 * direction + 2) * w])
            a = jnp.exp2(t_r * half_rate + half_rate)
            z = 1.0 - a * a
            rows = pl.ds(pl.multiple_of(base + t0, SUBLANES), tc)
            a_ref[rows, :] = a
            b_ref[rows, :] = (z * lax.rsqrt(jnp.maximum(z, TINY))) * (t_i * x_half + x_half)

    def ctx_body(c, carry):
        gates_chunk(c_base, 0, ll, c * tc)
        return carry

    def lat_body(c, carry):
        gates_chunk(l_base, lc, 0, c * tc)
        return carry

    lax.fori_loop(0, lc // tc, ctx_body, 0)
    lax.fori_loop(0, ll // tc, lat_body, 0, unroll=2)

    def seg_rows(m, k):
        return pl.ds(m * SUBLANES * pitch + k, SUBLANES, stride=pitch)

    nv = RNN_SEG_VREGS
    ones = jnp.ones((SUBLANES, w), F32)
    zeros = jnp.zeros((SUBLANES, w), F32)

    def totals_body(k, carry):
        pf, tf, pb, tb = carry
        kb = pitch - 1 - k
        npf, ntf, npb, ntb = [], [], [], []
        for m in range(nv):
            a = af[seg_rows(m, k), :]
            npf.append(a * pf[m])
            ntf.append(a * tf[m] + hf[seg_rows(m, k), :])
            a = ab[seg_rows(m, kb), :]
            npb.append(a * pb[m])
            ntb.append(a * tb[m] + hb[seg_rows(m, kb), :])
        return tuple(npf), tuple(ntf), tuple(npb), tuple(ntb)

    pf, tf, pb, tb = lax.fori_loop(0, pitch, totals_body,
                                   ((ones,) * nv, (zeros,) * nv, (ones,) * nv, (zeros,) * nv), unroll=2)
    for m in range(nv):
        rows = slice(m * SUBLANES, (m + 1) * SUBLANES)
        tot[0, rows, :] = pf[m]
        tot[1, rows, :] = tf[m]
        tot[2, rows, :] = pb[m]
        tot[3, rows, :] = tb[m]

    c = jnp.zeros((1, w), F32)
    for j in range(RNN_SEGMENTS):
        cin[0, j:j + 1, :] = c
        c = tot[0, j:j + 1, :] * c + tot[1, j:j + 1, :]
    c = jnp.zeros((1, w), F32)
    for j in reversed(range(RNN_SEGMENTS)):
        cin[1, j:j + 1, :] = c
        c = tot[2, j:j + 1, :] * c + tot[3, j:j + 1, :]

    def scan_body(k, carry):
        h_f, h_b = carry
        kb = pitch - 1 - k
        nf, nb_ = [], []
        for m in range(nv):
            h = af[seg_rows(m, k), :] * h_f[m] + hf[seg_rows(m, k), :]
            of[seg_rows(m, k), :] = h
            nf.append(h)
            h = ab[seg_rows(m, kb), :] * h_b[m] + hb[seg_rows(m, kb), :]
            ob[seg_rows(m, kb), :] = h
            nb_.append(h)
        return tuple(nf), tuple(nb_)

    h0_f = tuple(cin[0, m * SUBLANES:(m + 1) * SUBLANES, :] for m in range(nv))
    h0_b = tuple(cin[1, m * SUBLANES:(m + 1) * SUBLANES, :] for m in range(nv))
    lax.fori_loop(0, pitch, scan_body, (h0_f, h0_b), unroll=2)

    def out_ctx(c, carry):
        rows = pl.ds(pl.multiple_of(c * tc, SUBLANES), tc)
        brows = pl.ds(pl.multiple_of(ll + c * tc, SUBLANES), tc)
        yc_ref[rows, :] = (_gelu_tanh(gc_ref[rows, :].astype(F32)) * (of[rows, :] + ob[brows, :])).astype(yc_ref.dtype)
        return carry

    def out_lat(c, carry):
        rows = pl.ds(pl.multiple_of(c * tc, SUBLANES), tc)
        frows = pl.ds(pl.multiple_of(lc + c * tc, SUBLANES), tc)
        yl_ref[rows, :] = (_gelu_tanh(gl_ref[rows, :].astype(F32)) * (of[frows, :] + ob[rows, :])).astype(yl_ref.dtype)
        return carry

    lax.fori_loop(0, lc // tc, out_ctx, 0)
    lax.fori_loop(0, ll // tc, out_lat, 0)


def _rglru(p_ctx, p_lat, conv_w, conv_b, gate_w, gate_b, lam, n_batch, lc, ll):
    w = RNN_BLOCK_W
    nb = RNN_BLOCKS
    s = lc + ll
    rx0 = COL_RX // w
    rg0 = COL_RG // w
    pitch = _seg_pitch(s)
    n_rows = RNN_SEGMENTS * pitch
    kern = functools.partial(_rglru_kernel, lc=lc, ll=ll, pitch=pitch)
    return pl.pallas_call(
        kern,
        grid=(n_batch, nb),
        in_specs=[pl.BlockSpec((lc, w), lambda b, c: (b, rx0 + c)),
                  pl.BlockSpec((ll, w), lambda b, c: (b, rx0 + c)),
                  pl.BlockSpec((lc, w), lambda b, c: (b, rg0 + c)),
                  pl.BlockSpec((ll, w), lambda b, c: (b, rg0 + c)),
                  pl.BlockSpec((RNN_CONV_W, w), lambda b, c: (0, c)),
                  pl.BlockSpec((1, w), lambda b, c: (0, c)),
                  pl.BlockSpec((None, w, 4 * w), lambda b, c: (c, 0, 0)),
                  pl.BlockSpec((None, 1, 4 * w), lambda b, c: (c, 0, 0)),
                  pl.BlockSpec((2, w), lambda b, c: (0, c))],
        out_specs=[pl.BlockSpec((lc, w), lambda b, c: (b, c)),
                   pl.BlockSpec((ll, w), lambda b, c: (b, c))],
        out_shape=[jax.ShapeDtypeStruct((n_batch * lc, D_RNN), BF16),
                   jax.ShapeDtypeStruct((n_batch * ll, D_RNN), BF16)],
        scratch_shapes=[pltpu.VMEM((s + 4 * SUBLANES, w), F32),
                        pltpu.VMEM((n_rows, w), F32), pltpu.VMEM((n_rows, w), F32),
                        pltpu.VMEM((n_rows, w), F32), pltpu.VMEM((n_rows, w), F32),
                        pltpu.VMEM((n_rows, w), F32), pltpu.VMEM((n_rows, w), F32),
                        pltpu.VMEM((4, RNN_SEGMENTS, w), F32), pltpu.VMEM((2, RNN_SEGMENTS, w), F32)],
        compiler_params=_cparams(2),
        name="rglru",
    )(p_ctx, p_lat, p_ctx, p_lat, conv_w, conv_b, gate_w, gate_b, lam)


ATT_TQ = 256
ATT_KEY_CHUNK = 1024
LOG2E = 1.4426950408889634
ATT_SAFE_SPREAD = 96.0


def _attn_lat_kernel(q_ref, qn_ref, kc_ref, vc_ref, kl_ref, vl_ref, cq_ref, sq_ref, cqn_ref, sqn_ref, ck_ref, sk_ref,
                     qw_ref, kw_ref, o_ref, k_s, v_s, q_cur, q_nxt, qnorm_cur, qnorm_nxt, p_s, sc_s, *, lc):
    hd = ATT_HEAD_DIM
    half = hd // 4
    tq = q_ref.shape[0]
    s_len = k_s.shape[0]
    i = pl.program_id(2)
    qw = qw_ref[...] * (hd ** -0.5 * LOG2E)

    def prep_q(src_ref, cos_ref, sin_ref, q_dst, qnorm_dst, sc_idx):
        cos = cos_ref[...]
        sin = sin_ref[...]
        qmax = None
        for g in range(ATT_GROUP):
            rows = slice(g * tq, (g + 1) * tq)
            q = _rms_rows(src_ref[:, g * hd:(g + 1) * hd].astype(F32)) * qw
            q = _rope(q, cos, sin, half).astype(BF16)
            q_dst[rows, :] = q
            q32 = q.astype(F32)
            qn = jnp.sqrt(jnp.sum(q32 * q32, axis=-1, keepdims=True))
            qnorm_dst[rows, :] = jnp.broadcast_to(qn, (tq, hd))
            qmax = jnp.max(qn) if qmax is None else jnp.maximum(qmax, jnp.max(qn))
        sc_s[sc_idx] = qmax

    @pl.when(i == 0)
    def _():
        prep_q(q_ref, cq_ref, sq_ref, q_cur, qnorm_cur, 1)
        kw = kw_ref[...]
        kc = (_rms_rows(kc_ref[...].astype(F32)) * kw).astype(BF16)
        kl = _rope(_rms_rows(kl_ref[...].astype(F32)) * kw, ck_ref[...], sk_ref[...], half).astype(BF16)
        k_s[0:lc, :] = kc
        k_s[lc:, :] = kl
        v_s[0:lc, 0:hd] = vc_ref[...]
        v_s[lc:, 0:hd] = vl_ref[...]
        v_s[:, hd:] = jnp.ones((s_len, hd), BF16)
        kc32 = kc.astype(F32)
        kl32 = kl.astype(F32)
        kn_c = jnp.max(jnp.sqrt(jnp.sum(kc32 * kc32, axis=-1, keepdims=True)))
        kn_l = jnp.max(jnp.sqrt(jnp.sum(kl32 * kl32, axis=-1, keepdims=True)))
        sc_s[0] = jnp.maximum(kn_c, kn_l)

    kmax = sc_s[0]
    safe = 2.0 * sc_s[1] * kmax < ATT_SAFE_SPREAD

    @pl.when(safe)
    def _():
        prep_q(qn_ref, cqn_ref, sqn_ref, q_nxt, qnorm_nxt, 2)
        q_all = q_cur[...]
        bound = qnorm_cur[...] * kmax
        for c0 in range(0, s_len, ATT_KEY_CHUNK):
            n = min(ATT_KEY_CHUNK, s_len - c0)
            s = lax.dot_general(q_all, k_s[c0:c0 + n, :], (((1,), (1,)), ((), ())), preferred_element_type=F32)
            p_s[:, c0:c0 + n] = jnp.exp2(s - jnp.concatenate([bound] * (n // hd), axis=1)).astype(BF16)
        acc = jnp.dot(p_s[...], v_s[...], preferred_element_type=F32)
        for g in range(ATT_GROUP):
            a = acc[g * tq:(g + 1) * tq, :]
            o_ref[:, g * hd:(g + 1) * hd] = (a[:, 0:hd] / a[:, hd:]).astype(o_ref.dtype)

    @pl.when(jnp.logical_not(safe))
    def _():
        prep_q(qn_ref, cqn_ref, sqn_ref, q_nxt, qnorm_nxt, 2)
        k_all = k_s[...]
        v_all = v_s[:, 0:hd]
        for g in range(ATT_GROUP):
            s = lax.dot_general(q_cur[g * tq:(g + 1) * tq, :], k_all, (((1,), (1,)), ((), ())),
                                preferred_element_type=F32)
            p = jnp.exp2(s - jnp.max(s, axis=-1, keepdims=True))
            denom = jnp.sum(p, axis=-1, keepdims=True)
            o = jnp.dot(p.astype(BF16), v_all, preferred_element_type=F32)
            o_ref[:, g * hd:(g + 1) * hd] = (o / denom).astype(o_ref.dtype)

    q_cur[...] = q_nxt[...]
    qnorm_cur[...] = qnorm_nxt[...]
    sc_s[1] = sc_s[2]


def _attn_ctx_kernel(q_ref, kc_ref, vc_ref, qw_ref, kw_ref, o_ref):
    hd = ATT_HEAD_DIM
    k = (_rms_rows(kc_ref[...].astype(F32)) * kw_ref[...]).astype(BF16)
    qw = qw_ref[...] * (hd ** -0.5)
    v_c = vc_ref[...]
    for g in range(ATT_GROUP):
        q = (_rms_rows(q_ref[:, g * hd:(g + 1) * hd].astype(F32)) * qw).astype(BF16)
        s = lax.dot_general(q, k, (((1,), (1,)), ((), ())), preferred_element_type=F32)
        m = jnp.max(s, axis=-1, keepdims=True)
        p = jnp.exp(s - m)
        denom = jnp.sum(p, axis=-1, keepdims=True)
        o = jnp.dot(p.astype(BF16), v_c, preferred_element_type=F32)
        o_ref[:, g * hd:(g + 1) * hd] = (o / denom).astype(o_ref.dtype)


def _attn_lat(p_ctx, p_lat, cos, sin, q_norm_w, k_norm_w, n_batch, lc, ll):
    hd = ATT_HEAD_DIM
    gw = ATT_GROUP * hd
    tq = ATT_TQ
    nq = ll // tq
    q0 = COL_AQ // gw
    k0 = COL_AK // hd
    v0 = COL_AV // hd
    kern = functools.partial(_attn_lat_kernel, lc=lc)

    def nxt(i):
        return jnp.minimum(i + 1, nq - 1)

    return pl.pallas_call(
        kern,
        grid=(n_batch, ATT_KV_HEADS, nq),
        in_specs=[pl.BlockSpec((tq, gw), lambda b, h, i: (b * nq + i, q0 + h)),
                  pl.BlockSpec((tq, gw), lambda b, h, i: (b * nq + nxt(i), q0 + h)),
                  pl.BlockSpec((lc, hd), lambda b, h, i: (b, k0 + h)),
                  pl.BlockSpec((lc, hd), lambda b, h, i: (b, v0 + h)),
                  pl.BlockSpec((ll, hd), lambda b, h, i: (b, k0 + h)),
                  pl.BlockSpec((ll, hd), lambda b, h, i: (b, v0 + h)),
                  pl.BlockSpec((tq, hd), lambda b, h, i: (i, 0)),
                  pl.BlockSpec((tq, hd), lambda b, h, i: (i, 0)),
                  pl.BlockSpec((tq, hd), lambda b, h, i: (nxt(i), 0)),
                  pl.BlockSpec((tq, hd), lambda b, h, i: (nxt(i), 0)),
                  pl.BlockSpec((ll, hd), lambda b, h, i: (0, 0)),
                  pl.BlockSpec((ll, hd), lambda b, h, i: (0, 0)),
                  pl.BlockSpec((1, hd), lambda b, h, i: (0, 0)),
                  pl.BlockSpec((1, hd), lambda b, h, i: (0, 0))],
        out_specs=pl.BlockSpec((tq, gw), lambda b, h, i: (b * nq + i, h)),
        out_shape=jax.ShapeDtypeStruct((n_batch * ll, ATT_HEADS * hd), BF16),
        scratch_shapes=[pltpu.VMEM((lc + ll, hd), BF16),
                        pltpu.VMEM((lc + ll, 2 * hd), BF16),
                        pltpu.VMEM((ATT_GROUP * tq, hd), BF16), pltpu.VMEM((ATT_GROUP * tq, hd), BF16),
                        pltpu.VMEM((ATT_GROUP * tq, hd), F32), pltpu.VMEM((ATT_GROUP * tq, hd), F32),
                        pltpu.VMEM((ATT_GROUP * tq, lc + ll), BF16),
                        pltpu.SMEM((3,), F32)],
        compiler_params=_cparams(3),
        name="attn_lat",
    )(p_lat, p_lat, p_ctx, p_ctx, p_lat, p_lat, cos, sin, cos, sin, cos, sin, q_norm_w, k_norm_w)


def _attn_ctx(p_ctx, q_norm_w, k_norm_w, n_batch, lc):
    hd = ATT_HEAD_DIM
    gw = ATT_GROUP * hd
    q0 = COL_AQ // gw
    k0 = COL_AK // hd
    v0 = COL_AV // hd
    return pl.pallas_call(
        _attn_ctx_kernel,
        grid=(n_batch, ATT_KV_HEADS),
        in_specs=[pl.BlockSpec((lc, gw), lambda b, h: (b, q0 + h)),
                  pl.BlockSpec((lc, hd), lambda b, h: (b, k0 + h)),
                  pl.BlockSpec((lc, hd), lambda b, h: (b, v0 + h)),
                  pl.BlockSpec((1, hd), lambda b, h: (0, 0)),
                  pl.BlockSpec((1, hd), lambda b, h: (0, 0))],
        out_specs=pl.BlockSpec((lc, gw), lambda b, h: (b, h)),
        out_shape=jax.ShapeDtypeStruct((n_batch * lc, ATT_HEADS * hd), BF16),
        compiler_params=_cparams(2),
        name="attn_ctx",
    )(p_ctx, p_ctx, p_ctx, q_norm_w, k_norm_w)


def _ret_kernel(qc_ref, kc_ref, vc_ref, gc_ref, ql_ref, kl_ref, vl_ref, gl_ref, cos_ref, sin_ref, dec_ref,
                nw_ref, yc_ref, yl_ref, ob_s, sf_s, sb_s, qb_s, kb_s, qf_s, kf_s, *, lc, ll):
    c = RET_CHUNK
    dk = RET_QK_DIM
    dv = RET_V_DIM
    half = dk // 4
    dec = dec_ref[...]
    log_g = jnp.minimum(dec, 0.0) - jnp.log(1.0 + jnp.exp(-jnp.abs(dec)))
    lg_f = log_g[0:1, :]
    lg_b = log_g[1:2, :]
    pos = lax.broadcasted_iota(jnp.int32, (c, dk), 0).astype(F32)
    qd_f = jnp.exp((pos + 1.0) * lg_f)
    kd_f = jnp.exp((c - 1.0 - pos) * lg_f)
    cd_f = jnp.exp(float(c) * lg_f)
    qd_b = jnp.exp((c - pos) * lg_b)
    kd_b = jnp.exp(pos * lg_b)
    cd_b = jnp.exp(float(c) * lg_b)
    ii = lax.broadcasted_iota(jnp.int32, (c, c), 0)
    jj = lax.broadcasted_iota(jnp.int32, (c, c), 1)
    diff = (ii - jj).astype(F32)
    intra = jnp.where(ii >= jj, jnp.exp(jnp.maximum(diff, 0.0) * lg_f[:, 0:c]),
                      jnp.exp(jnp.maximum(-diff, 0.0) * lg_b[:, 0:c]))
    nw = nw_ref[...]
    k_scale = dk ** -0.5

    def qk_ctx(rows):
        return qc_ref[rows, :].astype(F32), kc_ref[rows, :].astype(F32) * k_scale

    def qk_lat(rows):
        cos = cos_ref[rows, :]
        sin = sin_ref[rows, :]
        q = _rope(ql_ref[rows, :].astype(F32), cos, sin, half)
        k = _rope(kl_ref[rows, :].astype(F32), cos, sin, half) * k_scale
        return q, k

    def state_update(s_ref, k_dec, v, c_dec):
        kv = lax.dot_general(k_dec, v, (((0,), (0,)), ((), ())), preferred_element_type=F32)
        s_ref[...] = s_ref[...] * c_dec + kv

    def bwd_chunk(q, k, v, orows):
        qb_s[orows, :] = q.astype(BF16)
        kb_s[orows, :] = k.astype(BF16)
        qf_s[orows, :] = (q * qd_f).astype(BF16)
        kf_s[orows, :] = (k * kd_f).astype(BF16)
        ob_s[orows, :] = jnp.dot((q * qd_b).astype(BF16), sb_s[...].astype(BF16), preferred_element_type=F32)
        state_update(sb_s, (k * kd_b).astype(BF16), v, cd_b)

    def fwd_chunk(v, gate, orows):
        att = lax.dot_general(qb_s[orows, :], kb_s[orows, :], (((1,), (1,)), ((), ())),
                              preferred_element_type=F32) * intra
        o = (jnp.dot(att.astype(BF16), v, preferred_element_type=F32)
             + jnp.dot(qf_s[orows, :], sf_s[...].astype(BF16), preferred_element_type=F32)
             + ob_s[orows, :])
        state_update(sf_s, kf_s[orows, :], v, cd_f)
        mu = jnp.mean(o, axis=-1, keepdims=True)
        oc = o - mu
        var = jnp.mean(oc * oc, axis=-1, keepdims=True)
        y = oc * lax.rsqrt(var + EPS) * nw
        return _silu(gate) * y

    sf_s[...] = jnp.zeros((dk, dv), F32)
    sb_s[...] = jnp.zeros((dk, dv), F32)

    for ci in reversed(range(lc // c)):
        rows = pl.ds(ci * c, c)
        q, k = qk_ctx(rows)
        bwd_chunk(q, k, vc_ref[rows, :], rows)

    n_l = ll // c

    def bwd_body(i, carry):
        ci = n_l - 1 - i
        rows = pl.ds(pl.multiple_of(ci * c, c), c)
        orows = pl.ds(pl.multiple_of(lc + ci * c, c), c)
        q, k = qk_lat(rows)
        bwd_chunk(q, k, vl_ref[rows, :], orows)
        return carry

    lax.fori_loop(0, n_l, bwd_body, 0, unroll=4)

    for ci in range(lc // c):
        rows = pl.ds(ci * c, c)
        yc_ref[rows, :] = fwd_chunk(vc_ref[rows, :], gc_ref[rows, :].astype(F32), rows).astype(yc_ref.dtype)

    def fwd_body(ci, carry):
        rows = pl.ds(pl.multiple_of(ci * c, c), c)
        orows = pl.ds(pl.multiple_of(lc + ci * c, c), c)
        yl_ref[rows, :] = fwd_chunk(vl_ref[rows, :], gl_ref[rows, :].astype(F32), orows).astype(yl_ref.dtype)
        return carry

    lax.fori_loop(0, n_l, fwd_body, 0, unroll=8)


def _retention(p_ctx, p_lat, cos, sin, dec, norm_w, n_batch, lc, ll):
    dk = RET_QK_DIM
    q0 = COL_RQ // dk
    k0 = COL_RK // dk
    v0 = COL_RV // dk
    g0 = COL_RGT // dk
    kern = functools.partial(_ret_kernel, lc=lc, ll=ll)

    def col(off):
        return lambda b, h: (b, off + h)

    return pl.pallas_call(
        kern,
        grid=(n_batch, RET_HEADS),
        in_specs=[pl.BlockSpec((lc, dk), col(q0)), pl.BlockSpec((lc, dk), col(k0)),
                  pl.BlockSpec((lc, dk), col(v0)), pl.BlockSpec((lc, dk), col(g0)),
                  pl.BlockSpec((ll, dk), col(q0)), pl.BlockSpec((ll, dk), col(k0)),
                  pl.BlockSpec((ll, dk), col(v0)), pl.BlockSpec((ll, dk), col(g0)),
                  pl.BlockSpec((ll, dk), lambda b, h: (0, 0), pipeline_mode=pl.Buffered(1)),
                  pl.BlockSpec((ll, dk), lambda b, h: (0, 0), pipeline_mode=pl.Buffered(1)),
                  pl.BlockSpec((None, 2, dk), lambda b, h: (h, 0, 0)),
                  pl.BlockSpec((1, dk), lambda b, h: (0, h))],
        out_specs=[pl.BlockSpec((lc, dk), lambda b, h: (b, h)),
                   pl.BlockSpec((ll, dk), lambda b, h: (b, h))],
        out_shape=[jax.ShapeDtypeStruct((n_batch * lc, RET_HEADS * RET_V_DIM), BF16),
                   jax.ShapeDtypeStruct((n_batch * ll, RET_HEADS * RET_V_DIM), BF16)],
        scratch_shapes=[pltpu.VMEM((lc + ll, RET_V_DIM), F32),
                        pltpu.VMEM((dk, RET_V_DIM), F32), pltpu.VMEM((dk, RET_V_DIM), F32)]
        + [pltpu.VMEM((lc + ll, dk), BF16)] * 4,
        compiler_params=_cparams(2),
        name="retention",
    )(p_ctx, p_ctx, p_ctx, p_ctx, p_lat, p_lat, p_lat, p_lat, cos, sin, dec, norm_w)


def _merge_kernel(x_ref, yr_ref, ya_ref, yt_ref, g0_ref, g1_ref, g2_ref, wb_ref, wo_ref, m_ref, nw_ref,
                  xo_ref, vo_ref):
    m = (_sigmoid(g0_ref[...].astype(F32)) * jnp.dot(yr_ref[...], wb_ref[0], preferred_element_type=F32)
         + _sigmoid(g1_ref[...].astype(F32)) * jnp.dot(ya_ref[...], wb_ref[1], preferred_element_type=F32)
         + _sigmoid(g2_ref[...].astype(F32)) * jnp.dot(yt_ref[...], wb_ref[2], preferred_element_type=F32))
    out = jnp.dot(m.astype(BF16), wo_ref[...], preferred_element_type=F32)
    x1 = x_ref[...] + m_ref[2:3, :] * out
    xo_ref[...] = x1
    v = _rms_rows(x1) * nw_ref[...]
    vo_ref[...] = (v * (1.0 + m_ref[4:5, :]) + m_ref[3:4, :]).astype(vo_ref.dtype)


def _merge(x2d, y_rnn, y_att, y_ret, p, mods, mod_row, norm_w, w_branch, w_out, tm):
    n_tok, d = x2d.shape
    bg0 = COL_BG // d
    tok = lambda i: (i, 0)
    const2 = lambda i: (0, 0)
    return pl.pallas_call(
        _merge_kernel,
        grid=(n_tok // tm,),
        in_specs=[pl.BlockSpec((tm, d), tok), pl.BlockSpec((tm, d), tok), pl.BlockSpec((tm, d), tok),
                  pl.BlockSpec((tm, d), tok),
                  pl.BlockSpec((tm, d), lambda i: (i, bg0)),
                  pl.BlockSpec((tm, d), lambda i: (i, bg0 + 1)),
                  pl.BlockSpec((tm, d), lambda i: (i, bg0 + 2)),
                  pl.BlockSpec((N_BRANCH, d, d), lambda i: (0, 0, 0)),
                  pl.BlockSpec((d, d), const2),
                  pl.BlockSpec((None, 6, d), lambda i: (mod_row(i), 0, 0)),
                  pl.BlockSpec((1, d), const2)],
        out_specs=[pl.BlockSpec((tm, d), tok), pl.BlockSpec((tm, d), tok)],
        out_shape=[jax.ShapeDtypeStruct((n_tok, d), F32), jax.ShapeDtypeStruct((n_tok, d), BF16)],
        compiler_params=_cparams(1),
        name="merge",
    )(x2d, y_rnn, y_att, y_ret, p, p, p, w_branch, w_out, mods, norm_w.reshape(1, d))


FFN_CHUNK = 256
FFN_HALO = 16


def _ffn_kernel(v_ref, vp_ref, vn_ref, wu_ref, cw_ref, cb_ref, wd_ref, x_ref, m_ref, fw_ref, o_ref,
                vext_s, a_s, h_s, *, tiles_per_seq, final):
    tm = v_ref.shape[0]
    f = h_s.shape[1]
    fc = FFN_CHUNK
    n_slab = fc // LANES
    ti = pl.program_id(0) % tiles_per_seq
    vext_s[0:FFN_HALO, :] = vp_ref[...]
    vext_s[FFN_HALO:FFN_HALO + tm, :] = v_ref[...]
    vext_s[FFN_HALO + tm:, :] = vn_ref[...]
    v_ext = vext_s[...]
    v_main = v_ref[...]
    first = ti == 0
    last = ti == tiles_per_seq - 1
    row0 = jnp.minimum(pl.program_id(0), 0)
    cw = cw_ref[...] * 0.5
    cb = cb_ref[...] * 0.5
    for c in range(f // fc):
        cols = slice(c * fc, (c + 1) * fc)
        a_ext = jnp.dot(v_ext, wu_ref[:, cols], preferred_element_type=F32)
        b = jnp.dot(v_main, wu_ref[:, f + c * fc:f + (c + 1) * fc], preferred_element_type=F32)
        taps = [[], [], []]
        for j in range(n_slab):
            slab = (c % 2) * n_slab + j
            lanes = slice(j * LANES, (j + 1) * LANES)
            a_s[slab] = a_ext[:, lanes]
            a_s[slab, FFN_HALO - 1:FFN_HALO, :] = jnp.where(first, 0.0, a_ext[FFN_HALO - 1:FFN_HALO, lanes])
            a_s[slab, FFN_HALO + tm:FFN_HALO + tm + 1, :] = jnp.where(
                last, 0.0, a_ext[FFN_HALO + tm:FFN_HALO + tm + 1, lanes])
            for k in range(FFN_CONV_W):
                taps[k].append(a_s[slab, pl.ds(row0 + (FFN_HALO - 1 + k), tm), :])
        a_m1, a_0, a_p1 = (jnp.concatenate(t, axis=1) for t in taps)
        half_conv = cw[0:1, cols] * a_m1 + cw[1:2, cols] * a_0 + cw[2:3, cols] * a_p1 + cb[:, cols]
        h_s[:, cols] = ((half_conv * (jnp.tanh(half_conv) + 1.0)) * b).astype(BF16)
    out = jnp.dot(h_s[...], wd_ref[...], preferred_element_type=F32)
    x2 = x_ref[...] + m_ref[5:6, :] * out
    if final:
        x2 = _rms_rows(x2) * fw_ref[...]
    o_ref[...] = x2


def _ffn(v2d, w_up, conv_w, conv_b, w_down, x2d, mods, mod_row, final_w, tm, seq_len, final):
    n_tok, d = x2d.shape
    f = D_FF
    tiles_per_seq = seq_len // tm
    hb = tm // FFN_HALO
    n_hblk = n_tok // FFN_HALO
    kern = functools.partial(_ffn_kernel, tiles_per_seq=tiles_per_seq, final=final)
    resident = dict(pipeline_mode=pl.Buffered(1))
    return pl.pallas_call(
        kern,
        grid=(n_tok // tm,),
        in_specs=[pl.BlockSpec((tm, d), lambda i: (i, 0)),
                  pl.BlockSpec((FFN_HALO, d), lambda i: (jnp.maximum(i * hb - 1, 0), 0)),
                  pl.BlockSpec((FFN_HALO, d), lambda i: (jnp.minimum((i + 1) * hb, n_hblk - 1), 0)),
                  pl.BlockSpec((d, 2 * f), lambda i: (0, 0), **resident),
                  pl.BlockSpec((FFN_CONV_W, f), lambda i: (0, 0)),
                  pl.BlockSpec((1, f), lambda i: (0, 0)),
                  pl.BlockSpec((f, d), lambda i: (0, 0), **resident),
                  pl.BlockSpec((tm, d), lambda i: (i, 0)),
                  pl.BlockSpec((None, 6, d), lambda i: (mod_row(i), 0, 0)),
                  pl.BlockSpec((1, d), lambda i: (0, 0))],
        out_specs=pl.BlockSpec((tm, d), lambda i: (i, 0)),
        out_shape=jax.ShapeDtypeStruct((n_tok, d), F32),
        scratch_shapes=[pltpu.VMEM((tm + 2 * FFN_HALO, d), BF16),
                        pltpu.VMEM((2 * FFN_CHUNK // LANES, tm + 2 * FFN_HALO, LANES), F32),
                        pltpu.VMEM((tm, f), BF16)],
        compiler_params=_cparams(1),
        name="ffn",
    )(v2d, v2d, v2d, w_up, conv_w, conv_b.reshape(1, f), w_down, x2d, mods, final_w.reshape(1, d))


def _ffn_up_kernel(v_ref, w_ref, o_ref):
    o_ref[...] = jnp.dot(v_ref[...], w_ref[...], preferred_element_type=F32).astype(o_ref.dtype)


def _ffn_up(v2d, w, tm):
    n_tok, d = v2d.shape
    n = w.shape[1]
    tn = D_FF
    return pl.pallas_call(
        _ffn_up_kernel,
        grid=(n_tok // tm, n // tn),
        in_specs=[pl.BlockSpec((tm, d), lambda i, j: (i, 0)),
                  pl.BlockSpec((d, tn), lambda i, j: (0, j))],
        out_specs=pl.BlockSpec((tm, tn), lambda i, j: (i, j)),
        out_shape=jax.ShapeDtypeStruct((n_tok, n), BF16),
        compiler_params=_cparams(2),
        name="ffn_up",
    )(v2d, w)


def _ffn_down_kernel(a_ref, b_ref, ap_ref, an_ref, cw_ref, cb_ref, wd_ref, x_ref, m_ref, fw_ref, o_ref, a_s,
                     *, tiles_per_seq, final):
    tm = a_ref.shape[0]
    ti = pl.program_id(0) % tiles_per_seq
    a = a_ref[...].astype(F32)
    prev_row = jnp.where(ti == 0, 0.0, ap_ref[SUBLANES - 1:SUBLANES, :].astype(F32))
    next_row = jnp.where(ti == tiles_per_seq - 1, 0.0, an_ref[0:1, :].astype(F32))
    row0 = jnp.minimum(pl.program_id(0), 0)
    n_slab = a_s.shape[0]
    m1, p1 = [], []
    for j in range(n_slab):
        lanes = slice(j * LANES, (j + 1) * LANES)
        a_s[j, SUBLANES:SUBLANES + tm, :] = a[:, lanes]
        a_s[j, SUBLANES - 1:SUBLANES, :] = prev_row[:, lanes]
        a_s[j, SUBLANES + tm:SUBLANES + tm + 1, :] = next_row[:, lanes]
        m1.append(a_s[j, pl.ds(row0 + (SUBLANES - 1), tm), :])
        p1.append(a_s[j, pl.ds(row0 + (SUBLANES + 1), tm), :])
    a_m1 = jnp.concatenate(m1, axis=1)
    a_p1 = jnp.concatenate(p1, axis=1)
    cw = cw_ref[...] * 0.5
    half_conv = cw[0:1, :] * a_m1 + cw[1:2, :] * a + cw[2:3, :] * a_p1 + cb_ref[...] * 0.5
    h = ((half_conv * (jnp.tanh(half_conv) + 1.0)) * b_ref[...].astype(F32)).astype(BF16)
    out = jnp.dot(h, wd_ref[...], preferred_element_type=F32)
    x2 = x_ref[...] + m_ref[5:6, :] * out
    if final:
        x2 = _rms_rows(x2) * fw_ref[...]
    o_ref[...] = x2


def _ffn_down(h, conv_w, conv_b, w_down, x2d, mods, mod_row, final_w, tm, seq_len, final):
    n_tok, d = x2d.shape
    f = D_FF
    tiles_per_seq = seq_len // tm
    hb = tm // SUBLANES
    n_hblk = n_tok // SUBLANES
    kern = functools.partial(_ffn_down_kernel, tiles_per_seq=tiles_per_seq, final=final)
    return pl.pallas_call(
        kern,
        grid=(n_tok // tm,),
        in_specs=[pl.BlockSpec((tm, f), lambda i: (i, 0)),
                  pl.BlockSpec((tm, f), lambda i: (i, 1)),
                  pl.BlockSpec((SUBLANES, f), lambda i: (jnp.maximum(i * hb - 1, 0), 0)),
                  pl.BlockSpec((SUBLANES, f), lambda i: (jnp.minimum((i + 1) * hb, n_hblk - 1), 0)),
                  pl.BlockSpec((FFN_CONV_W, f), lambda i: (0, 0)),
                  pl.BlockSpec((1, f), lambda i: (0, 0)),
                  pl.BlockSpec((f, d), lambda i: (0, 0)),
                  pl.BlockSpec((tm, d), lambda i: (i, 0)),
                  pl.BlockSpec((None, 6, d), lambda i: (mod_row(i), 0, 0)),
                  pl.BlockSpec((1, d), lambda i: (0, 0))],
        out_specs=pl.BlockSpec((tm, d), lambda i: (i, 0)),
        out_shape=jax.ShapeDtypeStruct((n_tok, d), F32),
        scratch_shapes=[pltpu.VMEM((f // LANES, tm + 2 * SUBLANES, LANES), F32)],
        compiler_params=_cparams(1),
        name="ffn_down",
    )(h, h, h, h, conv_w, conv_b.reshape(1, f), w_down, x2d, mods, final_w.reshape(1, d))


def _rope_tables(seq_len, head_dim):
    n_freq = head_dim // 4
    t = jnp.arange(seq_len, dtype=jnp.int32)
    row = (t // GRID_W).astype(F32)
    col = (t % GRID_W).astype(F32)
    inv = ROPE_THETA ** (-jnp.arange(n_freq, dtype=F32) / n_freq)
    ang = jnp.stack([row[:, None] * inv, col[:, None] * inv], axis=1)
    cos = jnp.cos(ang)
    sin = jnp.sin(ang)
    cos_t = jnp.stack([cos, cos], axis=2).reshape(seq_len, head_dim)
    sin_t = jnp.stack([-sin, sin], axis=2).reshape(seq_len, head_dim)
    return cos_t, sin_t


def kernel(x, c, ctx, c_ctx, mod_w, mod_b, norm1_w, norm2_w, w_in, rnn_conv_w, rnn_conv_b, rglru_w, rglru_b,
           rglru_lam, q_norm_w, k_norm_w, ret_decay, ret_norm_w, w_branch, w_out, ffn_up, ffn_conv_w,
           ffn_conv_b, ffn_down, final_norm_w):
    n_batch, ll, d = x.shape
    lc = ctx.shape[1]
    depth = mod_w.shape[0]
    assert d == D_MODEL and n_batch + 1 <= MOD_ROWS
    assert ll % 1024 == 0 and lc % RNN_CHUNK == 0 and lc % RET_CHUNK == 0

    cos_a, sin_a = _rope_tables(ll, ATT_HEAD_DIM)
    cos_r, sin_r = _rope_tables(ll, RET_QK_DIM)

    c_all = jnp.zeros((MOD_ROWS, d), F32).at[:n_batch].set(c).at[n_batch].set(c_ctx)
    mods = _modulation(c_all, mod_w, mod_b).reshape(depth, MOD_ROWS, 6, d)

    tm_lat = 1024
    tm_ctx = lc
    tm_ctx_big = next(t for t in (1024, 512, 256, lc) if (n_batch * lc) % t == 0)
    tm_ctx_mix = min(tm_ctx_big, 512)
    lat_tiles = ll // tm_lat
    lat_row_big = lambda i: i // lat_tiles
    ctx_row = lambda i: n_batch
    tm_mix = 512
    lat_row_mix = lambda i: i // (ll // tm_mix)

    x2 = x.reshape(n_batch * ll, d)
    cx2 = ctx.reshape(n_batch * lc, d)

    for l in range(depth):
        last = l == depth - 1
        w_in_p = jnp.concatenate([w_in[l][:, s:s + n] for s, n in _REF_SEGMENTS], axis=1).astype(BF16)
        gate_w = jnp.transpose(rglru_w[l], (2, 3, 0, 1, 4)).reshape(RNN_BLOCKS, RNN_BLOCK_W, 4 * RNN_BLOCK_W)
        gate_w = gate_w.astype(BF16)
        gate_b = jnp.transpose(rglru_b[l].reshape(2, 2, RNN_BLOCKS, RNN_BLOCK_W), (2, 0, 1, 3))
        gate_b = gate_b.reshape(RNN_BLOCKS, 1, 4 * RNN_BLOCK_W)
        dec = jnp.broadcast_to(jnp.transpose(ret_decay[l])[:, :, None], (RET_HEADS, 2, RET_QK_DIM))
        wb = w_branch[l].astype(BF16)
        wo = w_out[l].astype(BF16)
        w_up = ffn_up[l].astype(BF16)
        w_dn = ffn_down[l].astype(BF16)
        ml = mods[l]

        p_lat = _inproj(x2, ml, lat_row_big, norm1_w[l], w_in_p, tm_lat)
        p_ctx = _inproj(cx2, ml, ctx_row, norm1_w[l], w_in_p, tm_ctx_big)

        yr_c, yr_l = _rglru(p_ctx, p_lat, rnn_conv_w[l], rnn_conv_b[l].reshape(1, D_RNN), gate_w, gate_b,
                            rglru_lam[l], n_batch, lc, ll)
        ya_l = _attn_lat(p_ctx, p_lat, cos_a, sin_a, q_norm_w[l].reshape(1, -1), k_norm_w[l].reshape(1, -1),
                         n_batch, lc, ll)
        yt_c, yt_l = _retention(p_ctx, p_lat, cos_r, sin_r, dec, ret_norm_w[l].reshape(1, -1), n_batch, lc, ll)

        x2, v_lat = _merge(x2, yr_l, ya_l, yt_l, p_lat, ml, lat_row_mix, norm2_w[l], wb, wo, tm_mix)
        x2 = _ffn(v_lat, w_up, ffn_conv_w[l], ffn_conv_b[l], w_dn, x2, ml, lat_row_mix, final_norm_w,
                  tm_mix, ll, final=last)

        if not last:
            ya_c = _attn_ctx(p_ctx, q_norm_w[l].reshape(1, -1), k_norm_w[l].reshape(1, -1), n_batch, lc)
            cx2, v_ctx = _merge(cx2, yr_c, ya_c, yt_c, p_ctx, ml, ctx_row, norm2_w[l], wb, wo, tm_ctx_mix)
            cx2 = _ffn(v_ctx, w_up, ffn_conv_w[l], ffn_conv_b[l], w_dn, cx2, ml, ctx_row, final_norm_w,
                       tm_ctx, lc, final=False)

    return x2.reshape(n_batch, ll, d)
```

```python
import functools

import numpy as np
import jax
import jax.numpy as jnp
from jax import lax
from jax.experimental import pallas as pl
from jax.experimental.pallas import tpu as pltpu

F32 = jnp.float32
BF16 = jnp.bfloat16

D_MODEL = 1024
GRID_W = 64
D_RNN = 1024
RNN_BLOCKS = 8
RNN_BLOCK_W = D_RNN // RNN_BLOCKS
RNN_CONV_W = 4
RNN_PAD_L = 2
RG_C = 8.0
ATT_HEADS = 8
ATT_KV_HEADS = 2
ATT_HEAD_DIM = 128
ATT_GROUP = ATT_HEADS // ATT_KV_HEADS
ROPE_THETA = 10000.0
RET_HEADS = 4
RET_QK_DIM = 256
RET_V_DIM = 256
RET_CHUNK = 128
N_BRANCH = 3
D_FF = 2816
FFN_CONV_W = 3
EPS = 1e-6

COL_RX = 0
COL_RG = 1024
COL_AQ = 2048
COL_RQ = 3072
COL_RK = 4096
COL_RV = 5120
COL_RGT = 6144
COL_BG = 7168
COL_AK = 10240
COL_AV = 10496
IN_COLS = 10752
_REF_SEGMENTS = ((0, 1024), (1024, 1024), (2048, 1024), (3584, 1024), (4608, 1024), (5632, 1024),
                 (6656, 1024), (7680, 3072), (3072, 256), (3328, 256))

SUBLANES = 8
LANES = 128
MOD_ROWS = 16
VMEM_LIMIT = 52 * 1024 * 1024


def _cparams(n_grid):
    return pltpu.CompilerParams(dimension_semantics=("arbitrary",) * n_grid, vmem_limit_bytes=VMEM_LIMIT)


def _sigmoid(x):
    return 0.5 * jnp.tanh(0.5 * x) + 0.5


def _silu(x):
    return x * _sigmoid(x)


def _gelu_tanh(x):
    c = float(np.sqrt(2.0 / np.pi))
    half = 0.5 * x
    return half * jnp.tanh(x * (c + (c * 0.044715) * (x * x))) + half


def _rms_rows(x):
    return x * lax.rsqrt(jnp.mean(x * x, axis=-1, keepdims=True) + EPS)


def _rope(x, cos, sin_signed, half):
    n = x.shape[-1]
    from_lo = pltpu.roll(x, half, axis=1)
    from_hi = pltpu.roll(x, n - half, axis=1)
    lane = lax.broadcasted_iota(jnp.int32, x.shape, 1)
    partner = jnp.where((lane & half) != 0, from_lo, from_hi)
    return x * cos + partner * sin_signed


def _mod_kernel(c_ref, w_ref, b_ref, o_ref):
    sc = _silu(c_ref[...])
    o_ref[...] = jnp.dot(sc.astype(BF16), w_ref[...].astype(BF16), preferred_element_type=F32) + b_ref[...]


def _modulation(c_all, mod_w, mod_b):
    depth, d, n = mod_w.shape
    tn = 1536
    return pl.pallas_call(
        _mod_kernel,
        grid=(depth, n // tn),
        in_specs=[pl.BlockSpec((MOD_ROWS, d), lambda l, j: (0, 0)),
                  pl.BlockSpec((None, d, tn), lambda l, j: (l, 0, j)),
                  pl.BlockSpec((None, 1, tn), lambda l, j: (l, 0, j))],
        out_specs=pl.BlockSpec((None, MOD_ROWS, tn), lambda l, j: (l, 0, j)),
        out_shape=jax.ShapeDtypeStruct((depth, MOD_ROWS, n), F32),
        compiler_params=_cparams(2),
        name="modulation",
    )(c_all, mod_w, mod_b.reshape(depth, 1, n))


def _inproj_kernel(x_ref, m_ref, nw_ref, w_ref, o_ref, u_ref):
    @pl.when(pl.program_id(1) == 0)
    def _():
        y = _rms_rows(x_ref[...]) * nw_ref[...]
        u_ref[...] = (y * (1.0 + m_ref[1:2, :]) + m_ref[0:1, :]).astype(BF16)

    o_ref[...] = jnp.dot(u_ref[...], w_ref[...], preferred_element_type=F32).astype(o_ref.dtype)


def _inproj(x2d, mods, mod_row, norm_w, w, tm):
    n_tok, d = x2d.shape
    n = w.shape[1]
    tn = n // 3
    return pl.pallas_call(
        _inproj_kernel,
        grid=(n_tok // tm, n // tn),
        in_specs=[pl.BlockSpec((tm, d), lambda i, j: (i, 0)),
                  pl.BlockSpec((None, 6, d), lambda i, j: (mod_row(i), 0, 0)),
                  pl.BlockSpec((1, d), lambda i, j: (0, 0)),
                  pl.BlockSpec((d, tn), lambda i, j: (0, j))],
        out_specs=pl.BlockSpec((tm, tn), lambda i, j: (i, j)),
        out_shape=jax.ShapeDtypeStruct((n_tok, n), BF16),
        scratch_shapes=[pltpu.VMEM((tm, d), BF16)],
        compiler_params=_cparams(2),
        name="inproj",
    )(x2d, mods, norm_w.reshape(1, d), w)


RNN_CHUNK = 256


RNN_SEGMENTS = 32
RNN_SEG_VREGS = RNN_SEGMENTS // SUBLANES
TINY = 1e-37


def _seg_pitch(n_rows):
    p = -(-n_rows // RNN_SEGMENTS)
    while p % 8 != 4:
        p += 1
    return p


def _rglru_kernel(xc_ref, xl_ref, gc_ref, gl_ref, cw_ref, cb_ref, gw_ref, gb_ref, lam_ref,
                  yc_ref, yl_ref, xs, af, hf, ab, hb, of, ob, tot, cin, *, lc, ll, pitch):
    w = xs.shape[1]
    tc = RNN_CHUNK
    pad = SUBLANES
    s_len = lc + ll
    n_rows = RNN_SEGMENTS * pitch
    for ref in (af, hf, ab, hb):
        ref[s_len:n_rows, :] = jnp.zeros((n_rows - s_len, w), F32)
    zero_pad = jnp.zeros((pad, w), F32)
    c_base = pad
    l_base = lc + 3 * pad
    xs[0:pad, :] = zero_pad
    xs[c_base + lc:c_base + lc + 2 * pad, :] = jnp.zeros((2 * pad, w), F32)
    xs[l_base + ll:l_base + ll + pad, :] = zero_pad
    xs[c_base:c_base + lc, :] = xc_ref[...].astype(F32)
    xs[l_base:l_base + ll, :] = xl_ref[...].astype(F32)

    lam = lam_ref[...]
    sp = jnp.maximum(-lam, 0.0) + jnp.log(1.0 + jnp.exp(-jnp.abs(lam)))
    half_rate_f = sp[0:1, :] * (-0.5 * RG_C * LOG2E)
    half_rate_b = sp[1:2, :] * (-0.5 * RG_C * LOG2E)
    cw = cw_ref[...]
    cb = cb_ref[...]
    gw_half = (gw_ref[...].astype(F32) * 0.5).astype(BF16)
    gb_half = gb_ref[...] * 0.5

    def gates_chunk(xs_base, base_f, base_b, t0):
        x = cb + jnp.zeros((tc, w), F32)
        for k in range(RNN_CONV_W):
            x = x + cw[k:k + 1, :] * xs[pl.ds(xs_base + t0 + (k - RNN_PAD_L), tc), :]
        g = jnp.dot(x.astype(BF16), gw_half, preferred_element_type=F32) + gb_half
        x_half = 0.5 * x
        for direction, (a_ref, b_ref, half_rate, base) in enumerate(
                ((af, hf, half_rate_f, base_f), (ab, hb, half_rate_b, base_b))):
            t_r = jnp.tanh(g[:, (2 * direction) * w:(2 * direction + 1) * w])
            t_i = jnp.tanh(g[:, (2 * direction + 1) * w:(2 * direction + 2) * w])
            a = jnp.exp2(t_r * half_rate + half_rate)
            z = 1.0 - a * a
            rows = pl.ds(pl.multiple_of(base + t0, SUBLANES), tc)
            a_ref[rows, :] = a
            b_ref[rows, :] = (z * lax.rsqrt(jnp.maximum(z, TINY))) * (t_i * x_half + x_half)

    def ctx_body(c, carry):
        gates_chunk(c_base, 0, ll, c * tc)
        return carry

    def lat_body(c, carry):
        gates_chunk(l_base, lc, 0, c * tc)
        return carry

    lax.fori_loop(0, lc // tc, ctx_body, 0)
    lax.fori_loop(0, ll // tc, lat_body, 0, unroll=2)

    def seg_rows(m, k):
        return pl.ds(m * SUBLANES * pitch + k, SUBLANES, stride=pitch)

    nv = RNN_SEG_VREGS
    ones = jnp.ones((SUBLANES, w), F32)
    zeros = jnp.zeros((SUBLANES, w), F32)

    def totals_body(k, carry):
        pf, tf, pb, tb = carry
        kb = pitch - 1 - k
        npf, ntf, npb, ntb = [], [], [], []
        for m in range(nv):
            a = af[seg_rows(m, k), :]
            npf.append(a * pf[m])
            ntf.append(a * tf[m] + hf[seg_rows(m, k), :])
            a = ab[seg_rows(m, kb), :]
            npb.append(a * pb[m])
            ntb.append(a * tb[m] + hb[seg_rows(m, kb), :])
        return tuple(npf), tuple(ntf), tuple(npb), tuple(ntb)

    pf, tf, pb, tb = lax.fori_loop(0, pitch, totals_body,
                                   ((ones,) * nv, (zeros,) * nv, (ones,) * nv, (zeros,) * nv), unroll=4)
    for m in range(nv):
        rows = slice(m * SUBLANES, (m + 1) * SUBLANES)
        tot[0, rows, :] = pf[m]
        tot[1, rows, :] = tf[m]
        tot[2, rows, :] = pb[m]
        tot[3, rows, :] = tb[m]

    c = jnp.zeros((1, w), F32)
    for j in range(RNN_SEGMENTS):
        cin[0, j:j + 1, :] = c
        c = tot[0, j:j + 1, :] * c + tot[1, j:j + 1, :]
    c = jnp.zeros((1, w), F32)
    for j in reversed(range(RNN_SEGMENTS)):
        cin[1, j:j + 1, :] = c
        c = tot[2, j:j + 1, :] * c + tot[3, j:j + 1, :]

    def scan_body(k, carry):
        h_f, h_b = carry
        kb = pitch - 1 - k
        nf, nb_ = [], []
        for m in range(nv):
            h = af[seg_rows(m, k), :] * h_f[m] + hf[seg_rows(m, k), :]
            of[seg_rows(m, k), :] = h
            nf.append(h)
            h = ab[seg_rows(m, kb), :] * h_b[m] + hb[seg_rows(m, kb), :]
            ob[seg_rows(m, kb), :] = h
            nb_.append(h)
        return tuple(nf), tuple(nb_)

    h0_f = tuple(cin[0, m * SUBLANES:(m + 1) * SUBLANES, :] for m in range(nv))
    h0_b = tuple(cin[1, m * SUBLANES:(m + 1) * SUBLANES, :] for m in range(nv))
    lax.fori_loop(0, pitch, scan_body, (h0_f, h0_b), unroll=4)

    def out_ctx(c, carry):
        rows = pl.ds(pl.multiple_of(c * tc, SUBLANES), tc)
        brows = pl.ds(pl.multiple_of(ll + c * tc, SUBLANES), tc)
        yc_ref[rows, :] = (_gelu_tanh(gc_ref[rows, :].astype(F32)) * (of[rows, :] + ob[brows, :])).astype(yc_ref.dtype)
        return carry

    def out_lat(c, carry):
        rows = pl.ds(pl.multiple_of(c * tc, SUBLANES), tc)
        frows = pl.ds(pl.multiple_of(lc + c * tc, SUBLANES), tc)
        yl_ref[rows, :] = (_gelu_tanh(gl_ref[rows, :].astype(F32)) * (of[frows, :] + ob[rows, :])).astype(yl_ref.dtype)
        return carry

    lax.fori_loop(0, lc // tc, out_ctx, 0)
    lax.fori_loop(0, ll // tc, out_lat, 0)


def _rglru(p_ctx, p_lat, conv_w, conv_b, gate_w, gate_b, lam, n_batch, lc, ll):
    w = RNN_BLOCK_W
    nb = RNN_BLOCKS
    s = lc + ll
    rx0 = COL_RX // w
    rg0 = COL_RG // w
    pitch = _seg_pitch(s)
    n_rows = RNN_SEGMENTS * pitch
    kern = functools.partial(_rglru_kernel, lc=lc, ll=ll, pitch=pitch)
    return pl.pallas_call(
        kern,
        grid=(n_batch, nb),
        in_specs=[pl.BlockSpec((lc, w), lambda b, c: (b, rx0 + c)),
                  pl.BlockSpec((ll, w), lambda b, c: (b, rx0 + c)),
                  pl.BlockSpec((lc, w), lambda b, c: (b, rg0 + c)),
                  pl.BlockSpec((ll, w), lambda b, c: (b, rg0 + c)),
                  pl.BlockSpec((RNN_CONV_W, w), lambda b, c: (0, c)),
                  pl.BlockSpec((1, w), lambda b, c: (0, c)),
                  pl.BlockSpec((None, w, 4 * w), lambda b, c: (c, 0, 0)),
                  pl.BlockSpec((None, 1, 4 * w), lambda b, c: (c, 0, 0)),
                  pl.BlockSpec((2, w), lambda b, c: (0, c))],
        out_specs=[pl.BlockSpec((lc, w), lambda b, c: (b, c)),
                   pl.BlockSpec((ll, w), lambda b, c: (b, c))],
        out_shape=[jax.ShapeDtypeStruct((n_batch * lc, D_RNN), BF16),
                   jax.ShapeDtypeStruct((n_batch * ll, D_RNN), BF16)],
        scratch_shapes=[pltpu.VMEM((s + 4 * SUBLANES, w), F32),
                        pltpu.VMEM((n_rows, w), F32), pltpu.VMEM((n_rows, w), F32),
                        pltpu.VMEM((n_rows, w), F32), pltpu.VMEM((n_rows, w), F32),
                        pltpu.VMEM((n_rows, w), F32), pltpu.VMEM((n_rows, w), F32),
                        pltpu.VMEM((4, RNN_SEGMENTS, w), F32), pltpu.VMEM((2, RNN_SEGMENTS, w), F32)],
        compiler_params=_cparams(2),
        name="rglru",
    )(p_ctx, p_lat, p_ctx, p_lat, conv_w, conv_b, gate_w, gate_b, lam)


ATT_TQ = 512
ATT_KEY_CHUNK = 512
LOG2E = 1.4426950408889634
ATT_SAFE_SPREAD = 96.0


def _attn_lat_kernel(q_ref, qn_ref, kc_ref, vc_ref, kl_ref, vl_ref, cq_ref, sq_ref, cqn_ref, sqn_ref, ck_ref, sk_ref,
                     qw_ref, kw_ref, o_ref, k_s, v_s, q_cur, q_nxt, qnorm_cur, qnorm_nxt, p_s, sc_s, *, lc):
    hd = ATT_HEAD_DIM
    half = hd // 4
    tq = q_ref.shape[0]
    s_len = k_s.shape[0]
    i = pl.program_id(2)
    qw = qw_ref[...] * (hd ** -0.5 * LOG2E)

    def prep_q(src_ref, cos_ref, sin_ref, q_dst, qnorm_dst, sc_idx):
        cos = cos_ref[...]
        sin = sin_ref[...]
        qmax = None
        for g in range(ATT_GROUP):
            rows = slice(g * tq, (g + 1) * tq)
            q = _rms_rows(src_ref[:, g * hd:(g + 1) * hd].astype(F32)) * qw
            q = _rope(q, cos, sin, half).astype(BF16)
            q_dst[rows, :] = q
            q32 = q.astype(F32)
            qn = jnp.sqrt(jnp.sum(q32 * q32, axis=-1, keepdims=True))
            qnorm_dst[rows, :] = jnp.broadcast_to(qn, (tq, hd))
            qmax = jnp.max(qn) if qmax is None else jnp.maximum(qmax, jnp.max(qn))
        sc_s[sc_idx] = qmax

    @pl.when(i == 0)
    def _():
        prep_q(q_ref, cq_ref, sq_ref, q_cur, qnorm_cur, 1)
        kw = kw_ref[...]
        kc = (_rms_rows(kc_ref[...].astype(F32)) * kw).astype(BF16)
        kl = _rope(_rms_rows(kl_ref[...].astype(F32)) * kw, ck_ref[...], sk_ref[...], half).astype(BF16)
        k_s[0:lc, :] = kc
        k_s[lc:, :] = kl
        v_s[0:lc, 0:hd] = vc_ref[...]
        v_s[lc:, 0:hd] = vl_ref[...]
        v_s[:, hd:] = jnp.ones((s_len, hd), BF16)
        kc32 = kc.astype(F32)
        kl32 = kl.astype(F32)
        kn_c = jnp.max(jnp.sqrt(jnp.sum(kc32 * kc32, axis=-1, keepdims=True)))
        kn_l = jnp.max(jnp.sqrt(jnp.sum(kl32 * kl32, axis=-1, keepdims=True)))
        sc_s[0] = jnp.maximum(kn_c, kn_l)

    kmax = sc_s[0]
    safe = 2.0 * sc_s[1] * kmax < ATT_SAFE_SPREAD

    @pl.when(safe)
    def _():
        prep_q(qn_ref, cqn_ref, sqn_ref, q_nxt, qnorm_nxt, 2)
        q_all = q_cur[...]
        bound = qnorm_cur[...] * kmax
        for c0 in range(0, s_len, ATT_KEY_CHUNK):
            n = min(ATT_KEY_CHUNK, s_len - c0)
            s = lax.dot_general(q_all, k_s[c0:c0 + n, :], (((1,), (1,)), ((), ())), preferred_element_type=F32)
            p_s[:, c0:c0 + n] = jnp.exp2(s - jnp.concatenate([bound] * (n // hd), axis=1)).astype(BF16)
        acc = jnp.dot(p_s[...], v_s[...], preferred_element_type=F32)
        for g in range(ATT_GROUP):
            a = acc[g * tq:(g + 1) * tq, :]
            o_ref[:, g * hd:(g + 1) * hd] = (a[:, 0:hd] / a[:, hd:]).astype(o_ref.dtype)

    @pl.when(jnp.logical_not(safe))
    def _():
        prep_q(qn_ref, cqn_ref, sqn_ref, q_nxt, qnorm_nxt, 2)
        k_all = k_s[...]
        v_all = v_s[:, 0:hd]
        for g in range(ATT_GROUP):
            s = lax.dot_general(q_cur[g * tq:(g + 1) * tq, :], k_all, (((1,), (1,)), ((), ())),
                                preferred_element_type=F32)
            p = jnp.exp2(s - jnp.max(s, axis=-1, keepdims=True))
            denom = jnp.sum(p, axis=-1, keepdims=True)
            o = jnp.dot(p.astype(BF16), v_all, preferred_element_type=F32)
            o_ref[:, g * hd:(g + 1) * hd] = (o / denom).astype(o_ref.dtype)

    q_cur[...] = q_nxt[...]
    qnorm_cur[...] = qnorm_nxt[...]
    sc_s[1] = sc_s[2]


def _attn_ctx_kernel(q_ref, kc_ref, vc_ref, qw_ref, kw_ref, o_ref):
    hd = ATT_HEAD_DIM
    k = (_rms_rows(kc_ref[...].astype(F32)) * kw_ref[...]).astype(BF16)
    qw = qw_ref[...] * (hd ** -0.5)
    v_c = vc_ref[...]
    for g in range(ATT_GROUP):
        q = (_rms_rows(q_ref[:, g * hd:(g + 1) * hd].astype(F32)) * qw).astype(BF16)
        s = lax.dot_general(q, k, (((1,), (1,)), ((), ())), preferred_element_type=F32)
        m = jnp.max(s, axis=-1, keepdims=True)
        p = jnp.exp(s - m)
        denom = jnp.sum(p, axis=-1, keepdims=True)
        o = jnp.dot(p.astype(BF16), v_c, preferred_element_type=F32)
        o_ref[:, g * hd:(g + 1) * hd] = (o / denom).astype(o_ref.dtype)


def _attn_lat(p_ctx, p_lat, cos, sin, q_norm_w, k_norm_w, n_batch, lc, ll):
    hd = ATT_HEAD_DIM
    gw = ATT_GROUP * hd
    tq = ATT_TQ
    nq = ll // tq
    q0 = COL_AQ // gw
    k0 = COL_AK // hd
    v0 = COL_AV // hd
    kern = functools.partial(_attn_lat_kernel, lc=lc)

    def nxt(i):
        return jnp.minimum(i + 1, nq - 1)

    return pl.pallas_call(
        kern,
        grid=(n_batch, ATT_KV_HEADS, nq),
        in_specs=[pl.BlockSpec((tq, gw), lambda b, h, i: (b * nq + i, q0 + h)),
                  pl.BlockSpec((tq, gw), lambda b, h, i: (b * nq + nxt(i), q0 + h)),
                  pl.BlockSpec((lc, hd), lambda b, h, i: (b, k0 + h)),
                  pl.BlockSpec((lc, hd), lambda b, h, i: (b, v0 + h)),
                  pl.BlockSpec((ll, hd), lambda b, h, i: (b, k0 + h)),
                  pl.BlockSpec((ll, hd), lambda b, h, i: (b, v0 + h)),
                  pl.BlockSpec((tq, hd), lambda b, h, i: (i, 0)),
                  pl.BlockSpec((tq, hd), lambda b, h, i: (i, 0)),
                  pl.BlockSpec((tq, hd), lambda b, h, i: (nxt(i), 0)),
                  pl.BlockSpec((tq, hd), lambda b, h, i: (nxt(i), 0)),
                  pl.BlockSpec((ll, hd), lambda b, h, i: (0, 0), pipeline_mode=pl.Buffered(1)),
                  pl.BlockSpec((ll, hd), lambda b, h, i: (0, 0), pipeline_mode=pl.Buffered(1)),
                  pl.BlockSpec((1, hd), lambda b, h, i: (0, 0)),
                  pl.BlockSpec((1, hd), lambda b, h, i: (0, 0))],
        out_specs=pl.BlockSpec((tq, gw), lambda b, h, i: (b * nq + i, h)),
        out_shape=jax.ShapeDtypeStruct((n_batch * ll, ATT_HEADS * hd), BF16),
        scratch_shapes=[pltpu.VMEM((lc + ll, hd), BF16),
                        pltpu.VMEM((lc + ll, 2 * hd), BF16),
                        pltpu.VMEM((ATT_GROUP * tq, hd), BF16), pltpu.VMEM((ATT_GROUP * tq, hd), BF16),
                        pltpu.VMEM((ATT_GROUP * tq, hd), F32), pltpu.VMEM((ATT_GROUP * tq, hd), F32),
                        pltpu.VMEM((ATT_GROUP * tq, lc + ll), BF16),
                        pltpu.SMEM((3,), F32)],
        compiler_params=_cparams(3),
        name="attn_lat",
    )(p_lat, p_lat, p_ctx, p_ctx, p_lat, p_lat, cos, sin, cos, sin, cos, sin, q_norm_w, k_norm_w)


def _attn_ctx(p_ctx, q_norm_w, k_norm_w, n_batch, lc):
    hd = ATT_HEAD_DIM
    gw = ATT_GROUP * hd
    q0 = COL_AQ // gw
    k0 = COL_AK // hd
    v0 = COL_AV // hd
    return pl.pallas_call(
        _attn_ctx_kernel,
        grid=(n_batch, ATT_KV_HEADS),
        in_specs=[pl.BlockSpec((lc, gw), lambda b, h: (b, q0 + h)),
                  pl.BlockSpec((lc, hd), lambda b, h: (b, k0 + h)),
                  pl.BlockSpec((lc, hd), lambda b, h: (b, v0 + h)),
                  pl.BlockSpec((1, hd), lambda b, h: (0, 0)),
                  pl.BlockSpec((1, hd), lambda b, h: (0, 0))],
        out_specs=pl.BlockSpec((lc, gw), lambda b, h: (b, h)),
        out_shape=jax.ShapeDtypeStruct((n_batch * lc, ATT_HEADS * hd), BF16),
        compiler_params=_cparams(2),
        name="attn_ctx",
    )(p_ctx, p_ctx, p_ctx, q_norm_w, k_norm_w)


def _ret_kernel(qc_ref, kc_ref, vc_ref, gc_ref, ql_ref, kl_ref, vl_ref, gl_ref, cos_ref, sin_ref, dec_ref,
                nw_ref, yc_ref, yl_ref, ob_s, sf_s, sb_s, qb_s, kb_s, qf_s, kf_s, *, lc, ll):
    c = RET_CHUNK
    dk = RET_QK_DIM
    dv = RET_V_DIM
    half = dk // 4
    dec = dec_ref[...]
    log_g = jnp.minimum(dec, 0.0) - jnp.log(1.0 + jnp.exp(-jnp.abs(dec)))
    lg_f = log_g[0:1, :]
    lg_b = log_g[1:2, :]
    pos = lax.broadcasted_iota(jnp.int32, (c, dk), 0).astype(F32)
    qd_f = jnp.exp((pos + 1.0) * lg_f)
    kd_f = jnp.exp((c - 1.0 - pos) * lg_f)
    cd_f = jnp.exp(float(c) * lg_f)
    qd_b = jnp.exp((c - pos) * lg_b)
    kd_b = jnp.exp(pos * lg_b)
    cd_b = jnp.exp(float(c) * lg_b)
    ii = lax.broadcasted_iota(jnp.int32, (c, c), 0)
    jj = lax.broadcasted_iota(jnp.int32, (c, c), 1)
    diff = (ii - jj).astype(F32)
    intra = jnp.where(ii >= jj, jnp.exp(jnp.maximum(diff, 0.0) * lg_f[:, 0:c]),
                      jnp.exp(jnp.maximum(-diff, 0.0) * lg_b[:, 0:c]))
    nw = nw_ref[...]
    k_scale = dk ** -0.5

    def qk_ctx(rows):
        return qc_ref[rows, :].astype(F32), kc_ref[rows, :].astype(F32) * k_scale

    def qk_lat(rows):
        cos = cos_ref[rows, :]
        sin = sin_ref[rows, :]
        q = _rope(ql_ref[rows, :].astype(F32), cos, sin, half)
        k = _rope(kl_ref[rows, :].astype(F32), cos, sin, half) * k_scale
        return q, k

    def state_update(s_ref, k_dec, v, c_dec):
        kv = lax.dot_general(k_dec, v, (((0,), (0,)), ((), ())), preferred_element_type=F32)
        s_ref[...] = s_ref[...] * c_dec + kv

    def bwd_chunk(q, k, v, orows):
        qb_s[orows, :] = q.astype(BF16)
        kb_s[orows, :] = k.astype(BF16)
        qf_s[orows, :] = (q * qd_f).astype(BF16)
        kf_s[orows, :] = (k * kd_f).astype(BF16)
        ob_s[orows, :] = jnp.dot((q * qd_b).astype(BF16), sb_s[...].astype(BF16), preferred_element_type=F32)
        state_update(sb_s, (k * kd_b).astype(BF16), v, cd_b)

    def fwd_chunk(v, gate, orows):
        att = lax.dot_general(qb_s[orows, :], kb_s[orows, :], (((1,), (1,)), ((), ())),
                              preferred_element_type=F32) * intra
        o = (jnp.dot(att.astype(BF16), v, preferred_element_type=F32)
             + jnp.dot(qf_s[orows, :], sf_s[...].astype(BF16), preferred_element_type=F32)
             + ob_s[orows, :])
        state_update(sf_s, kf_s[orows, :], v, cd_f)
        mu = jnp.mean(o, axis=-1, keepdims=True)
        oc = o - mu
        var = jnp.mean(oc * oc, axis=-1, keepdims=True)
        y = oc * lax.rsqrt(var + EPS) * nw
        return _silu(gate) * y

    sf_s[...] = jnp.zeros((dk, dv), F32)
    sb_s[...] = jnp.zeros((dk, dv), F32)

    for ci in reversed(range(lc // c)):
        rows = pl.ds(ci * c, c)
        q, k = qk_ctx(rows)
        bwd_chunk(q, k, vc_ref[rows, :], rows)

    n_l = ll // c

    def bwd_body(i, carry):
        ci = n_l - 1 - i
        rows = pl.ds(pl.multiple_of(ci * c, c), c)
        orows = pl.ds(pl.multiple_of(lc + ci * c, c), c)
        q, k = qk_lat(rows)
        bwd_chunk(q, k, vl_ref[rows, :], orows)
        return carry

    lax.fori_loop(0, n_l, bwd_body, 0, unroll=4)

    for ci in range(lc // c):
        rows = pl.ds(ci * c, c)
        yc_ref[rows, :] = fwd_chunk(vc_ref[rows, :], gc_ref[rows, :].astype(F32), rows).astype(yc_ref.dtype)

    def fwd_body(ci, carry):
        rows = pl.ds(pl.multiple_of(ci * c, c), c)
        orows = pl.ds(pl.multiple_of(lc + ci * c, c), c)
        yl_ref[rows, :] = fwd_chunk(vl_ref[rows, :], gl_ref[rows, :].astype(F32), orows).astype(yl_ref.dtype)
        return carry

    lax.fori_loop(0, n_l, fwd_body, 0, unroll=8)


def _retention(p_ctx, p_lat, cos, sin, dec, norm_w, n_batch, lc, ll):
    dk = RET_QK_DIM
    q0 = COL_RQ // dk
    k0 = COL_RK // dk
    v0 = COL_RV // dk
    g0 = COL_RGT // dk
    kern = functools.partial(_ret_kernel, lc=lc, ll=ll)

    def col(off):
        return lambda b, h: (b, off + h)

    return pl.pallas_call(
        kern,
        grid=(n_batch, RET_HEADS),
        in_specs=[pl.BlockSpec((lc, dk), col(q0)), pl.BlockSpec((lc, dk), col(k0)),
                  pl.BlockSpec((lc, dk), col(v0)), pl.BlockSpec((lc, dk), col(g0)),
                  pl.BlockSpec((ll, dk), col(q0)), pl.BlockSpec((ll, dk), col(k0)),
                  pl.BlockSpec((ll, dk), col(v0)), pl.BlockSpec((ll, dk), col(g0)),
                  pl.BlockSpec((ll, dk), lambda b, h: (0, 0), pipeline_mode=pl.Buffered(1)),
                  pl.BlockSpec((ll, dk), lambda b, h: (0, 0), pipeline_mode=pl.Buffered(1)),
                  pl.BlockSpec((None, 2, dk), lambda b, h: (h, 0, 0)),
                  pl.BlockSpec((1, dk), lambda b, h: (0, h))],
        out_specs=[pl.BlockSpec((lc, dk), lambda b, h: (b, h)),
                   pl.BlockSpec((ll, dk), lambda b, h: (b, h))],
        out_shape=[jax.ShapeDtypeStruct((n_batch * lc, RET_HEADS * RET_V_DIM), BF16),
                   jax.ShapeDtypeStruct((n_batch * ll, RET_HEADS * RET_V_DIM), BF16)],
        scratch_shapes=[pltpu.VMEM((lc + ll, RET_V_DIM), F32),
                        pltpu.VMEM((dk, RET_V_DIM), F32), pltpu.VMEM((dk, RET_V_DIM), F32)]
        + [pltpu.VMEM((lc + ll, dk), BF16)] * 4,
        compiler_params=_cparams(2),
        name="retention",
    )(p_ctx, p_ctx, p_ctx, p_ctx, p_lat, p_lat, p_lat, p_lat, cos, sin, dec, norm_w)


def _merge_kernel(x_ref, yr_ref, ya_ref, yt_ref, g0_ref, g1_ref, g2_ref, wb_ref, wo_ref, m_ref, nw_ref,
                  xo_ref, vo_ref):
    m = (_sigmoid(g0_ref[...].astype(F32)) * jnp.dot(yr_ref[...], wb_ref[0], preferred_element_type=F32)
         + _sigmoid(g1_ref[...].astype(F32)) * jnp.dot(ya_ref[...], wb_ref[1], preferred_element_type=F32)
         + _sigmoid(g2_ref[...].astype(F32)) * jnp.dot(yt_ref[...], wb_ref[2], preferred_element_type=F32))
    out = jnp.dot(m.astype(BF16), wo_ref[...], preferred_element_type=F32)
    x1 = x_ref[...] + m_ref[2:3, :] * out
    xo_ref[...] = x1
    v = _rms_rows(x1) * nw_ref[...]
    vo_ref[...] = (v * (1.0 + m_ref[4:5, :]) + m_ref[3:4, :]).astype(vo_ref.dtype)


def _merge(x2d, y_rnn, y_att, y_ret, p, mods, mod_row, norm_w, w_branch, w_out, tm):
    n_tok, d = x2d.shape
    bg0 = COL_BG // d
    tok = lambda i: (i, 0)
    const2 = lambda i: (0, 0)
    return pl.pallas_call(
        _merge_kernel,
        grid=(n_tok // tm,),
        in_specs=[pl.BlockSpec((tm, d), tok), pl.BlockSpec((tm, d), tok), pl.BlockSpec((tm, d), tok),
                  pl.BlockSpec((tm, d), tok),
                  pl.BlockSpec((tm, d), lambda i: (i, bg0)),
                  pl.BlockSpec((tm, d), lambda i: (i, bg0 + 1)),
                  pl.BlockSpec((tm, d), lambda i: (i, bg0 + 2)),
                  pl.BlockSpec((N_BRANCH, d, d), lambda i: (0, 0, 0)),
                  pl.BlockSpec((d, d), const2),
                  pl.BlockSpec((None, 6, d), lambda i: (mod_row(i), 0, 0)),
                  pl.BlockSpec((1, d), const2)],
        out_specs=[pl.BlockSpec((tm, d), tok), pl.BlockSpec((tm, d), tok)],
        out_shape=[jax.ShapeDtypeStruct((n_tok, d), F32), jax.ShapeDtypeStruct((n_tok, d), BF16)],
        compiler_params=_cparams(1),
        name="merge",
    )(x2d, y_rnn, y_att, y_ret, p, p, p, w_branch, w_out, mods, norm_w.reshape(1, d))


FFN_CHUNK = 256
FFN_HALO = 16


def _ffn_kernel(v_ref, vp_ref, vn_ref, wu_ref, cw_ref, cb_ref, wd_ref, x_ref, m_ref, fw_ref, o_ref,
                vext_s, a_s, h_s, *, tiles_per_seq, final):
    tm = v_ref.shape[0]
    f = h_s.shape[1]
    fc = FFN_CHUNK
    n_slab = fc // LANES
    ti = pl.program_id(0) % tiles_per_seq
    vext_s[0:FFN_HALO, :] = vp_ref[...]
    vext_s[FFN_HALO:FFN_HALO + tm, :] = v_ref[...]
    vext_s[FFN_HALO + tm:, :] = vn_ref[...]
    v_ext = vext_s[...]
    v_main = v_ref[...]
    first = ti == 0
    last = ti == tiles_per_seq - 1
    row0 = jnp.minimum(pl.program_id(0), 0)
    cw = cw_ref[...] * 0.5
    cb = cb_ref[...] * 0.5
    for c in range(f // fc):
        cols = slice(c * fc, (c + 1) * fc)
        a_ext = jnp.dot(v_ext, wu_ref[:, cols], preferred_element_type=F32)
        b = jnp.dot(v_main, wu_ref[:, f + c * fc:f + (c + 1) * fc], preferred_element_type=F32)
        taps = [[], [], []]
        for j in range(n_slab):
            slab = (c % 2) * n_slab + j
            lanes = slice(j * LANES, (j + 1) * LANES)
            a_s[slab] = a_ext[:, lanes]
            a_s[slab, FFN_HALO - 1:FFN_HALO, :] = jnp.where(first, 0.0, a_ext[FFN_HALO - 1:FFN_HALO, lanes])
            a_s[slab, FFN_HALO + tm:FFN_HALO + tm + 1, :] = jnp.where(
                last, 0.0, a_ext[FFN_HALO + tm:FFN_HALO + tm + 1, lanes])
            for k in range(FFN_CONV_W):
                taps[k].append(a_s[slab, pl.ds(row0 + (FFN_HALO - 1 + k), tm), :])
        a_m1, a_0, a_p1 = (jnp.concatenate(t, axis=1) for t in taps)
        half_conv = cw[0:1, cols] * a_m1 + cw[1:2, cols] * a_0 + cw[2:3, cols] * a_p1 + cb[:, cols]
        h_s[:, cols] = ((half_conv * (jnp.tanh(half_conv) + 1.0)) * b).astype(BF16)
    out = jnp.dot(h_s[...], wd_ref[...], preferred_element_type=F32)
    x2 = x_ref[...] + m_ref[5:6, :] * out
    if final:
        x2 = _rms_rows(x2) * fw_ref[...]
    o_ref[...] = x2


def _ffn(v2d, w_up, conv_w, conv_b, w_down, x2d, mods, mod_row, final_w, tm, seq_len, final):
    n_tok, d = x2d.shape
    f = D_FF
    tiles_per_seq = seq_len // tm
    hb = tm // FFN_HALO
    n_hblk = n_tok // FFN_HALO
    kern = functools.partial(_ffn_kernel, tiles_per_seq=tiles_per_seq, final=final)
    resident = dict(pipeline_mode=pl.Buffered(1))
    return pl.pallas_call(
        kern,
        grid=(n_tok // tm,),
        in_specs=[pl.BlockSpec((tm, d), lambda i: (i, 0)),
                  pl.BlockSpec((FFN_HALO, d), lambda i: (jnp.maximum(i * hb - 1, 0), 0)),
                  pl.BlockSpec((FFN_HALO, d), lambda i: (jnp.minimum((i + 1) * hb, n_hblk - 1), 0)),
                  pl.BlockSpec((d, 2 * f), lambda i: (0, 0), **resident),
                  pl.BlockSpec((FFN_CONV_W, f), lambda i: (0, 0)),
                  pl.BlockSpec((1, f), lambda i: (0, 0)),
                  pl.BlockSpec((f, d), lambda i: (0, 0), **resident),
                  pl.BlockSpec((tm, d), lambda i: (i, 0)),
                  pl.BlockSpec((None, 6, d), lambda i: (mod_row(i), 0, 0)),
                  pl.BlockSpec((1, d), lambda i: (0, 0))],
        out_specs=pl.BlockSpec((tm, d), lambda i: (i, 0)),
        out_shape=jax.ShapeDtypeStruct((n_tok, d), F32),
        scratch_shapes=[pltpu.VMEM((tm + 2 * FFN_HALO, d), BF16),
                        pltpu.VMEM((2 * FFN_CHUNK // LANES, tm + 2 * FFN_HALO, LANES), F32),
                        pltpu.VMEM((tm, f), BF16)],
        compiler_params=_cparams(1),
        name="ffn",
    )(v2d, v2d, v2d, w_up, conv_w, conv_b.reshape(1, f), w_down, x2d, mods, final_w.reshape(1, d))


def _rope_tables(seq_len, head_dim):
    n_freq = head_dim // 4
    t = np.arange(seq_len)
    row = (t // GRID_W).astype(np.float32)
    col = (t % GRID_W).astype(np.float32)
    inv = (ROPE_THETA ** (-np.arange(n_freq, dtype=np.float32) / n_freq)).astype(np.float32)
    ang = np.stack([row[:, None] * inv, col[:, None] * inv], axis=1).astype(np.float64)
    cos = np.cos(ang)
    sin = np.sin(ang)
    cos_t = np.stack([cos, cos], axis=2).reshape(seq_len, head_dim).astype(np.float32)
    sin_t = np.stack([-sin, sin], axis=2).reshape(seq_len, head_dim).astype(np.float32)
    return jnp.asarray(cos_t), jnp.asarray(sin_t)


def kernel(x, c, ctx, c_ctx, mod_w, mod_b, norm1_w, norm2_w, w_in, rnn_conv_w, rnn_conv_b, rglru_w, rglru_b,
           rglru_lam, q_norm_w, k_norm_w, ret_decay, ret_norm_w, w_branch, w_out, ffn_up, ffn_conv_w,
           ffn_conv_b, ffn_down, final_norm_w):
    n_batch, ll, d = x.shape
    lc = ctx.shape[1]
    depth = mod_w.shape[0]
    assert d == D_MODEL and n_batch + 1 <= MOD_ROWS
    assert ll % 1024 == 0 and lc % RNN_CHUNK == 0 and lc % RET_CHUNK == 0

    cos_a, sin_a = _rope_tables(ll, ATT_HEAD_DIM)
    cos_r, sin_r = _rope_tables(ll, RET_QK_DIM)

    c_all = jnp.zeros((MOD_ROWS, d), F32).at[:n_batch].set(c).at[n_batch].set(c_ctx)
    mods = _modulation(c_all, mod_w, mod_b).reshape(depth, MOD_ROWS, 6, d)

    tm_lat = 1024
    tm_ctx = lc
    tm_ctx_big = next(t for t in (1024, 512, 256, lc) if (n_batch * lc) % t == 0)
    tm_ctx_mix = min(tm_ctx_big, 512)
    lat_tiles = ll // tm_lat
    lat_row_big = lambda i: i // lat_tiles
    ctx_row = lambda i: n_batch
    tm_mix = 512
    lat_row_mix = lambda i: i // (ll // tm_mix)

    x2 = x.reshape(n_batch * ll, d)
    cx2 = ctx.reshape(n_batch * lc, d)

    for l in range(depth):
        last = l == depth - 1
        w_in_p = jnp.concatenate([w_in[l][:, s:s + n] for s, n in _REF_SEGMENTS], axis=1).astype(BF16)
        gate_w = jnp.transpose(rglru_w[l], (2, 3, 0, 1, 4)).reshape(RNN_BLOCKS, RNN_BLOCK_W, 4 * RNN_BLOCK_W)
        gate_w = gate_w.astype(BF16)
        gate_b = jnp.transpose(rglru_b[l].reshape(2, 2, RNN_BLOCKS, RNN_BLOCK_W), (2, 0, 1, 3))
        gate_b = gate_b.reshape(RNN_BLOCKS, 1, 4 * RNN_BLOCK_W)
        dec = jnp.broadcast_to(jnp.transpose(ret_decay[l])[:, :, None], (RET_HEADS, 2, RET_QK_DIM))
        wb = w_branch[l].astype(BF16)
        wo = w_out[l].astype(BF16)
        w_up = ffn_up[l].astype(BF16)
        w_dn = ffn_down[l].astype(BF16)
        ml = mods[l]

        p_lat = _inproj(x2, ml, lat_row_big, norm1_w[l], w_in_p, tm_lat)
        p_ctx = _inproj(cx2, ml, ctx_row, norm1_w[l], w_in_p, tm_ctx_big)

        yr_c, yr_l = _rglru(p_ctx, p_lat, rnn_conv_w[l], rnn_conv_b[l].reshape(1, D_RNN), gate_w, gate_b,
                            rglru_lam[l], n_batch, lc, ll)
        ya_l = _attn_lat(p_ctx, p_lat, cos_a, sin_a, q_norm_w[l].reshape(1, -1), k_norm_w[l].reshape(1, -1),
                         n_batch, lc, ll)
        yt_c, yt_l = _retention(p_ctx, p_lat, cos_r, sin_r, dec, ret_norm_w[l].reshape(1, -1), n_batch, lc, ll)

        x2, v_lat = _merge(x2, yr_l, ya_l, yt_l, p_lat, ml, lat_row_mix, norm2_w[l], wb, wo, tm_mix)
        x2 = _ffn(v_lat, w_up, ffn_conv_w[l], ffn_conv_b[l], w_dn, x2, ml, lat_row_mix, final_norm_w,
                  tm_mix, ll, final=last)

        if not last:
            ya_c = _attn_ctx(p_ctx, q_norm_w[l].reshape(1, -1), k_norm_w[l].reshape(1, -1), n_batch, lc)
            cx2, v_ctx = _merge(cx2, yr_c, ya_c, yt_c, p_ctx, ml, ctx_row, norm2_w[l], wb, wo, tm_ctx_mix)
            cx2 = _ffn(v_ctx, w_up, ffn_conv_w[l], ffn_conv_b[l], w_dn, cx2, ml, ctx_row, final_norm_w,
                       tm_ctx, lc, final=False)

    return x2.reshape(n_batch, ll, d)
```

```python
import functools

import numpy as np
import jax
import jax.numpy as jnp
from jax import lax
from jax.experimental import pallas as pl
from jax.experimental.pallas import tpu as pltpu

F32 = jnp.float32
BF16 = jnp.bfloat16

D_MODEL = 1024
GRID_W = 64
D_RNN = 1024
RNN_BLOCKS = 8
RNN_BLOCK_W = D_RNN // RNN_BLOCKS
RNN_CONV_W = 4
RNN_PAD_L = 2
RG_C = 8.0
ATT_HEADS = 8
ATT_KV_HEADS = 2
ATT_HEAD_DIM = 128
ATT_GROUP = ATT_HEADS // ATT_KV_HEADS
ROPE_THETA = 10000.0
RET_HEADS = 4
RET_QK_DIM = 256
RET_V_DIM = 256
RET_CHUNK = 128
RET_GROUP = 4
N_BRANCH = 3
D_FF = 2816
FFN_CONV_W = 3
EPS = 1e-6

COL_RX = 0
COL_RG = 1024
COL_AQ = 2048
COL_RQ = 3072
COL_RK = 4096
COL_RV = 5120
COL_RGT = 6144
COL_BG = 7168
COL_AK = 10240
COL_AV = 10496
IN_COLS = 10752
_REF_SEGMENTS = ((0, 1024), (1024, 1024), (2048, 1024), (3584, 1024), (4608, 1024), (5632, 1024),
                 (6656, 1024), (7680, 3072), (3072, 256), (3328, 256))

SUBLANES = 8
LANES = 128
MOD_ROWS = 16
VMEM_LIMIT = 52 * 1024 * 1024


def _cparams(n_grid):
    return pltpu.CompilerParams(dimension_semantics=("arbitrary",) * n_grid, vmem_limit_bytes=VMEM_LIMIT)


def _sigmoid(x):
    return 0.5 * jnp.tanh(0.5 * x) + 0.5


def _silu(x):
    return x * _sigmoid(x)


def _gelu_tanh(x):
    c = float(np.sqrt(2.0 / np.pi))
    half = 0.5 * x
    return half * jnp.tanh(x * (c + (c * 0.044715) * (x * x))) + half


def _rms_rows(x):
    return x * lax.rsqrt(jnp.mean(x * x, axis=-1, keepdims=True) + EPS)


def _rope(x, cos, sin_signed, half):
    n = x.shape[-1]
    from_lo = pltpu.roll(x, half, axis=1)
    from_hi = pltpu.roll(x, n - half, axis=1)
    lane = lax.broadcasted_iota(jnp.int32, x.shape, 1)
    partner = jnp.where((lane & half) != 0, from_lo, from_hi)
    return x * cos + partner * sin_signed


def _mod_kernel(c_ref, w_ref, b_ref, o_ref):
    sc = _silu(c_ref[...])
    o_ref[...] = jnp.dot(sc.astype(BF16), w_ref[...].astype(BF16), preferred_element_type=F32) + b_ref[...]


def _modulation(c_all, mod_w, mod_b):
    depth, d, n = mod_w.shape
    tn = 1536
    return pl.pallas_call(
        _mod_kernel,
        grid=(depth, n // tn),
        in_specs=[pl.BlockSpec((MOD_ROWS, d), lambda l, j: (0, 0)),
                  pl.BlockSpec((None, d, tn), lambda l, j: (l, 0, j)),
                  pl.BlockSpec((None, 1, tn), lambda l, j: (l, 0, j))],
        out_specs=pl.BlockSpec((None, MOD_ROWS, tn), lambda l, j: (l, 0, j)),
        out_shape=jax.ShapeDtypeStruct((depth, MOD_ROWS, n), F32),
        compiler_params=_cparams(2),
        name="modulation",
    )(c_all, mod_w, mod_b.reshape(depth, 1, n))


def _inproj_kernel(x_ref, m_ref, nw_ref, w_ref, o_ref, u_ref):
    @pl.when(pl.program_id(1) == 0)
    def _():
        y = _rms_rows(x_ref[...]) * nw_ref[...]
        u_ref[...] = (y * (1.0 + m_ref[1:2, :]) + m_ref[0:1, :]).astype(BF16)

    o_ref[...] = jnp.dot(u_ref[...], w_ref[...], preferred_element_type=F32).astype(o_ref.dtype)


def _inproj(x2d, mods, mod_row, norm_w, w, tm):
    n_tok, d = x2d.shape
    n = w.shape[1]
    tn = n // 3
    return pl.pallas_call(
        _inproj_kernel,
        grid=(n_tok // tm, n // tn),
        in_specs=[pl.BlockSpec((tm, d), lambda i, j: (i, 0)),
                  pl.BlockSpec((None, 6, d), lambda i, j: (mod_row(i), 0, 0)),
                  pl.BlockSpec((1, d), lambda i, j: (0, 0)),
                  pl.BlockSpec((d, tn), lambda i, j: (0, j))],
        out_specs=pl.BlockSpec((tm, tn), lambda i, j: (i, j)),
        out_shape=jax.ShapeDtypeStruct((n_tok, n), BF16),
        scratch_shapes=[pltpu.VMEM((tm, d), BF16)],
        compiler_params=_cparams(2),
        name="inproj",
    )(x2d, mods, norm_w.reshape(1, d), w)


RNN_CHUNK = 256


RNN_SEGMENTS = 32
RNN_SEG_VREGS = RNN_SEGMENTS // SUBLANES
TINY = 1e-37


def _seg_pitch(n_rows):
    p = -(-n_rows // RNN_SEGMENTS)
    while p % 8 != 4:
        p += 1
    return p


def _rglru_kernel(xc_ref, xl_ref, gc_ref, gl_ref, cw_ref, cb_ref, gw_ref, gb_ref, lam_ref,
                  yc_ref, yl_ref, xs, af, hf, ab, hb, of, ob, tot, cin, *, lc, ll, pitch):
    w = xs.shape[1]
    tc = RNN_CHUNK
    pad = SUBLANES
    s_len = lc + ll
    n_rows = RNN_SEGMENTS * pitch
    for ref in (af, hf, ab, hb):
        ref[s_len:n_rows, :] = jnp.zeros((n_rows - s_len, w), F32)
    zero_pad = jnp.zeros((pad, w), F32)
    c_base = pad
    l_base = lc + 3 * pad
    xs[0:pad, :] = zero_pad
    xs[c_base + lc:c_base + lc + 2 * pad, :] = jnp.zeros((2 * pad, w), F32)
    xs[l_base + ll:l_base + ll + pad, :] = zero_pad
    xs[c_base:c_base + lc, :] = xc_ref[...].astype(F32)
    xs[l_base:l_base + ll, :] = xl_ref[...].astype(F32)

    lam = lam_ref[...]
    sp = jnp.maximum(-lam, 0.0) + jnp.log(1.0 + jnp.exp(-jnp.abs(lam)))
    half_rate_f = sp[0:1, :] * (-0.5 * RG_C * LOG2E)
    half_rate_b = sp[1:2, :] * (-0.5 * RG_C * LOG2E)
    cw = cw_ref[...]
    cb = cb_ref[...]
    gw_half = (gw_ref[...].astype(F32) * 0.5).astype(BF16)
    gb_half = gb_ref[...] * 0.5

    def gates_chunk(xs_base, base_f, base_b, t0):
        x = cb + jnp.zeros((tc, w), F32)
        for k in range(RNN_CONV_W):
            x = x + cw[k:k + 1, :] * xs[pl.ds(xs_base + t0 + (k - RNN_PAD_L), tc), :]
        g = jnp.dot(x.astype(BF16), gw_half, preferred_element_type=F32) + gb_half
        x_half = 0.5 * x
        for direction, (a_ref, b_ref, half_rate, base) in enumerate(
                ((af, hf, half_rate_f, base_f), (ab, hb, half_rate_b, base_b))):
            t_r = jnp.tanh(g[:, (2 * direction) * w:(2 * direction + 1) * w])
            t_i = jnp.tanh(g[:, (2 * direction + 1) * w:(2 * direction + 2) * w])
            a = jnp.exp2(t_r * half_rate + half_rate)
            z = 1.0 - a * a
            rows = pl.ds(pl.multiple_of(base + t0, SUBLANES), tc)
            a_ref[rows, :] = a
            b_ref[rows, :] = (z * lax.rsqrt(jnp.maximum(z, TINY))) * (t_i * x_half + x_half)

    def ctx_body(c, carry):
        gates_chunk(c_base, 0, ll, c * tc)
        return carry

    def lat_body(c, carry):
        gates_chunk(l_base, lc, 0, c * tc)
        return carry

    lax.fori_loop(0, lc // tc, ctx_body, 0)
    lax.fori_loop(0, ll // tc, lat_body, 0, unroll=2)

    def seg_rows(m, k):
        return pl.ds(m * SUBLANES * pitch + k, SUBLANES, stride=pitch)

    nv = RNN_SEG_VREGS
    ones = jnp.ones((SUBLANES, w), F32)
    zeros = jnp.zeros((SUBLANES, w), F32)

    def totals_body(k, carry):
        pf, tf, pb, tb = carry
        kb = pitch - 1 - k
        npf, ntf, npb, ntb = [], [], [], []
        for m in range(nv):
            a = af[seg_rows(m, k), :]
            npf.append(a * pf[m])
            ntf.append(a * tf[m] + hf[seg_rows(m, k), :])
            a = ab[seg_rows(m, kb), :]
            npb.append(a * pb[m])
            ntb.append(a * tb[m] + hb[seg_rows(m, kb), :])
        return tuple(npf), tuple(ntf), tuple(npb), tuple(ntb)

    pf, tf, pb, tb = lax.fori_loop(0, pitch, totals_body,
                                   ((ones,) * nv, (zeros,) * nv, (ones,) * nv, (zeros,) * nv), unroll=4)
    for m in range(nv):
        rows = slice(m * SUBLANES, (m + 1) * SUBLANES)
        tot[0, rows, :] = pf[m]
        tot[1, rows, :] = tf[m]
        tot[2, rows, :] = pb[m]
        tot[3, rows, :] = tb[m]

    c = jnp.zeros((1, w), F32)
    for j in range(RNN_SEGMENTS):
        cin[0, j:j + 1, :] = c
        c = tot[0, j:j + 1, :] * c + tot[1, j:j + 1, :]
    c = jnp.zeros((1, w), F32)
    for j in reversed(range(RNN_SEGMENTS)):
        cin[1, j:j + 1, :] = c
        c = tot[2, j:j + 1, :] * c + tot[3, j:j + 1, :]

    def scan_body(k, carry):
        h_f, h_b = carry
        kb = pitch - 1 - k
        nf, nb_ = [], []
        for m in range(nv):
            h = af[seg_rows(m, k), :] * h_f[m] + hf[seg_rows(m, k), :]
            of[seg_rows(m, k), :] = h
            nf.append(h)
            h = ab[seg_rows(m, kb), :] * h_b[m] + hb[seg_rows(m, kb), :]
            ob[seg_rows(m, kb), :] = h
            nb_.append(h)
        return tuple(nf), tuple(nb_)

    h0_f = tuple(cin[0, m * SUBLANES:(m + 1) * SUBLANES, :] for m in range(nv))
    h0_b = tuple(cin[1, m * SUBLANES:(m + 1) * SUBLANES, :] for m in range(nv))
    lax.fori_loop(0, pitch, scan_body, (h0_f, h0_b), unroll=4)

    def out_ctx(c, carry):
        rows = pl.ds(pl.multiple_of(c * tc, SUBLANES), tc)
        brows = pl.ds(pl.multiple_of(ll + c * tc, SUBLANES), tc)
        yc_ref[rows, :] = (_gelu_tanh(gc_ref[rows, :].astype(F32)) * (of[rows, :] + ob[brows, :])).astype(yc_ref.dtype)
        return carry

    def out_lat(c, carry):
        rows = pl.ds(pl.multiple_of(c * tc, SUBLANES), tc)
        frows = pl.ds(pl.multiple_of(lc + c * tc, SUBLANES), tc)
        yl_ref[rows, :] = (_gelu_tanh(gl_ref[rows, :].astype(F32)) * (of[frows, :] + ob[rows, :])).astype(yl_ref.dtype)
        return carry

    lax.fori_loop(0, lc // tc, out_ctx, 0)
    lax.fori_loop(0, ll // tc, out_lat, 0)


def _rglru(p_ctx, p_lat, conv_w, conv_b, gate_w, gate_b, lam, n_batch, lc, ll):
    w = RNN_BLOCK_W
    nb = RNN_BLOCKS
    s = lc + ll
    rx0 = COL_RX // w
    rg0 = COL_RG // w
    pitch = _seg_pitch(s)
    n_rows = RNN_SEGMENTS * pitch
    kern = functools.partial(_rglru_kernel, lc=lc, ll=ll, pitch=pitch)
    return pl.pallas_call(
        kern,
        grid=(n_batch, nb),
        in_specs=[pl.BlockSpec((lc, w), lambda b, c: (b, rx0 + c)),
                  pl.BlockSpec((ll, w), lambda b, c: (b, rx0 + c)),
                  pl.BlockSpec((lc, w), lambda b, c: (b, rg0 + c)),
                  pl.BlockSpec((ll, w), lambda b, c: (b, rg0 + c)),
                  pl.BlockSpec((RNN_CONV_W, w), lambda b, c: (0, c)),
                  pl.BlockSpec((1, w), lambda b, c: (0, c)),
                  pl.BlockSpec((None, w, 4 * w), lambda b, c: (c, 0, 0)),
                  pl.BlockSpec((None, 1, 4 * w), lambda b, c: (c, 0, 0)),
                  pl.BlockSpec((2, w), lambda b, c: (0, c))],
        out_specs=[pl.BlockSpec((lc, w), lambda b, c: (b, c)),
                   pl.BlockSpec((ll, w), lambda b, c: (b, c))],
        out_shape=[jax.ShapeDtypeStruct((n_batch * lc, D_RNN), BF16),
                   jax.ShapeDtypeStruct((n_batch * ll, D_RNN), BF16)],
        scratch_shapes=[pltpu.VMEM((s + 4 * SUBLANES, w), F32),
                        pltpu.VMEM((n_rows, w), F32), pltpu.VMEM((n_rows, w), F32),
                        pltpu.VMEM((n_rows, w), F32), pltpu.VMEM((n_rows, w), F32),
                        pltpu.VMEM((n_rows, w), F32), pltpu.VMEM((n_rows, w), F32),
                        pltpu.VMEM((4, RNN_SEGMENTS, w), F32), pltpu.VMEM((2, RNN_SEGMENTS, w), F32)],
        compiler_params=_cparams(2),
        name="rglru",
    )(p_ctx, p_lat, p_ctx, p_lat, conv_w, conv_b, gate_w, gate_b, lam)


ATT_TQ = 512
ATT_KEY_CHUNK = 512
LOG2E = 1.4426950408889634
ATT_SAFE_SPREAD = 96.0


def _attn_lat_kernel(q_ref, qn_ref, kc_ref, vc_ref, kl_ref, vl_ref, cq_ref, sq_ref, cqn_ref, sqn_ref, ck_ref, sk_ref,
                     qw_ref, kw_ref, o_ref, k_s, v_s, q_cur, q_nxt, qnorm_cur, qnorm_nxt, p_s, sc_s, *, lc):
    hd = ATT_HEAD_DIM
    half = hd // 4
    tq = q_ref.shape[0]
    s_len = k_s.shape[0]
    i = pl.program_id(2)
    qw = qw_ref[...] * (hd ** -0.5 * LOG2E)

    def prep_q(src_ref, cos_ref, sin_ref, q_dst, qnorm_dst, sc_idx):
        cos = cos_ref[...]
        sin = sin_ref[...]
        qmax = None
        for g in range(ATT_GROUP):
            rows = slice(g * tq, (g + 1) * tq)
            q = _rms_rows(src_ref[:, g * hd:(g + 1) * hd].astype(F32)) * qw
            q = _rope(q, cos, sin, half).astype(BF16)
            q_dst[rows, :] = q
            q32 = q.astype(F32)
            qn = jnp.sqrt(jnp.sum(q32 * q32, axis=-1, keepdims=True))
            qnorm_dst[rows, :] = jnp.broadcast_to(qn, (tq, hd))
            qmax = jnp.max(qn) if qmax is None else jnp.maximum(qmax, jnp.max(qn))
        sc_s[sc_idx] = qmax

    @pl.when(i == 0)
    def _():
        prep_q(q_ref, cq_ref, sq_ref, q_cur, qnorm_cur, 1)
        kw = kw_ref[...]
        kc = (_rms_rows(kc_ref[...].astype(F32)) * kw).astype(BF16)
        kl = _rope(_rms_rows(kl_ref[...].astype(F32)) * kw, ck_ref[...], sk_ref[...], half).astype(BF16)
        k_s[0:lc, :] = kc
        k_s[lc:, :] = kl
        v_s[0:lc, 0:hd] = vc_ref[...]
        v_s[lc:, 0:hd] = vl_ref[...]
        v_s[:, hd:] = jnp.ones((s_len, hd), BF16)
        kc32 = kc.astype(F32)
        kl32 = kl.astype(F32)
        kn_c = jnp.max(jnp.sqrt(jnp.sum(kc32 * kc32, axis=-1, keepdims=True)))
        kn_l = jnp.max(jnp.sqrt(jnp.sum(kl32 * kl32, axis=-1, keepdims=True)))
        sc_s[0] = jnp.maximum(kn_c, kn_l)

    kmax = sc_s[0]
    safe = 2.0 * sc_s[1] * kmax < ATT_SAFE_SPREAD

    @pl.when(safe)
    def _():
        prep_q(qn_ref, cqn_ref, sqn_ref, q_nxt, qnorm_nxt, 2)
        q_all = q_cur[...]
        bound = qnorm_cur[...] * kmax
        for c0 in range(0, s_len, ATT_KEY_CHUNK):
            n = min(ATT_KEY_CHUNK, s_len - c0)
            s = lax.dot_general(q_all, k_s[c0:c0 + n, :], (((1,), (1,)), ((), ())), preferred_element_type=F32)
            p_s[:, c0:c0 + n] = jnp.exp2(s - jnp.concatenate([bound] * (n // hd), axis=1)).astype(BF16)
        acc = jnp.dot(p_s[...], v_s[...], preferred_element_type=F32)
        for g in range(ATT_GROUP):
            a = acc[g * tq:(g + 1) * tq, :]
            o_ref[:, g * hd:(g + 1) * hd] = (a[:, 0:hd] / a[:, hd:]).astype(o_ref.dtype)

    @pl.when(jnp.logical_not(safe))
    def _():
        prep_q(qn_ref, cqn_ref, sqn_ref, q_nxt, qnorm_nxt, 2)
        k_all = k_s[...]
        v_all = v_s[:, 0:hd]
        for g in range(ATT_GROUP):
            s = lax.dot_general(q_cur[g * tq:(g + 1) * tq, :], k_all, (((1,), (1,)), ((), ())),
                                preferred_element_type=F32)
            p = jnp.exp2(s - jnp.max(s, axis=-1, keepdims=True))
            denom = jnp.sum(p, axis=-1, keepdims=True)
            o = jnp.dot(p.astype(BF16), v_all, preferred_element_type=F32)
            o_ref[:, g * hd:(g + 1) * hd] = (o / denom).astype(o_ref.dtype)

    q_cur[...] = q_nxt[...]
    qnorm_cur[...] = qnorm_nxt[...]
    sc_s[1] = sc_s[2]


def _attn_ctx_kernel(q_ref, kc_ref, vc_ref, qw_ref, kw_ref, o_ref):
    hd = ATT_HEAD_DIM
    k = (_rms_rows(kc_ref[...].astype(F32)) * kw_ref[...]).astype(BF16)
    qw = qw_ref[...] * (hd ** -0.5)
    v_c = vc_ref[...]
    for g in range(ATT_GROUP):
        q = (_rms_rows(q_ref[:, g * hd:(g + 1) * hd].astype(F32)) * qw).astype(BF16)
        s = lax.dot_general(q, k, (((1,), (1,)), ((), ())), preferred_element_type=F32)
        m = jnp.max(s, axis=-1, keepdims=True)
        p = jnp.exp(s - m)
        denom = jnp.sum(p, axis=-1, keepdims=True)
        o = jnp.dot(p.astype(BF16), v_c, preferred_element_type=F32)
        o_ref[:, g * hd:(g + 1) * hd] = (o / denom).astype(o_ref.dtype)


def _attn_lat(p_ctx, p_lat, cos, sin, q_norm_w, k_norm_w, n_batch, lc, ll):
    hd = ATT_HEAD_DIM
    gw = ATT_GROUP * hd
    tq = ATT_TQ
    nq = ll // tq
    q0 = COL_AQ // gw
    k0 = COL_AK // hd
    v0 = COL_AV // hd
    kern = functools.partial(_attn_lat_kernel, lc=lc)

    def nxt(i):
        return jnp.minimum(i + 1, nq - 1)

    return pl.pallas_call(
        kern,
        grid=(n_batch, ATT_KV_HEADS, nq),
        in_specs=[pl.BlockSpec((tq, gw), lambda b, h, i: (b * nq + i, q0 + h)),
                  pl.BlockSpec((tq, gw), lambda b, h, i: (b * nq + nxt(i), q0 + h)),
                  pl.BlockSpec((lc, hd), lambda b, h, i: (b, k0 + h)),
                  pl.BlockSpec((lc, hd), lambda b, h, i: (b, v0 + h)),
                  pl.BlockSpec((ll, hd), lambda b, h, i: (b, k0 + h)),
                  pl.BlockSpec((ll, hd), lambda b, h, i: (b, v0 + h)),
                  pl.BlockSpec((tq, hd), lambda b, h, i: (i, 0)),
                  pl.BlockSpec((tq, hd), lambda b, h, i: (i, 0)),
                  pl.BlockSpec((tq, hd), lambda b, h, i: (nxt(i), 0)),
                  pl.BlockSpec((tq, hd), lambda b, h, i: (nxt(i), 0)),
                  pl.BlockSpec((ll, hd), lambda b, h, i: (0, 0), pipeline_mode=pl.Buffered(1)),
                  pl.BlockSpec((ll, hd), lambda b, h, i: (0, 0), pipeline_mode=pl.Buffered(1)),
                  pl.BlockSpec((1, hd), lambda b, h, i: (0, 0)),
                  pl.BlockSpec((1, hd), lambda b, h, i: (0, 0))],
        out_specs=pl.BlockSpec((tq, gw), lambda b, h, i: (b * nq + i, h)),
        out_shape=jax.ShapeDtypeStruct((n_batch * ll, ATT_HEADS * hd), BF16),
        scratch_shapes=[pltpu.VMEM((lc + ll, hd), BF16),
                        pltpu.VMEM((lc + ll, 2 * hd), BF16),
                        pltpu.VMEM((ATT_GROUP * tq, hd), BF16), pltpu.VMEM((ATT_GROUP * tq, hd), BF16),
                        pltpu.VMEM((ATT_GROUP * tq, hd), F32), pltpu.VMEM((ATT_GROUP * tq, hd), F32),
                        pltpu.VMEM((ATT_GROUP * tq, lc + ll), BF16),
                        pltpu.SMEM((3,), F32)],
        compiler_params=_cparams(3),
        name="attn_lat",
    )(p_lat, p_lat, p_ctx, p_ctx, p_lat, p_lat, cos, sin, cos, sin, cos, sin, q_norm_w, k_norm_w)


def _attn_ctx(p_ctx, q_norm_w, k_norm_w, n_batch, lc):
    hd = ATT_HEAD_DIM
    gw = ATT_GROUP * hd
    q0 = COL_AQ // gw
    k0 = COL_AK // hd
    v0 = COL_AV // hd
    return pl.pallas_call(
        _attn_ctx_kernel,
        grid=(n_batch, ATT_KV_HEADS),
        in_specs=[pl.BlockSpec((lc, gw), lambda b, h: (b, q0 + h)),
                  pl.BlockSpec((lc, hd), lambda b, h: (b, k0 + h)),
                  pl.BlockSpec((lc, hd), lambda b, h: (b, v0 + h)),
                  pl.BlockSpec((1, hd), lambda b, h: (0, 0)),
                  pl.BlockSpec((1, hd), lambda b, h: (0, 0))],
        out_specs=pl.BlockSpec((lc, gw), lambda b, h: (b, h)),
        out_shape=jax.ShapeDtypeStruct((n_batch * lc, ATT_HEADS * hd), BF16),
        compiler_params=_cparams(2),
        name="attn_ctx",
    )(p_ctx, p_ctx, p_ctx, q_norm_w, k_norm_w)


def _ret_kernel(qc_ref, kc_ref, vc_ref, gc_ref, ql_ref, kl_ref, vl_ref, gl_ref, rcos_ref, rsin_ref, ccos_ref,
                csin_ref, dec_ref, nw_ref, yc_ref, yl_ref,
                qb_s, kb_s, qf_s, qr_s, kf_s, sbs_s, sf_s, sb_s, o_a, o_b, *, lc, ll):
    c = RET_CHUNK
    dk = RET_QK_DIM
    dv = RET_V_DIM
    half = dk // 4
    n_c = lc // c
    n_l = ll // c
    rows_per_chunk = c // GRID_W
    dec = dec_ref[...]
    log_g = jnp.minimum(dec, 0.0) - jnp.log(1.0 + jnp.exp(-jnp.abs(dec)))
    lg_f = log_g[0:1, :]
    lg_b = log_g[1:2, :]
    pos = lax.broadcasted_iota(jnp.int32, (c, dk), 0).astype(F32)
    qd_f = jnp.exp((pos + 1.0) * lg_f)
    kd_f = jnp.exp((c - 1.0 - pos) * lg_f)
    cd_f = jnp.exp(float(c) * lg_f)
    qd_b = jnp.exp((c - pos) * lg_b)
    kd_b = jnp.exp(pos * lg_b)
    cd_b = jnp.exp(float(c) * lg_b)
    ii = lax.broadcasted_iota(jnp.int32, (c, c), 0)
    jj = lax.broadcasted_iota(jnp.int32, (c, c), 1)
    diff = (ii - jj).astype(F32)
    intra = jnp.where(ii >= jj, jnp.exp(jnp.maximum(diff, 0.0) * lg_f[:, 0:c]),
                      jnp.exp(jnp.maximum(-diff, 0.0) * lg_b[:, 0:c]))
    nw = nw_ref[...]
    k_scale = dk ** -0.5
    col_cos = jnp.concatenate([ccos_ref[...]] * rows_per_chunk, axis=0)
    col_sin = jnp.concatenate([csin_ref[...]] * rows_per_chunk, axis=0)

    def rope_tables(ci):
        def row_part(ref):
            return jnp.concatenate(
                [jnp.broadcast_to(ref[pl.ds(ci * rows_per_chunk + r, 1), :], (GRID_W, dk // 2))
                 for r in range(rows_per_chunk)], axis=0)
        return (jnp.concatenate([row_part(rcos_ref), col_cos], axis=1),
                jnp.concatenate([row_part(rsin_ref), col_sin], axis=1))

    def state_update(s_ref, k_dec, v, c_dec):
        kv = lax.dot_general(k_dec, v, (((0,), (0,)), ((), ())), preferred_element_type=F32)
        s_ref[...] = s_ref[...] * c_dec + kv

    def sweep1_chunk(q, k, v, g, orows):
        qb_s[orows, :] = q.astype(BF16)
        kb_s[orows, :] = k.astype(BF16)
        qf_s[orows, :] = (q * qd_f).astype(BF16)
        qr_s[orows, :] = (q * qd_b).astype(BF16)
        kf_s[orows, :] = (k * kd_f).astype(BF16)
        sbs_s[g] = sb_s[...].astype(BF16)
        state_update(sb_s, (k * kd_b).astype(BF16), v, cd_b)

    sf_s[...] = jnp.zeros((dk, dv), F32)
    sb_s[...] = jnp.zeros((dk, dv), F32)

    for ci in reversed(range(n_c)):
        rows = pl.ds(ci * c, c)
        sweep1_chunk(qc_ref[rows, :].astype(F32), kc_ref[rows, :].astype(F32) * k_scale, vc_ref[rows, :], ci, rows)

    def sweep1_body(i, carry):
        ci = n_l - 1 - i
        rows = pl.ds(pl.multiple_of(ci * c, c), c)
        orows = pl.ds(pl.multiple_of(lc + ci * c, c), c)
        cos, sin = rope_tables(ci)
        q = _rope(ql_ref[rows, :].astype(F32), cos, sin, half)
        k = _rope(kl_ref[rows, :].astype(F32), cos, sin, half) * k_scale
        sweep1_chunk(q, k, vl_ref[rows, :], n_c + ci, orows)
        return carry

    lax.fori_loop(0, n_l, sweep1_body, 0, unroll=4)

    def raw_out(g, orows, v):
        att = lax.dot_general(qb_s[orows, :], kb_s[orows, :], (((1,), (1,)), ((), ())),
                              preferred_element_type=F32) * intra
        o = (jnp.dot(att.astype(BF16), v, preferred_element_type=F32)
             + jnp.dot(qf_s[orows, :], sf_s[...].astype(BF16), preferred_element_type=F32)
             + jnp.dot(qr_s[orows, :], sbs_s[g], preferred_element_type=F32))
        state_update(sf_s, kf_s[orows, :], v, cd_f)
        return o

    def finish(o, gate):
        mu = jnp.mean(o, axis=-1, keepdims=True)
        oc = o - mu
        var = jnp.mean(oc * oc, axis=-1, keepdims=True)
        y = oc * lax.rsqrt(var + EPS) * nw
        return _silu(gate) * y

    for ci in range(n_c):
        rows = pl.ds(ci * c, c)
        yc_ref[rows, :] = finish(raw_out(ci, rows, vc_ref[rows, :]), gc_ref[rows, :].astype(F32)).astype(yc_ref.dtype)

    gsz = RET_GROUP
    n_groups = n_l // gsz

    def group_matmuls(grp, o_dst):
        for j in range(gsz):
            ci = grp * gsz + j
            rows = pl.ds(pl.multiple_of(ci * c, c), c)
            orows = pl.ds(pl.multiple_of(lc + ci * c, c), c)
            o_dst[j * c:(j + 1) * c, :] = raw_out(n_c + ci, orows, vl_ref[rows, :])

    def group_finish(grp, o_src):
        for j in range(gsz):
            ci = grp * gsz + j
            rows = pl.ds(pl.multiple_of(ci * c, c), c)
            yl_ref[rows, :] = finish(o_src[j * c:(j + 1) * c, :], gl_ref[rows, :].astype(F32)).astype(yl_ref.dtype)

    group_matmuls(0, o_a)

    def pair_body(p, carry):
        group_matmuls(2 * p + 1, o_b)
        group_finish(2 * p, o_a)
        group_matmuls(2 * p + 2, o_a)
        group_finish(2 * p + 1, o_b)
        return carry

    lax.fori_loop(0, n_groups // 2 - 1, pair_body, 0)
    group_matmuls(n_groups - 1, o_b)
    group_finish(n_groups - 2, o_a)
    group_finish(n_groups - 1, o_b)


def _retention(p_ctx, p_lat, cos, sin, dec, norm_w, n_batch, lc, ll):
    dk = RET_QK_DIM
    q0 = COL_RQ // dk
    k0 = COL_RK // dk
    v0 = COL_RV // dk
    g0 = COL_RGT // dk
    n_chunks = (lc + ll) // RET_CHUNK
    assert RET_CHUNK % GRID_W == 0 and ll % (2 * RET_GROUP * RET_CHUNK) == 0
    row_cos, row_sin = cos[::GRID_W, :dk // 2], sin[::GRID_W, :dk // 2]
    col_cos, col_sin = cos[:GRID_W, dk // 2:], sin[:GRID_W, dk // 2:]
    kern = functools.partial(_ret_kernel, lc=lc, ll=ll)

    def col(off):
        return lambda b, h: (b, off + h)

    const = lambda b, h: (0, 0)
    return pl.pallas_call(
        kern,
        grid=(n_batch, RET_HEADS),
        in_specs=[pl.BlockSpec((lc, dk), col(q0)), pl.BlockSpec((lc, dk), col(k0)),
                  pl.BlockSpec((lc, dk), col(v0)), pl.BlockSpec((lc, dk), col(g0)),
                  pl.BlockSpec((ll, dk), col(q0)), pl.BlockSpec((ll, dk), col(k0)),
                  pl.BlockSpec((ll, dk), col(v0)), pl.BlockSpec((ll, dk), col(g0)),
                  pl.BlockSpec(row_cos.shape, const), pl.BlockSpec(row_sin.shape, const),
                  pl.BlockSpec(col_cos.shape, const), pl.BlockSpec(col_sin.shape, const),
                  pl.BlockSpec((None, 2, dk), lambda b, h: (h, 0, 0)),
                  pl.BlockSpec((1, dk), lambda b, h: (0, h))],
        out_specs=[pl.BlockSpec((lc, dk), lambda b, h: (b, h)),
                   pl.BlockSpec((ll, dk), lambda b, h: (b, h))],
        out_shape=[jax.ShapeDtypeStruct((n_batch * lc, RET_HEADS * RET_V_DIM), BF16),
                   jax.ShapeDtypeStruct((n_batch * ll, RET_HEADS * RET_V_DIM), BF16)],
        scratch_shapes=[pltpu.VMEM((lc + ll, dk), BF16)] * 5
        + [pltpu.VMEM((n_chunks, dk, RET_V_DIM), BF16)]
        + [pltpu.VMEM((dk, RET_V_DIM), F32)] * 2
        + [pltpu.VMEM((RET_GROUP * RET_CHUNK, RET_V_DIM), F32)] * 2,
        compiler_params=_cparams(2),
        name="retention",
    )(p_ctx, p_ctx, p_ctx, p_ctx, p_lat, p_lat, p_lat, p_lat, row_cos, row_sin, col_cos, col_sin, dec, norm_w)


def _merge_kernel(x_ref, yr_ref, ya_ref, yt_ref, g0_ref, g1_ref, g2_ref, wb_ref, wo_ref, m_ref, nw_ref,
                  xo_ref, vo_ref):
    m = (_sigmoid(g0_ref[...].astype(F32)) * jnp.dot(yr_ref[...], wb_ref[0], preferred_element_type=F32)
         + _sigmoid(g1_ref[...].astype(F32)) * jnp.dot(ya_ref[...], wb_ref[1], preferred_element_type=F32)
         + _sigmoid(g2_ref[...].astype(F32)) * jnp.dot(yt_ref[...], wb_ref[2], preferred_element_type=F32))
    out = jnp.dot(m.astype(BF16), wo_ref[...], preferred_element_type=F32)
    x1 = x_ref[...] + m_ref[2:3, :] * out
    xo_ref[...] = x1
    v = _rms_rows(x1) * nw_ref[...]
    vo_ref[...] = (v * (1.0 + m_ref[4:5, :]) + m_ref[3:4, :]).astype(vo_ref.dtype)


def _merge(x2d, y_rnn, y_att, y_ret, p, mods, mod_row, norm_w, w_branch, w_out, tm):
    n_tok, d = x2d.shape
    bg0 = COL_BG // d
    tok = lambda i: (i, 0)
    const2 = lambda i: (0, 0)
    return pl.pallas_call(
        _merge_kernel,
        grid=(n_tok // tm,),
        in_specs=[pl.BlockSpec((tm, d), tok), pl.BlockSpec((tm, d), tok), pl.BlockSpec((tm, d), tok),
                  pl.BlockSpec((tm, d), tok),
                  pl.BlockSpec((tm, d), lambda i: (i, bg0)),
                  pl.BlockSpec((tm, d), lambda i: (i, bg0 + 1)),
                  pl.BlockSpec((tm, d), lambda i: (i, bg0 + 2)),
                  pl.BlockSpec((N_BRANCH, d, d), lambda i: (0, 0, 0)),
                  pl.BlockSpec((d, d), const2),
                  pl.BlockSpec((None, 6, d), lambda i: (mod_row(i), 0, 0)),
                  pl.BlockSpec((1, d), const2)],
        out_specs=[pl.BlockSpec((tm, d), tok), pl.BlockSpec((tm, d), tok)],
        out_shape=[jax.ShapeDtypeStruct((n_tok, d), F32), jax.ShapeDtypeStruct((n_tok, d), BF16)],
        compiler_params=_cparams(1),
        name="merge",
    )(x2d, y_rnn, y_att, y_ret, p, p, p, w_branch, w_out, mods, norm_w.reshape(1, d))


FFN_CHUNK = 256
FFN_HALO = 16


def _ffn_kernel(v_ref, vp_ref, vn_ref, wu_ref, cw_ref, cb_ref, wd_ref, x_ref, m_ref, fw_ref, o_ref,
                vext_s, a_s, h_s, *, tiles_per_seq, final):
    tm = v_ref.shape[0]
    f = h_s.shape[1]
    fc = FFN_CHUNK
    n_slab = fc // LANES
    ti = pl.program_id(0) % tiles_per_seq
    vext_s[0:FFN_HALO, :] = vp_ref[...]
    vext_s[FFN_HALO:FFN_HALO + tm, :] = v_ref[...]
    vext_s[FFN_HALO + tm:, :] = vn_ref[...]
    v_ext = vext_s[...]
    v_main = v_ref[...]
    first = ti == 0
    last = ti == tiles_per_seq - 1
    row0 = jnp.minimum(pl.program_id(0), 0)
    cw = cw_ref[...] * 0.5
    cb = cb_ref[...] * 0.5
    for c in range(f // fc):
        cols = slice(c * fc, (c + 1) * fc)
        a_ext = jnp.dot(v_ext, wu_ref[:, cols], preferred_element_type=F32)
        b = jnp.dot(v_main, wu_ref[:, f + c * fc:f + (c + 1) * fc], preferred_element_type=F32)
        taps = [[], [], []]
        for j in range(n_slab):
            slab = (c % 2) * n_slab + j
            lanes = slice(j * LANES, (j + 1) * LANES)
            a_s[slab] = a_ext[:, lanes]
            a_s[slab, FFN_HALO - 1:FFN_HALO, :] = jnp.where(first, 0.0, a_ext[FFN_HALO - 1:FFN_HALO, lanes])
            a_s[slab, FFN_HALO + tm:FFN_HALO + tm + 1, :] = jnp.where(
                last, 0.0, a_ext[FFN_HALO + tm:FFN_HALO + tm + 1, lanes])
            for k in range(FFN_CONV_W):
                taps[k].append(a_s[slab, pl.ds(row0 + (FFN_HALO - 1 + k), tm), :])
        a_m1, a_0, a_p1 = (jnp.concatenate(t, axis=1) for t in taps)
        half_conv = cw[0:1, cols] * a_m1 + cw[1:2, cols] * a_0 + cw[2:3, cols] * a_p1 + cb[:, cols]
        h_s[:, cols] = ((half_conv * (jnp.tanh(half_conv) + 1.0)) * b).astype(BF16)
    out = jnp.dot(h_s[...], wd_ref[...], preferred_element_type=F32)
    x2 = x_ref[...] + m_ref[5:6, :] * out
    if final:
        x2 = _rms_rows(x2) * fw_ref[...]
    o_ref[...] = x2


def _ffn(v2d, w_up, conv_w, conv_b, w_down, x2d, mods, mod_row, final_w, tm, seq_len, final):
    n_tok, d = x2d.shape
    f = D_FF
    tiles_per_seq = seq_len // tm
    hb = tm // FFN_HALO
    n_hblk = n_tok // FFN_HALO
    kern = functools.partial(_ffn_kernel, tiles_per_seq=tiles_per_seq, final=final)
    resident = dict(pipeline_mode=pl.Buffered(1))
    return pl.pallas_call(
        kern,
        grid=(n_tok // tm,),
        in_specs=[pl.BlockSpec((tm, d), lambda i: (i, 0)),
                  pl.BlockSpec((FFN_HALO, d), lambda i: (jnp.maximum(i * hb - 1, 0), 0)),
                  pl.BlockSpec((FFN_HALO, d), lambda i: (jnp.minimum((i + 1) * hb, n_hblk - 1), 0)),
                  pl.BlockSpec((d, 2 * f), lambda i: (0, 0), **resident),
                  pl.BlockSpec((FFN_CONV_W, f), lambda i: (0, 0)),
                  pl.BlockSpec((1, f), lambda i: (0, 0)),
                  pl.BlockSpec((f, d), lambda i: (0, 0), **resident),
                  pl.BlockSpec((tm, d), lambda i: (i, 0)),
                  pl.BlockSpec((None, 6, d), lambda i: (mod_row(i), 0, 0)),
                  pl.BlockSpec((1, d), lambda i: (0, 0))],
        out_specs=pl.BlockSpec((tm, d), lambda i: (i, 0)),
        out_shape=jax.ShapeDtypeStruct((n_tok, d), F32),
        scratch_shapes=[pltpu.VMEM((tm + 2 * FFN_HALO, d), BF16),
                        pltpu.VMEM((2 * FFN_CHUNK // LANES, tm + 2 * FFN_HALO, LANES), F32),
                        pltpu.VMEM((tm, f), BF16)],
        compiler_params=_cparams(1),
        name="ffn",
    )(v2d, v2d, v2d, w_up, conv_w, conv_b.reshape(1, f), w_down, x2d, mods, final_w.reshape(1, d))


def _rope_tables(seq_len, head_dim):
    n_freq = head_dim // 4
    t = np.arange(seq_len)
    row = (t // GRID_W).astype(np.float32)
    col = (t % GRID_W).astype(np.float32)
    inv = (ROPE_THETA ** (-np.arange(n_freq, dtype=np.float32) / n_freq)).astype(np.float32)
    ang = np.stack([row[:, None] * inv, col[:, None] * inv], axis=1).astype(np.float64)
    cos = np.cos(ang)
    sin = np.sin(ang)
    cos_t = np.stack([cos, cos], axis=2).reshape(seq_len, head_dim).astype(np.float32)
    sin_t = np.stack([-sin, sin], axis=2).reshape(seq_len, head_dim).astype(np.float32)
    return jnp.asarray(cos_t), jnp.asarray(sin_t)


def kernel(x, c, ctx, c_ctx, mod_w, mod_b, norm1_w, norm2_w, w_in, rnn_conv_w, rnn_conv_b, rglru_w, rglru_b,
           rglru_lam, q_norm_w, k_norm_w, ret_decay, ret_norm_w, w_branch, w_out, ffn_up, ffn_conv_w,
           ffn_conv_b, ffn_down, final_norm_w):
    n_batch, ll, d = x.shape
    lc = ctx.shape[1]
    depth = mod_w.shape[0]
    assert d == D_MODEL and n_batch + 1 <= MOD_ROWS
    assert ll % 1024 == 0 and lc % RNN_CHUNK == 0 and lc % RET_CHUNK == 0

    cos_a, sin_a = _rope_tables(ll, ATT_HEAD_DIM)
    cos_r, sin_r = _rope_tables(ll, RET_QK_DIM)

    c_all = jnp.zeros((MOD_ROWS, d), F32).at[:n_batch].set(c).at[n_batch].set(c_ctx)
    mods = _modulation(c_all, mod_w, mod_b).reshape(depth, MOD_ROWS, 6, d)

    tm_lat = 1024
    tm_ctx = lc
    tm_ctx_big = next(t for t in (1024, 512, 256, lc) if (n_batch * lc) % t == 0)
    tm_ctx_mix = min(tm_ctx_big, 512)
    lat_tiles = ll // tm_lat
    lat_row_big = lambda i: i // lat_tiles
    ctx_row = lambda i: n_batch
    tm_mix = 512
    lat_row_mix = lambda i: i // (ll // tm_mix)

    x2 = x.reshape(n_batch * ll, d)
    cx2 = ctx.reshape(n_batch * lc, d)

    for l in range(depth):
        last = l == depth - 1
        w_in_p = jnp.concatenate([w_in[l][:, s:s + n] for s, n in _REF_SEGMENTS], axis=1).astype(BF16)
        gate_w = jnp.transpose(rglru_w[l], (2, 3, 0, 1, 4)).reshape(RNN_BLOCKS, RNN_BLOCK_W, 4 * RNN_BLOCK_W)
        gate_w = gate_w.astype(BF16)
        gate_b = jnp.transpose(rglru_b[l].reshape(2, 2, RNN_BLOCKS, RNN_BLOCK_W), (2, 0, 1, 3))
        gate_b = gate_b.reshape(RNN_BLOCKS, 1, 4 * RNN_BLOCK_W)
        dec = jnp.broadcast_to(jnp.transpose(ret_decay[l])[:, :, None], (RET_HEADS, 2, RET_QK_DIM))
        wb = w_branch[l].astype(BF16)
        wo = w_out[l].astype(BF16)
        w_up = ffn_up[l].astype(BF16)
        w_dn = ffn_down[l].astype(BF16)
        ml = mods[l]

        p_lat = _inproj(x2, ml, lat_row_big, norm1_w[l], w_in_p, tm_lat)
        p_ctx = _inproj(cx2, ml, ctx_row, norm1_w[l], w_in_p, tm_ctx_big)

        yr_c, yr_l = _rglru(p_ctx, p_lat, rnn_conv_w[l], rnn_conv_b[l].reshape(1, D_RNN), gate_w, gate_b,
                            rglru_lam[l], n_batch, lc, ll)
        ya_l = _attn_lat(p_ctx, p_lat, cos_a, sin_a, q_norm_w[l].reshape(1, -1), k_norm_w[l].reshape(1, -1),
                         n_batch, lc, ll)
        yt_c, yt_l = _retention(p_ctx, p_lat, cos_r, sin_r, dec, ret_norm_w[l].reshape(1, -1), n_batch, lc, ll)

        x2, v_lat = _merge(x2, yr_l, ya_l, yt_l, p_lat, ml, lat_row_mix, norm2_w[l], wb, wo, tm_mix)
        x2 = _ffn(v_lat, w_up, ffn_conv_w[l], ffn_conv_b[l], w_dn, x2, ml, lat_row_mix, final_norm_w,
                  tm_mix, ll, final=last)

        if not last:
            ya_c = _attn_ctx(p_ctx, q_norm_w[l].reshape(1, -1), k_norm_w[l].reshape(1, -1), n_batch, lc)
            cx2, v_ctx = _merge(cx2, yr_c, ya_c, yt_c, p_ctx, ml, ctx_row, norm2_w[l], wb, wo, tm_ctx_mix)
            cx2 = _ffn(v_ctx, w_up, ffn_conv_w[l], ffn_conv_b[l], w_dn, cx2, ml, ctx_row, final_norm_w,
                       tm_ctx, lc, final=False)

    return x2.reshape(n_batch, ll, d)
```

```python
import functools

import numpy as np
import jax
import jax.numpy as jnp
from jax import lax
from jax.experimental import pallas as pl
from jax.experimental.pallas import tpu as pltpu

F32 = jnp.float32
BF16 = jnp.bfloat16

D_MODEL = 1024
GRID_W = 64
D_RNN = 1024
RNN_BLOCKS = 8
RNN_BLOCK_W = D_RNN // RNN_BLOCKS
RNN_CONV_W = 4
RNN_PAD_L = 2
RG_C = 8.0
ATT_HEADS = 8
ATT_KV_HEADS = 2
ATT_HEAD_DIM = 128
ATT_GROUP = ATT_HEADS // ATT_KV_HEADS
ROPE_THETA = 10000.0
RET_HEADS = 4
RET_QK_DIM = 256
RET_V_DIM = 256
RET_CHUNK = 128
RET_GROUP = 4
N_BRANCH = 3
D_FF = 2816
FFN_CONV_W = 3
EPS = 1e-6

COL_RX = 0
COL_RG = 1024
COL_AQ = 2048
COL_RQ = 3072
COL_RK = 4096
COL_RV = 5120
COL_RGT = 6144
COL_BG = 7168
COL_AK = 10240
COL_AV = 10496
IN_COLS = 10752
_REF_SEGMENTS = ((0, 1024), (1024, 1024), (2048, 1024), (3584, 1024), (4608, 1024), (5632, 1024),
                 (6656, 1024), (7680, 3072), (3072, 256), (3328, 256))

SUBLANES = 8
LANES = 128
MOD_ROWS = 16
VMEM_LIMIT = 52 * 1024 * 1024


def _cparams(n_grid):
    return pltpu.CompilerParams(dimension_semantics=("arbitrary",) * n_grid, vmem_limit_bytes=VMEM_LIMIT)


def _sigmoid(x):
    return 0.5 * jnp.tanh(0.5 * x) + 0.5


def _silu(x):
    return x * _sigmoid(x)


def _gelu_tanh(x):
    c = float(np.sqrt(2.0 / np.pi))
    half = 0.5 * x
    return half * jnp.tanh(x * (c + (c * 0.044715) * (x * x))) + half


def _rms_rows(x):
    return x * lax.rsqrt(jnp.mean(x * x, axis=-1, keepdims=True) + EPS)


def _rope(x, cos, sin_signed, half):
    n = x.shape[-1]
    if 2 * half == LANES:
        partner = jnp.concatenate([pltpu.roll(x[:, j:j + LANES], half, axis=1) for j in range(0, n, LANES)], axis=1)
        return x * cos + partner * sin_signed
    from_lo = pltpu.roll(x, half, axis=1)
    from_hi = pltpu.roll(x, n - half, axis=1)
    lane = lax.broadcasted_iota(jnp.int32, x.shape, 1)
    partner = jnp.where((lane & half) != 0, from_lo, from_hi)
    return x * cos + partner * sin_signed


def _mod_kernel(c_ref, w_ref, b_ref, o_ref):
    sc = _silu(c_ref[...])
    o_ref[...] = jnp.dot(sc.astype(BF16), w_ref[...].astype(BF16), preferred_element_type=F32) + b_ref[...]


def _modulation(c_all, mod_w, mod_b):
    depth, d, n = mod_w.shape
    tn = 1536
    return pl.pallas_call(
        _mod_kernel,
        grid=(depth, n // tn),
        in_specs=[pl.BlockSpec((MOD_ROWS, d), lambda l, j: (0, 0)),
                  pl.BlockSpec((None, d, tn), lambda l, j: (l, 0, j)),
                  pl.BlockSpec((None, 1, tn), lambda l, j: (l, 0, j))],
        out_specs=pl.BlockSpec((None, MOD_ROWS, tn), lambda l, j: (l, 0, j)),
        out_shape=jax.ShapeDtypeStruct((depth, MOD_ROWS, n), F32),
        compiler_params=_cparams(2),
        name="modulation",
    )(c_all, mod_w, mod_b.reshape(depth, 1, n))


def _inproj_kernel(x_ref, m_ref, nw_ref, w_ref, o_ref, u_ref):
    @pl.when(pl.program_id(1) == 0)
    def _():
        y = _rms_rows(x_ref[...]) * nw_ref[...]
        u_ref[...] = (y * (1.0 + m_ref[1:2, :]) + m_ref[0:1, :]).astype(BF16)

    o_ref[...] = jnp.dot(u_ref[...], w_ref[...], preferred_element_type=F32).astype(o_ref.dtype)


def _inproj(x2d, mods, mod_row, norm_w, w, tm):
    n_tok, d = x2d.shape
    n = w.shape[1]
    tn = n // 6
    return pl.pallas_call(
        _inproj_kernel,
        grid=(n_tok // tm, n // tn),
        in_specs=[pl.BlockSpec((tm, d), lambda i, j: (i, 0)),
                  pl.BlockSpec((None, 6, d), lambda i, j: (mod_row(i), 0, 0)),
                  pl.BlockSpec((1, d), lambda i, j: (0, 0)),
                  pl.BlockSpec((d, tn), lambda i, j: (0, j))],
        out_specs=pl.BlockSpec((tm, tn), lambda i, j: (i, j)),
        out_shape=jax.ShapeDtypeStruct((n_tok, n), BF16),
        scratch_shapes=[pltpu.VMEM((tm, d), BF16)],
        compiler_params=_cparams(2),
        name="inproj",
    )(x2d, mods, norm_w.reshape(1, d), w)


RNN_CHUNK = 256


RNN_SEGMENTS = 32
RNN_SEG_VREGS = RNN_SEGMENTS // SUBLANES
TINY = 1e-37


def _seg_pitch(n_rows):
    p = -(-n_rows // RNN_SEGMENTS)
    while p % 8 != 4:
        p += 1
    return p


def _rglru_kernel(xc_ref, xl_ref, gc_ref, gl_ref, cw_ref, cb_ref, gw_ref, gb_ref, lam_ref,
                  yc_ref, yl_ref, xs, af, hf, ab, hb, of, ob, tot, cin, *, lc, ll, pitch):
    w = xs.shape[1]
    tc = RNN_CHUNK
    pad = SUBLANES
    s_len = lc + ll
    n_rows = RNN_SEGMENTS * pitch
    for ref in (af, hf, ab, hb):
        ref[s_len:n_rows, :] = jnp.zeros((n_rows - s_len, w), F32)
    zero_pad = jnp.zeros((pad, w), F32)
    c_base = pad
    l_base = lc + 3 * pad
    xs[0:pad, :] = zero_pad
    xs[c_base + lc:c_base + lc + 2 * pad, :] = jnp.zeros((2 * pad, w), F32)
    xs[l_base + ll:l_base + ll + pad, :] = zero_pad
    xs[c_base:c_base + lc, :] = xc_ref[...].astype(F32)
    xs[l_base:l_base + ll, :] = xl_ref[...].astype(F32)

    lam = lam_ref[...]
    sp = jnp.maximum(-lam, 0.0) + jnp.log(1.0 + jnp.exp(-jnp.abs(lam)))
    half_rate_f = sp[0:1, :] * (-0.5 * RG_C * LOG2E)
    half_rate_b = sp[1:2, :] * (-0.5 * RG_C * LOG2E)
    cw = cw_ref[...]
    cb = cb_ref[...]
    gw_half = (gw_ref[...].astype(F32) * 0.5).astype(BF16)
    gb_half = gb_ref[...] * 0.5

    def gates_chunk(xs_base, base_f, base_b, t0):
        x = cb + jnp.zeros((tc, w), F32)
        for k in range(RNN_CONV_W):
            x = x + cw[k:k + 1, :] * xs[pl.ds(xs_base + t0 + (k - RNN_PAD_L), tc), :]
        g = jnp.dot(x.astype(BF16), gw_half, preferred_element_type=F32) + gb_half
        x_half = 0.5 * x
        for direction, (a_ref, b_ref, half_rate, base) in enumerate(
                ((af, hf, half_rate_f, base_f), (ab, hb, half_rate_b, base_b))):
            t_r = jnp.tanh(g[:, (2 * direction) * w:(2 * direction + 1) * w])
            t_i = jnp.tanh(g[:, (2 * direction + 1) * w:(2 * direction + 2) * w])
            a = jnp.exp2(t_r * half_rate + half_rate)
            z = 1.0 - a * a
            rows = pl.ds(pl.multiple_of(base + t0, SUBLANES), tc)
            a_ref[rows, :] = a
            b_ref[rows, :] = (z * lax.rsqrt(jnp.maximum(z, TINY))) * (t_i * x_half + x_half)

    def ctx_body(c, carry):
        gates_chunk(c_base, 0, ll, c * tc)
        return carry

    def lat_body(c, carry):
        gates_chunk(l_base, lc, 0, c * tc)
        return carry

    lax.fori_loop(0, lc // tc, ctx_body, 0)
    lax.fori_loop(0, ll // tc, lat_body, 0, unroll=2)

    def seg_rows(m, k):
        return pl.ds(m * SUBLANES * pitch + k, SUBLANES, stride=pitch)

    nv = RNN_SEG_VREGS
    ones = jnp.ones((SUBLANES, w), F32)
    zeros = jnp.zeros((SUBLANES, w), F32)

    def totals_body(k, carry):
        pf, tf, pb, tb = carry
        kb = pitch - 1 - k
        npf, ntf, npb, ntb = [], [], [], []
        for m in range(nv):
            a = af[seg_rows(m, k), :]
            npf.append(a * pf[m])
            ntf.append(a * tf[m] + hf[seg_rows(m, k), :])
            a = ab[seg_rows(m, kb), :]
            npb.append(a * pb[m])
            ntb.append(a * tb[m] + hb[seg_rows(m, kb), :])
        return tuple(npf), tuple(ntf), tuple(npb), tuple(ntb)

    pf, tf, pb, tb = lax.fori_loop(0, pitch, totals_body,
                                   ((ones,) * nv, (zeros,) * nv, (ones,) * nv, (zeros,) * nv), unroll=4)
    for m in range(nv):
        rows = slice(m * SUBLANES, (m + 1) * SUBLANES)
        tot[0, rows, :] = pf[m]
        tot[1, rows, :] = tf[m]
        tot[2, rows, :] = pb[m]
        tot[3, rows, :] = tb[m]

    c = jnp.zeros((1, w), F32)
    for j in range(RNN_SEGMENTS):
        cin[0, j:j + 1, :] = c
        c = tot[0, j:j + 1, :] * c + tot[1, j:j + 1, :]
    c = jnp.zeros((1, w), F32)
    for j in reversed(range(RNN_SEGMENTS)):
        cin[1, j:j + 1, :] = c
        c = tot[2, j:j + 1, :] * c + tot[3, j:j + 1, :]

    def scan_body(k, carry):
        h_f, h_b = carry
        kb = pitch - 1 - k
        nf, nb_ = [], []
        for m in range(nv):
            h = af[seg_rows(m, k), :] * h_f[m] + hf[seg_rows(m, k), :]
            of[seg_rows(m, k), :] = h
            nf.append(h)
            h = ab[seg_rows(m, kb), :] * h_b[m] + hb[seg_rows(m, kb), :]
            ob[seg_rows(m, kb), :] = h
            nb_.append(h)
        return tuple(nf), tuple(nb_)

    h0_f = tuple(cin[0, m * SUBLANES:(m + 1) * SUBLANES, :] for m in range(nv))
    h0_b = tuple(cin[1, m * SUBLANES:(m + 1) * SUBLANES, :] for m in range(nv))
    lax.fori_loop(0, pitch, scan_body, (h0_f, h0_b), unroll=4)

    def out_ctx(c, carry):
        rows = pl.ds(pl.multiple_of(c * tc, SUBLANES), tc)
        brows = pl.ds(pl.multiple_of(ll + c * tc, SUBLANES), tc)
        yc_ref[rows, :] = (_gelu_tanh(gc_ref[rows, :].astype(F32)) * (of[rows, :] + ob[brows, :])).astype(yc_ref.dtype)
        return carry

    def out_lat(c, carry):
        rows = pl.ds(pl.multiple_of(c * tc, SUBLANES), tc)
        frows = pl.ds(pl.multiple_of(lc + c * tc, SUBLANES), tc)
        yl_ref[rows, :] = (_gelu_tanh(gl_ref[rows, :].astype(F32)) * (of[frows, :] + ob[rows, :])).astype(yl_ref.dtype)
        return carry

    lax.fori_loop(0, lc // tc, out_ctx, 0)
    lax.fori_loop(0, ll // tc, out_lat, 0)


def _rglru(p_ctx, p_lat, conv_w, conv_b, gate_w, gate_b, lam, n_batch, lc, ll):
    w = RNN_BLOCK_W
    nb = RNN_BLOCKS
    s = lc + ll
    rx0 = COL_RX // w
    rg0 = COL_RG // w
    pitch = _seg_pitch(s)
    n_rows = RNN_SEGMENTS * pitch
    kern = functools.partial(_rglru_kernel, lc=lc, ll=ll, pitch=pitch)
    return pl.pallas_call(
        kern,
        grid=(n_batch, nb),
        in_specs=[pl.BlockSpec((lc, w), lambda b, c: (b, rx0 + c)),
                  pl.BlockSpec((ll, w), lambda b, c: (b, rx0 + c)),
                  pl.BlockSpec((lc, w), lambda b, c: (b, rg0 + c)),
                  pl.BlockSpec((ll, w), lambda b, c: (b, rg0 + c)),
                  pl.BlockSpec((RNN_CONV_W, w), lambda b, c: (0, c)),
                  pl.BlockSpec((1, w), lambda b, c: (0, c)),
                  pl.BlockSpec((None, w, 4 * w), lambda b, c: (c, 0, 0)),
                  pl.BlockSpec((None, 1, 4 * w), lambda b, c: (c, 0, 0)),
                  pl.BlockSpec((2, w), lambda b, c: (0, c))],
        out_specs=[pl.BlockSpec((lc, w), lambda b, c: (b, c)),
                   pl.BlockSpec((ll, w), lambda b, c: (b, c))],
        out_shape=[jax.ShapeDtypeStruct((n_batch * lc, D_RNN), BF16),
                   jax.ShapeDtypeStruct((n_batch * ll, D_RNN), BF16)],
        scratch_shapes=[pltpu.VMEM((s + 4 * SUBLANES, w), F32),
                        pltpu.VMEM((n_rows, w), F32), pltpu.VMEM((n_rows, w), F32),
                        pltpu.VMEM((n_rows, w), F32), pltpu.VMEM((n_rows, w), F32),
                        pltpu.VMEM((n_rows, w), F32), pltpu.VMEM((n_rows, w), F32),
                        pltpu.VMEM((4, RNN_SEGMENTS, w), F32), pltpu.VMEM((2, RNN_SEGMENTS, w), F32)],
        compiler_params=_cparams(2),
        name="rglru",
    )(p_ctx, p_lat, p_ctx, p_lat, conv_w, conv_b, gate_w, gate_b, lam)


ATT_TQ = 512
ATT_KEY_CHUNK = 512
LOG2E = 1.4426950408889634
ATT_SAFE_SPREAD = 96.0


def _attn_lat_kernel(q_ref, qn_ref, kc_ref, vc_ref, kl_ref, vl_ref, cq_ref, sq_ref, cqn_ref, sqn_ref, ck_ref, sk_ref,
                     qw_ref, kw_ref, o_ref, k_s, v_s, q_cur, q_nxt, qnorm_cur, qnorm_nxt, p_s, sc_s, *, lc):
    hd = ATT_HEAD_DIM
    half = hd // 4
    tq = q_ref.shape[0]
    s_len = k_s.shape[0]
    i = pl.program_id(2)
    qw = qw_ref[...] * (hd ** -0.5 * LOG2E)

    def prep_q(src_ref, cos_ref, sin_ref, q_dst, qnorm_dst, sc_idx):
        cos = cos_ref[...]
        sin = sin_ref[...]
        qmax = None
        for g in range(ATT_GROUP):
            rows = slice(g * tq, (g + 1) * tq)
            q = _rms_rows(src_ref[:, g * hd:(g + 1) * hd].astype(F32)) * qw
            q = _rope(q, cos, sin, half).astype(BF16)
            q_dst[rows, :] = q
            q32 = q.astype(F32)
            qn = jnp.sqrt(jnp.sum(q32 * q32, axis=-1, keepdims=True))
            qnorm_dst[rows, :] = jnp.broadcast_to(qn, (tq, hd))
            qmax = jnp.max(qn) if qmax is None else jnp.maximum(qmax, jnp.max(qn))
        sc_s[sc_idx] = qmax

    @pl.when(i == 0)
    def _():
        prep_q(q_ref, cq_ref, sq_ref, q_cur, qnorm_cur, 1)
        kw = kw_ref[...]
        kc = (_rms_rows(kc_ref[...].astype(F32)) * kw).astype(BF16)
        kl = _rope(_rms_rows(kl_ref[...].astype(F32)) * kw, ck_ref[...], sk_ref[...], half).astype(BF16)
        k_s[0:lc, :] = kc
        k_s[lc:, :] = kl
        v_s[0:lc, 0:hd] = vc_ref[...]
        v_s[lc:, 0:hd] = vl_ref[...]
        v_s[:, hd:] = jnp.ones((s_len, hd), BF16)
        kc32 = kc.astype(F32)
        kl32 = kl.astype(F32)
        kn_c = jnp.max(jnp.sqrt(jnp.sum(kc32 * kc32, axis=-1, keepdims=True)))
        kn_l = jnp.max(jnp.sqrt(jnp.sum(kl32 * kl32, axis=-1, keepdims=True)))
        sc_s[0] = jnp.maximum(kn_c, kn_l)

    kmax = sc_s[0]
    safe = 2.0 * sc_s[1] * kmax < ATT_SAFE_SPREAD

    @pl.when(safe)
    def _():
        prep_q(qn_ref, cqn_ref, sqn_ref, q_nxt, qnorm_nxt, 2)
        q_all = q_cur[...]
        bound = qnorm_cur[...] * kmax
        for c0 in range(0, s_len, ATT_KEY_CHUNK):
            n = min(ATT_KEY_CHUNK, s_len - c0)
            s = lax.dot_general(q_all, k_s[c0:c0 + n, :], (((1,), (1,)), ((), ())), preferred_element_type=F32)
            p_s[:, c0:c0 + n] = jnp.exp2(s - jnp.concatenate([bound] * (n // hd), axis=1)).astype(BF16)
        acc = jnp.dot(p_s[...], v_s[...], preferred_element_type=F32)
        for g in range(ATT_GROUP):
            a = acc[g * tq:(g + 1) * tq, :]
            o_ref[:, g * hd:(g + 1) * hd] = (a[:, 0:hd] / a[:, hd:]).astype(o_ref.dtype)

    @pl.when(jnp.logical_not(safe))
    def _():
        prep_q(qn_ref, cqn_ref, sqn_ref, q_nxt, qnorm_nxt, 2)
        k_all = k_s[...]
        v_all = v_s[:, 0:hd]
        for g in range(ATT_GROUP):
            s = lax.dot_general(q_cur[g * tq:(g + 1) * tq, :], k_all, (((1,), (1,)), ((), ())),
                                preferred_element_type=F32)
            p = jnp.exp2(s - jnp.max(s, axis=-1, keepdims=True))
            denom = jnp.sum(p, axis=-1, keepdims=True)
            o = jnp.dot(p.astype(BF16), v_all, preferred_element_type=F32)
            o_ref[:, g * hd:(g + 1) * hd] = (o / denom).astype(o_ref.dtype)

    q_cur[...] = q_nxt[...]
    qnorm_cur[...] = qnorm_nxt[...]
    sc_s[1] = sc_s[2]


def _attn_ctx_kernel(q_ref, kc_ref, vc_ref, qw_ref, kw_ref, o_ref):
    hd = ATT_HEAD_DIM
    k = (_rms_rows(kc_ref[...].astype(F32)) * kw_ref[...]).astype(BF16)
    qw = qw_ref[...] * (hd ** -0.5)
    v_c = vc_ref[...]
    for g in range(ATT_GROUP):
        q = (_rms_rows(q_ref[:, g * hd:(g + 1) * hd].astype(F32)) * qw).astype(BF16)
        s = lax.dot_general(q, k, (((1,), (1,)), ((), ())), preferred_element_type=F32)
        m = jnp.max(s, axis=-1, keepdims=True)
        p = jnp.exp(s - m)
        denom = jnp.sum(p, axis=-1, keepdims=True)
        o = jnp.dot(p.astype(BF16), v_c, preferred_element_type=F32)
        o_ref[:, g * hd:(g + 1) * hd] = (o / denom).astype(o_ref.dtype)


def _attn_lat(p_ctx, p_lat, cos, sin, q_norm_w, k_norm_w, n_batch, lc, ll):
    hd = ATT_HEAD_DIM
    gw = ATT_GROUP * hd
    tq = ATT_TQ
    nq = ll // tq
    q0 = COL_AQ // gw
    k0 = COL_AK // hd
    v0 = COL_AV // hd
    kern = functools.partial(_attn_lat_kernel, lc=lc)

    def nxt(i):
        return jnp.minimum(i + 1, nq - 1)

    return pl.pallas_call(
        kern,
        grid=(n_batch, ATT_KV_HEADS, nq),
        in_specs=[pl.BlockSpec((tq, gw), lambda b, h, i: (b * nq + i, q0 + h)),
                  pl.BlockSpec((tq, gw), lambda b, h, i: (b * nq + nxt(i), q0 + h)),
                  pl.BlockSpec((lc, hd), lambda b, h, i: (b, k0 + h)),
                  pl.BlockSpec((lc, hd), lambda b, h, i: (b, v0 + h)),
                  pl.BlockSpec((ll, hd), lambda b, h, i: (b, k0 + h)),
                  pl.BlockSpec((ll, hd), lambda b, h, i: (b, v0 + h)),
                  pl.BlockSpec((tq, hd), lambda b, h, i: (i, 0)),
                  pl.BlockSpec((tq, hd), lambda b, h, i: (i, 0)),
                  pl.BlockSpec((tq, hd), lambda b, h, i: (nxt(i), 0)),
                  pl.BlockSpec((tq, hd), lambda b, h, i: (nxt(i), 0)),
                  pl.BlockSpec((ll, hd), lambda b, h, i: (0, 0), pipeline_mode=pl.Buffered(1)),
                  pl.BlockSpec((ll, hd), lambda b, h, i: (0, 0), pipeline_mode=pl.Buffered(1)),
                  pl.BlockSpec((1, hd), lambda b, h, i: (0, 0)),
                  pl.BlockSpec((1, hd), lambda b, h, i: (0, 0))],
        out_specs=pl.BlockSpec((tq, gw), lambda b, h, i: (b * nq + i, h)),
        out_shape=jax.ShapeDtypeStruct((n_batch * ll, ATT_HEADS * hd), BF16),
        scratch_shapes=[pltpu.VMEM((lc + ll, hd), BF16),
                        pltpu.VMEM((lc + ll, 2 * hd), BF16),
                        pltpu.VMEM((ATT_GROUP * tq, hd), BF16), pltpu.VMEM((ATT_GROUP * tq, hd), BF16),
                        pltpu.VMEM((ATT_GROUP * tq, hd), F32), pltpu.VMEM((ATT_GROUP * tq, hd), F32),
                        pltpu.VMEM((ATT_GROUP * tq, lc + ll), BF16),
                        pltpu.SMEM((3,), F32)],
        compiler_params=_cparams(3),
        name="attn_lat",
    )(p_lat, p_lat, p_ctx, p_ctx, p_lat, p_lat, cos, sin, cos, sin, cos, sin, q_norm_w, k_norm_w)


def _attn_ctx(p_ctx, q_norm_w, k_norm_w, n_batch, lc):
    hd = ATT_HEAD_DIM
    gw = ATT_GROUP * hd
    q0 = COL_AQ // gw
    k0 = COL_AK // hd
    v0 = COL_AV // hd
    return pl.pallas_call(
        _attn_ctx_kernel,
        grid=(n_batch, ATT_KV_HEADS),
        in_specs=[pl.BlockSpec((lc, gw), lambda b, h: (b, q0 + h)),
                  pl.BlockSpec((lc, hd), lambda b, h: (b, k0 + h)),
                  pl.BlockSpec((lc, hd), lambda b, h: (b, v0 + h)),
                  pl.BlockSpec((1, hd), lambda b, h: (0, 0)),
                  pl.BlockSpec((1, hd), lambda b, h: (0, 0))],
        out_specs=pl.BlockSpec((lc, gw), lambda b, h: (b, h)),
        out_shape=jax.ShapeDtypeStruct((n_batch * lc, ATT_HEADS * hd), BF16),
        compiler_params=_cparams(2),
        name="attn_ctx",
    )(p_ctx, p_ctx, p_ctx, q_norm_w, k_norm_w)


def _ret_kernel(qc_ref, kc_ref, vc_ref, gc_ref, ql_ref, kl_ref, vl_ref, gl_ref, rcos_ref, rsin_ref, ccos_ref,
                csin_ref, dec_ref, nw_ref, yc_ref, yl_ref,
                qb_s, kb_s, qf_s, qr_s, kf_s, sbs_s, sf_s, sb_s, o_a, o_b, *, lc, ll):
    c = RET_CHUNK
    dk = RET_QK_DIM
    dv = RET_V_DIM
    half = dk // 4
    n_c = lc // c
    n_l = ll // c
    rows_per_chunk = c // GRID_W
    dec = dec_ref[...]
    log_g = jnp.minimum(dec, 0.0) - jnp.log(1.0 + jnp.exp(-jnp.abs(dec)))
    lg_f = log_g[0:1, :]
    lg_b = log_g[1:2, :]
    pos = lax.broadcasted_iota(jnp.int32, (c, dk), 0).astype(F32)
    qd_f = jnp.exp((pos + 1.0) * lg_f)
    kd_f = jnp.exp((c - 1.0 - pos) * lg_f)
    cd_f = jnp.exp(float(c) * lg_f)
    qd_b = jnp.exp((c - pos) * lg_b)
    kd_b = jnp.exp(pos * lg_b)
    cd_b = jnp.exp(float(c) * lg_b)
    ii = lax.broadcasted_iota(jnp.int32, (c, c), 0)
    jj = lax.broadcasted_iota(jnp.int32, (c, c), 1)
    diff = (ii - jj).astype(F32)
    intra = jnp.where(ii >= jj, jnp.exp(jnp.maximum(diff, 0.0) * lg_f[:, 0:c]),
                      jnp.exp(jnp.maximum(-diff, 0.0) * lg_b[:, 0:c]))
    nw = nw_ref[...]
    k_scale = dk ** -0.5
    col_cos = jnp.concatenate([ccos_ref[...]] * rows_per_chunk, axis=0)
    col_sin = jnp.concatenate([csin_ref[...]] * rows_per_chunk, axis=0)

    def rope_tables(ci):
        def row_part(ref):
            return jnp.concatenate(
                [jnp.broadcast_to(ref[pl.ds(ci * rows_per_chunk + r, 1), :], (GRID_W, dk // 2))
                 for r in range(rows_per_chunk)], axis=0)
        return (jnp.concatenate([row_part(rcos_ref), col_cos], axis=1),
                jnp.concatenate([row_part(rsin_ref), col_sin], axis=1))

    def state_update(s_ref, k_dec, v, c_dec):
        kv = lax.dot_general(k_dec, v, (((0,), (0,)), ((), ())), preferred_element_type=F32)
        s_ref[...] = s_ref[...] * c_dec + kv

    def sweep1_chunk(q, k, v, g, orows):
        qb_s[orows, :] = q.astype(BF16)
        kb_s[orows, :] = k.astype(BF16)
        qf_s[orows, :] = (q * qd_f).astype(BF16)
        qr_s[orows, :] = (q * qd_b).astype(BF16)
        kf_s[orows, :] = (k * kd_f).astype(BF16)
        sbs_s[g] = sb_s[...].astype(BF16)
        state_update(sb_s, (k * kd_b).astype(BF16), v, cd_b)

    sf_s[...] = jnp.zeros((dk, dv), F32)
    sb_s[...] = jnp.zeros((dk, dv), F32)

    for ci in reversed(range(n_c)):
        rows = pl.ds(ci * c, c)
        sweep1_chunk(qc_ref[rows, :].astype(F32), kc_ref[rows, :].astype(F32) * k_scale, vc_ref[rows, :], ci, rows)

    def sweep1_body(i, carry):
        ci = n_l - 1 - i
        rows = pl.ds(pl.multiple_of(ci * c, c), c)
        orows = pl.ds(pl.multiple_of(lc + ci * c, c), c)
        cos, sin = rope_tables(ci)
        q = _rope(ql_ref[rows, :].astype(F32), cos, sin, half)
        k = _rope(kl_ref[rows, :].astype(F32), cos, sin, half) * k_scale
        sweep1_chunk(q, k, vl_ref[rows, :], n_c + ci, orows)
        return carry

    lax.fori_loop(0, n_l, sweep1_body, 0, unroll=4)

    def raw_out(g, orows, v):
        att = lax.dot_general(qb_s[orows, :], kb_s[orows, :], (((1,), (1,)), ((), ())),
                              preferred_element_type=F32) * intra
        o = (jnp.dot(att.astype(BF16), v, preferred_element_type=F32)
             + jnp.dot(qf_s[orows, :], sf_s[...].astype(BF16), preferred_element_type=F32)
             + jnp.dot(qr_s[orows, :], sbs_s[g], preferred_element_type=F32))
        state_update(sf_s, kf_s[orows, :], v, cd_f)
        return o

    def finish(o, gate):
        mu = jnp.mean(o, axis=-1, keepdims=True)
        oc = o - mu
        var = jnp.mean(oc * oc, axis=-1, keepdims=True)
        y = oc * lax.rsqrt(var + EPS) * nw
        return _silu(gate) * y

    for ci in range(n_c):
        rows = pl.ds(ci * c, c)
        yc_ref[rows, :] = finish(raw_out(ci, rows, vc_ref[rows, :]), gc_ref[rows, :].astype(F32)).astype(yc_ref.dtype)

    gsz = RET_GROUP
    n_groups = n_l // gsz

    def group_matmuls(grp, o_dst):
        for j in range(gsz):
            ci = grp * gsz + j
            rows = pl.ds(pl.multiple_of(ci * c, c), c)
            orows = pl.ds(pl.multiple_of(lc + ci * c, c), c)
            o_dst[j * c:(j + 1) * c, :] = raw_out(n_c + ci, orows, vl_ref[rows, :])

    def group_finish(grp, o_src):
        for j in range(gsz):
            ci = grp * gsz + j
            rows = pl.ds(pl.multiple_of(ci * c, c), c)
            yl_ref[rows, :] = finish(o_src[j * c:(j + 1) * c, :], gl_ref[rows, :].astype(F32)).astype(yl_ref.dtype)

    group_matmuls(0, o_a)

    def pair_body(p, carry):
        group_matmuls(2 * p + 1, o_b)
        group_finish(2 * p, o_a)
        group_matmuls(2 * p + 2, o_a)
        group_finish(2 * p + 1, o_b)
        return carry

    lax.fori_loop(0, n_groups // 2 - 1, pair_body, 0)
    group_matmuls(n_groups - 1, o_b)
    group_finish(n_groups - 2, o_a)
    group_finish(n_groups - 1, o_b)


def _retention(p_ctx, p_lat, cos, sin, dec, norm_w, n_batch, lc, ll):
    dk = RET_QK_DIM
    q0 = COL_RQ // dk
    k0 = COL_RK // dk
    v0 = COL_RV // dk
    g0 = COL_RGT // dk
    n_chunks = (lc + ll) // RET_CHUNK
    assert RET_CHUNK % GRID_W == 0 and ll % (2 * RET_GROUP * RET_CHUNK) == 0
    row_cos, row_sin = cos[::GRID_W, :dk // 2], sin[::GRID_W, :dk // 2]
    col_cos, col_sin = cos[:GRID_W, dk // 2:], sin[:GRID_W, dk // 2:]
    kern = functools.partial(_ret_kernel, lc=lc, ll=ll)

    def col(off):
        return lambda b, h: (b, off + h)

    const = lambda b, h: (0, 0)
    return pl.pallas_call(
        kern,
        grid=(n_batch, RET_HEADS),
        in_specs=[pl.BlockSpec((lc, dk), col(q0)), pl.BlockSpec((lc, dk), col(k0)),
                  pl.BlockSpec((lc, dk), col(v0)), pl.BlockSpec((lc, dk), col(g0)),
                  pl.BlockSpec((ll, dk), col(q0)), pl.BlockSpec((ll, dk), col(k0)),
                  pl.BlockSpec((ll, dk), col(v0)), pl.BlockSpec((ll, dk), col(g0)),
                  pl.BlockSpec(row_cos.shape, const), pl.BlockSpec(row_sin.shape, const),
                  pl.BlockSpec(col_cos.shape, const), pl.BlockSpec(col_sin.shape, const),
                  pl.BlockSpec((None, 2, dk), lambda b, h: (h, 0, 0)),
                  pl.BlockSpec((1, dk), lambda b, h: (0, h))],
        out_specs=[pl.BlockSpec((lc, dk), lambda b, h: (b, h)),
                   pl.BlockSpec((ll, dk), lambda b, h: (b, h))],
        out_shape=[jax.ShapeDtypeStruct((n_batch * lc, RET_HEADS * RET_V_DIM), BF16),
                   jax.ShapeDtypeStruct((n_batch * ll, RET_HEADS * RET_V_DIM), BF16)],
        scratch_shapes=[pltpu.VMEM((lc + ll, dk), BF16)] * 5
        + [pltpu.VMEM((n_chunks, dk, RET_V_DIM), BF16)]
        + [pltpu.VMEM((dk, RET_V_DIM), F32)] * 2
        + [pltpu.VMEM((RET_GROUP * RET_CHUNK, RET_V_DIM), F32)] * 2,
        compiler_params=_cparams(2),
        name="retention",
    )(p_ctx, p_ctx, p_ctx, p_ctx, p_lat, p_lat, p_lat, p_lat, row_cos, row_sin, col_cos, col_sin, dec, norm_w)


def _merge_kernel(x_ref, yr_ref, ya_ref, yt_ref, g0_ref, g1_ref, g2_ref, wb_ref, wo_ref, m_ref, nw_ref,
                  xo_ref, vo_ref):
    m = (_sigmoid(g0_ref[...].astype(F32)) * jnp.dot(yr_ref[...], wb_ref[0], preferred_element_type=F32)
         + _sigmoid(g1_ref[...].astype(F32)) * jnp.dot(ya_ref[...], wb_ref[1], preferred_element_type=F32)
         + _sigmoid(g2_ref[...].astype(F32)) * jnp.dot(yt_ref[...], wb_ref[2], preferred_element_type=F32))
    out = jnp.dot(m.astype(BF16), wo_ref[...], preferred_element_type=F32)
    x1 = x_ref[...] + m_ref[2:3, :] * out
    xo_ref[...] = x1
    v = _rms_rows(x1) * nw_ref[...]
    vo_ref[...] = (v * (1.0 + m_ref[4:5, :]) + m_ref[3:4, :]).astype(vo_ref.dtype)


def _merge(x2d, y_rnn, y_att, y_ret, p, mods, mod_row, norm_w, w_branch, w_out, tm):
    n_tok, d = x2d.shape
    bg0 = COL_BG // d
    tok = lambda i: (i, 0)
    const2 = lambda i: (0, 0)
    return pl.pallas_call(
        _merge_kernel,
        grid=(n_tok // tm,),
        in_specs=[pl.BlockSpec((tm, d), tok), pl.BlockSpec((tm, d), tok), pl.BlockSpec((tm, d), tok),
                  pl.BlockSpec((tm, d), tok),
                  pl.BlockSpec((tm, d), lambda i: (i, bg0)),
                  pl.BlockSpec((tm, d), lambda i: (i, bg0 + 1)),
                  pl.BlockSpec((tm, d), lambda i: (i, bg0 + 2)),
                  pl.BlockSpec((N_BRANCH, d, d), lambda i: (0, 0, 0)),
                  pl.BlockSpec((d, d), const2),
                  pl.BlockSpec((None, 6, d), lambda i: (mod_row(i), 0, 0)),
                  pl.BlockSpec((1, d), const2)],
        out_specs=[pl.BlockSpec((tm, d), tok), pl.BlockSpec((tm, d), tok)],
        out_shape=[jax.ShapeDtypeStruct((n_tok, d), F32), jax.ShapeDtypeStruct((n_tok, d), BF16)],
        compiler_params=_cparams(1),
        name="merge",
    )(x2d, y_rnn, y_att, y_ret, p, p, p, w_branch, w_out, mods, norm_w.reshape(1, d))


FFN_CHUNK = 256
FFN_HALO = 16


def _ffn_kernel(v_ref, vp_ref, vn_ref, wu_ref, cw_ref, cb_ref, wd_ref, x_ref, m_ref, fw_ref, o_ref,
                vext_s, a_s, h_s, *, tiles_per_seq, final):
    tm = v_ref.shape[0]
    f = h_s.shape[1]
    fc = FFN_CHUNK
    n_slab = fc // LANES
    ti = pl.program_id(0) % tiles_per_seq
    vext_s[0:FFN_HALO, :] = vp_ref[...]
    vext_s[FFN_HALO:FFN_HALO + tm, :] = v_ref[...]
    vext_s[FFN_HALO + tm:, :] = vn_ref[...]
    v_ext = vext_s[...]
    v_main = v_ref[...]
    first = ti == 0
    last = ti == tiles_per_seq - 1
    row0 = jnp.minimum(pl.program_id(0), 0)
    cw = cw_ref[...] * 0.5
    cb = cb_ref[...] * 0.5
    for c in range(f // fc):
        cols = slice(c * fc, (c + 1) * fc)
        a_ext = jnp.dot(v_ext, wu_ref[:, cols], preferred_element_type=F32)
        b = jnp.dot(v_main, wu_ref[:, f + c * fc:f + (c + 1) * fc], preferred_element_type=F32)
        taps = [[], [], []]
        for j in range(n_slab):
            slab = (c % 2) * n_slab + j
            lanes = slice(j * LANES, (j + 1) * LANES)
            a_s[slab] = a_ext[:, lanes]
            a_s[slab, FFN_HALO - 1:FFN_HALO, :] = jnp.where(first, 0.0, a_ext[FFN_HALO - 1:FFN_HALO, lanes])
            a_s[slab, FFN_HALO + tm:FFN_HALO + tm + 1, :] = jnp.where(
                last, 0.0, a_ext[FFN_HALO + tm:FFN_HALO + tm + 1, lanes])
            for k in range(FFN_CONV_W):
                taps[k].append(a_s[slab, pl.ds(row0 + (FFN_HALO - 1 + k), tm), :])
        a_m1, a_0, a_p1 = (jnp.concatenate(t, axis=1) for t in taps)
        half_conv = cw[0:1, cols] * a_m1 + cw[1:2, cols] * a_0 + cw[2:3, cols] * a_p1 + cb[:, cols]
        h_s[:, cols] = ((half_conv * (jnp.tanh(half_conv) + 1.0)) * b).astype(BF16)
    out = jnp.dot(h_s[...], wd_ref[...], preferred_element_type=F32)
    x2 = x_ref[...] + m_ref[5:6, :] * out
    if final:
        x2 = _rms_rows(x2) * fw_ref[...]
    o_ref[...] = x2


def _ffn(v2d, w_up, conv_w, conv_b, w_down, x2d, mods, mod_row, final_w, tm, seq_len, final):
    n_tok, d = x2d.shape
    f = D_FF
    tiles_per_seq = seq_len // tm
    hb = tm // FFN_HALO
    n_hblk = n_tok // FFN_HALO
    kern = functools.partial(_ffn_kernel, tiles_per_seq=tiles_per_seq, final=final)
    resident = dict(pipeline_mode=pl.Buffered(1))
    return pl.pallas_call(
        kern,
        grid=(n_tok // tm,),
        in_specs=[pl.BlockSpec((tm, d), lambda i: (i, 0)),
                  pl.BlockSpec((FFN_HALO, d), lambda i: (jnp.maximum(i * hb - 1, 0), 0)),
                  pl.BlockSpec((FFN_HALO, d), lambda i: (jnp.minimum((i + 1) * hb, n_hblk - 1), 0)),
                  pl.BlockSpec((d, 2 * f), lambda i: (0, 0), **resident),
                  pl.BlockSpec((FFN_CONV_W, f), lambda i: (0, 0)),
                  pl.BlockSpec((1, f), lambda i: (0, 0)),
                  pl.BlockSpec((f, d), lambda i: (0, 0), **resident),
                  pl.BlockSpec((tm, d), lambda i: (i, 0)),
                  pl.BlockSpec((None, 6, d), lambda i: (mod_row(i), 0, 0)),
                  pl.BlockSpec((1, d), lambda i: (0, 0))],
        out_specs=pl.BlockSpec((tm, d), lambda i: (i, 0)),
        out_shape=jax.ShapeDtypeStruct((n_tok, d), F32),
        scratch_shapes=[pltpu.VMEM((tm + 2 * FFN_HALO, d), BF16),
                        pltpu.VMEM((2 * FFN_CHUNK // LANES, tm + 2 * FFN_HALO, LANES), F32),
                        pltpu.VMEM((tm, f), BF16)],
        compiler_params=_cparams(1),
        name="ffn",
    )(v2d, v2d, v2d, w_up, conv_w, conv_b.reshape(1, f), w_down, x2d, mods, final_w.reshape(1, d))


def _rope_tables(seq_len, head_dim):
    n_freq = head_dim // 4
    t = np.arange(seq_len)
    row = (t // GRID_W).astype(np.float32)
    col = (t % GRID_W).astype(np.float32)
    inv = (ROPE_THETA ** (-np.arange(n_freq, dtype=np.float32) / n_freq)).astype(np.float32)
    ang = np.stack([row[:, None] * inv, col[:, None] * inv], axis=1).astype(np.float64)
    cos = np.cos(ang)
    sin = np.sin(ang)
    cos_t = np.stack([cos, cos], axis=2).reshape(seq_len, head_dim).astype(np.float32)
    sin_t = np.stack([-sin, sin], axis=2).reshape(seq_len, head_dim).astype(np.float32)
    return jnp.asarray(cos_t), jnp.asarray(sin_t)


def kernel(x, c, ctx, c_ctx, mod_w, mod_b, norm1_w, norm2_w, w_in, rnn_conv_w, rnn_conv_b, rglru_w, rglru_b,
           rglru_lam, q_norm_w, k_norm_w, ret_decay, ret_norm_w, w_branch, w_out, ffn_up, ffn_conv_w,
           ffn_conv_b, ffn_down, final_norm_w):
    n_batch, ll, d = x.shape
    lc = ctx.shape[1]
    depth = mod_w.shape[0]
    assert d == D_MODEL and n_batch + 1 <= MOD_ROWS
    assert ll % 2048 == 0 and lc % RNN_CHUNK == 0 and lc % RET_CHUNK == 0

    cos_a, sin_a = _rope_tables(ll, ATT_HEAD_DIM)
    cos_r, sin_r = _rope_tables(ll, RET_QK_DIM)

    c_all = jnp.zeros((MOD_ROWS, d), F32).at[:n_batch].set(c).at[n_batch].set(c_ctx)
    mods = _modulation(c_all, mod_w, mod_b).reshape(depth, MOD_ROWS, 6, d)

    tm_lat = 2048
    tm_ctx = lc
    tm_ctx_big = next(t for t in (1024, 512, 256, lc) if (n_batch * lc) % t == 0)
    tm_ctx_mix = min(tm_ctx_big, 512)
    lat_tiles = ll // tm_lat
    lat_row_big = lambda i: i // lat_tiles
    ctx_row = lambda i: n_batch
    tm_mix = 512
    lat_row_mix = lambda i: i // (ll // tm_mix)

    x2 = x.reshape(n_batch * ll, d)
    cx2 = ctx.reshape(n_batch * lc, d)

    for l in range(depth):
        last = l == depth - 1
        w_in_p = jnp.concatenate([w_in[l][:, s:s + n] for s, n in _REF_SEGMENTS], axis=1).astype(BF16)
        gate_w = jnp.transpose(rglru_w[l], (2, 3, 0, 1, 4)).reshape(RNN_BLOCKS, RNN_BLOCK_W, 4 * RNN_BLOCK_W)
        gate_w = gate_w.astype(BF16)
        gate_b = jnp.transpose(rglru_b[l].reshape(2, 2, RNN_BLOCKS, RNN_BLOCK_W), (2, 0, 1, 3))
        gate_b = gate_b.reshape(RNN_BLOCKS, 1, 4 * RNN_BLOCK_W)
        dec = jnp.broadcast_to(jnp.transpose(ret_decay[l])[:, :, None], (RET_HEADS, 2, RET_QK_DIM))
        wb = w_branch[l].astype(BF16)
        wo = w_out[l].astype(BF16)
        w_up = ffn_up[l].astype(BF16)
        w_dn = ffn_down[l].astype(BF16)
        ml = mods[l]

        p_lat = _inproj(x2, ml, lat_row_big, norm1_w[l], w_in_p, tm_lat)
        p_ctx = _inproj(cx2, ml, ctx_row, norm1_w[l], w_in_p, tm_ctx_big)

        yr_c, yr_l = _rglru(p_ctx, p_lat, rnn_conv_w[l], rnn_conv_b[l].reshape(1, D_RNN), gate_w, gate_b,
                            rglru_lam[l], n_batch, lc, ll)
        ya_l = _attn_lat(p_ctx, p_lat, cos_a, sin_a, q_norm_w[l].reshape(1, -1), k_norm_w[l].reshape(1, -1),
                         n_batch, lc, ll)
        yt_c, yt_l = _retention(p_ctx, p_lat, cos_r, sin_r, dec, ret_norm_w[l].reshape(1, -1), n_batch, lc, ll)

        x2, v_lat = _merge(x2, yr_l, ya_l, yt_l, p_lat, ml, lat_row_mix, norm2_w[l], wb, wo, tm_mix)
        x2 = _ffn(v_lat, w_up, ffn_conv_w[l], ffn_conv_b[l], w_dn, x2, ml, lat_row_mix, final_norm_w,
                  tm_mix, ll, final=last)

        if not last:
            ya_c = _attn_ctx(p_ctx, q_norm_w[l].reshape(1, -1), k_norm_w[l].reshape(1, -1), n_batch, lc)
            cx2, v_ctx = _merge(cx2, yr_c, ya_c, yt_c, p_ctx, ml, ctx_row, norm2_w[l], wb, wo, tm_ctx_mix)
            cx2 = _ffn(v_ctx, w_up, ffn_conv_w[l], ffn_conv_b[l], w_dn, cx2, ml, ctx_row, final_norm_w,
                       tm_ctx, lc, final=False)

    return x2.reshape(n_batch, ll, d)
```

```python
import functools

import numpy as np
import jax
import jax.numpy as jnp
from jax import lax
from jax.experimental import pallas as pl
from jax.experimental.pallas import tpu as pltpu

F32 = jnp.float32
BF16 = jnp.bfloat16

D_MODEL = 1024
GRID_W = 64
D_RNN = 1024
RNN_BLOCKS = 8
RNN_BLOCK_W = D_RNN // RNN_BLOCKS
RNN_CONV_W = 4
RNN_PAD_L = 2
RG_C = 8.0
ATT_HEADS = 8
ATT_KV_HEADS = 2
ATT_HEAD_DIM = 128
ATT_GROUP = ATT_HEADS // ATT_KV_HEADS
ROPE_THETA = 10000.0
RET_HEADS = 4
RET_QK_DIM = 256
RET_V_DIM = 256
RET_CHUNK = 128
RET_GROUP = 8
N_BRANCH = 3
D_FF = 2816
FFN_CONV_W = 3
EPS = 1e-6

COL_RX = 0
COL_RG = 1024
COL_AQ = 2048
COL_RQ = 3072
COL_RK = 4096
COL_RV = 5120
COL_RGT = 6144
COL_BG = 7168
COL_AK = 10240
COL_AV = 10496
IN_COLS = 10752
_REF_SEGMENTS = ((0, 1024), (1024, 1024), (2048, 1024), (3584, 1024), (4608, 1024), (5632, 1024),
                 (6656, 1024), (7680, 3072), (3072, 256), (3328, 256))

SUBLANES = 8
LANES = 128
MOD_ROWS = 16
VMEM_LIMIT = 52 * 1024 * 1024


def _cparams(n_grid):
    return pltpu.CompilerParams(dimension_semantics=("arbitrary",) * n_grid, vmem_limit_bytes=VMEM_LIMIT)


def _sigmoid(x):
    return 0.5 * jnp.tanh(0.5 * x) + 0.5


def _silu(x):
    return x * _sigmoid(x)


def _gelu_tanh(x):
    c = float(np.sqrt(2.0 / np.pi))
    half = 0.5 * x
    return half * jnp.tanh(x * (c + (c * 0.044715) * (x * x))) + half


def _rms_rows(x):
    return x * lax.rsqrt(jnp.mean(x * x, axis=-1, keepdims=True) + EPS)


def _rope(x, cos, sin_signed, half):
    n = x.shape[-1]
    from_lo = pltpu.roll(x, half, axis=1)
    from_hi = pltpu.roll(x, n - half, axis=1)
    lane = lax.broadcasted_iota(jnp.int32, x.shape, 1)
    partner = jnp.where((lane & half) != 0, from_lo, from_hi)
    return x * cos + partner * sin_signed


def _mod_kernel(c_ref, w_ref, b_ref, o_ref):
    sc = _silu(c_ref[...])
    o_ref[...] = jnp.dot(sc.astype(BF16), w_ref[...].astype(BF16), preferred_element_type=F32) + b_ref[...]


def _modulation(c_all, mod_w, mod_b):
    depth, d, n = mod_w.shape
    tn = 1536
    return pl.pallas_call(
        _mod_kernel,
        grid=(depth, n // tn),
        in_specs=[pl.BlockSpec((MOD_ROWS, d), lambda l, j: (0, 0)),
                  pl.BlockSpec((None, d, tn), lambda l, j: (l, 0, j)),
                  pl.BlockSpec((None, 1, tn), lambda l, j: (l, 0, j))],
        out_specs=pl.BlockSpec((None, MOD_ROWS, tn), lambda l, j: (l, 0, j)),
        out_shape=jax.ShapeDtypeStruct((depth, MOD_ROWS, n), F32),
        compiler_params=_cparams(2),
        name="modulation",
    )(c_all, mod_w, mod_b.reshape(depth, 1, n))


def _inproj_kernel(x_ref, m_ref, nw_ref, w_ref, o_ref, u_ref):
    @pl.when(pl.program_id(1) == 0)
    def _():
        y = _rms_rows(x_ref[...]) * nw_ref[...]
        u_ref[...] = (y * (1.0 + m_ref[1:2, :]) + m_ref[0:1, :]).astype(BF16)

    o_ref[...] = jnp.dot(u_ref[...], w_ref[...], preferred_element_type=F32).astype(o_ref.dtype)


def _inproj(x2d, mods, mod_row, norm_w, w, tm):
    n_tok, d = x2d.shape
    n = w.shape[1]
    tn = n // 3
    return pl.pallas_call(
        _inproj_kernel,
        grid=(n_tok // tm, n // tn),
        in_specs=[pl.BlockSpec((tm, d), lambda i, j: (i, 0)),
                  pl.BlockSpec((None, 6, d), lambda i, j: (mod_row(i), 0, 0)),
                  pl.BlockSpec((1, d), lambda i, j: (0, 0)),
                  pl.BlockSpec((d, tn), lambda i, j: (0, j))],
        out_specs=pl.BlockSpec((tm, tn), lambda i, j: (i, j)),
        out_shape=jax.ShapeDtypeStruct((n_tok, n), BF16),
        scratch_shapes=[pltpu.VMEM((tm, d), BF16)],
        compiler_params=_cparams(2),
        name="inproj",
    )(x2d, mods, norm_w.reshape(1, d), w)


RNN_CHUNK = 256


RNN_SEGMENTS = 32
RNN_SEG_VREGS = RNN_SEGMENTS // SUBLANES
TINY = 1e-37


def _seg_pitch(n_rows):
    p = -(-n_rows // RNN_SEGMENTS)
    while p % 8 != 4:
        p += 1
    return p


def _rglru_kernel(xc_ref, xl_ref, gc_ref, gl_ref, cw_ref, cb_ref, gw_ref, gb_ref, lam_ref,
                  yc_ref, yl_ref, xs, af, hf, ab, hb, of, ob, tot, cin, *, lc, ll, pitch):
    w = xs.shape[1]
    tc = RNN_CHUNK
    pad = SUBLANES
    s_len = lc + ll
    n_rows = RNN_SEGMENTS * pitch
    for ref in (af, hf, ab, hb):
        ref[s_len:n_rows, :] = jnp.zeros((n_rows - s_len, w), F32)
    zero_pad = jnp.zeros((pad, w), F32)
    c_base = pad
    l_base = lc + 3 * pad
    xs[0:pad, :] = zero_pad
    xs[c_base + lc:c_base + lc + 2 * pad, :] = jnp.zeros((2 * pad, w), F32)
    xs[l_base + ll:l_base + ll + pad, :] = zero_pad
    xs[c_base:c_base + lc, :] = xc_ref[...].astype(F32)
    xs[l_base:l_base + ll, :] = xl_ref[...].astype(F32)

    lam = lam_ref[...]
    sp = jnp.maximum(-lam, 0.0) + jnp.log(1.0 + jnp.exp(-jnp.abs(lam)))
    half_rate_f = sp[0:1, :] * (-0.5 * RG_C * LOG2E)
    half_rate_b = sp[1:2, :] * (-0.5 * RG_C * LOG2E)
    cw = cw_ref[...]
    cb = cb_ref[...]
    gw_half = (gw_ref[...].astype(F32) * 0.5).astype(BF16)
    gb_half = gb_ref[...] * 0.5

    def gates_chunk(xs_base, base_f, base_b, t0):
        x = cb + jnp.zeros((tc, w), F32)
        for k in range(RNN_CONV_W):
            x = x + cw[k:k + 1, :] * xs[pl.ds(xs_base + t0 + (k - RNN_PAD_L), tc), :]
        g = jnp.dot(x.astype(BF16), gw_half, preferred_element_type=F32) + gb_half
        x_half = 0.5 * x
        for direction, (a_ref, b_ref, half_rate, base) in enumerate(
                ((af, hf, half_rate_f, base_f), (ab, hb, half_rate_b, base_b))):
            t_r = jnp.tanh(g[:, (2 * direction) * w:(2 * direction + 1) * w])
            t_i = jnp.tanh(g[:, (2 * direction + 1) * w:(2 * direction + 2) * w])
            a = jnp.exp2(t_r * half_rate + half_rate)
            z = 1.0 - a * a
            rows = pl.ds(pl.multiple_of(base + t0, SUBLANES), tc)
            a_ref[rows, :] = a
            b_ref[rows, :] = (z * lax.rsqrt(jnp.maximum(z, TINY))) * (t_i * x_half + x_half)

    def ctx_body(c, carry):
        gates_chunk(c_base, 0, ll, c * tc)
        return carry

    def lat_body(c, carry):
        gates_chunk(l_base, lc, 0, c * tc)
        return carry

    lax.fori_loop(0, lc // tc, ctx_body, 0)
    lax.fori_loop(0, ll // tc, lat_body, 0, unroll=4)

    def seg_rows(m, k):
        return pl.ds(m * SUBLANES * pitch + k, SUBLANES, stride=pitch)

    nv = RNN_SEG_VREGS
    ones = jnp.ones((SUBLANES, w), F32)
    zeros = jnp.zeros((SUBLANES, w), F32)

    def totals_body(k, carry):
        pf, tf, pb, tb = carry
        kb = pitch - 1 - k
        npf, ntf, npb, ntb = [], [], [], []
        for m in range(nv):
            a = af[seg_rows(m, k), :]
            npf.append(a * pf[m])
            ntf.append(a * tf[m] + hf[seg_rows(m, k), :])
            a = ab[seg_rows(m, kb), :]
            npb.append(a * pb[m])
            ntb.append(a * tb[m] + hb[seg_rows(m, kb), :])
        return tuple(npf), tuple(ntf), tuple(npb), tuple(ntb)

    pf, tf, pb, tb = lax.fori_loop(0, pitch, totals_body,
                                   ((ones,) * nv, (zeros,) * nv, (ones,) * nv, (zeros,) * nv), unroll=4)
    for m in range(nv):
        rows = slice(m * SUBLANES, (m + 1) * SUBLANES)
        tot[0, rows, :] = pf[m]
        tot[1, rows, :] = tf[m]
        tot[2, rows, :] = pb[m]
        tot[3, rows, :] = tb[m]

    c = jnp.zeros((1, w), F32)
    for j in range(RNN_SEGMENTS):
        cin[0, j:j + 1, :] = c
        c = tot[0, j:j + 1, :] * c + tot[1, j:j + 1, :]
    c = jnp.zeros((1, w), F32)
    for j in reversed(range(RNN_SEGMENTS)):
        cin[1, j:j + 1, :] = c
        c = tot[2, j:j + 1, :] * c + tot[3, j:j + 1, :]

    def scan_body(k, carry):
        h_f, h_b = carry
        kb = pitch - 1 - k
        nf, nb_ = [], []
        for m in range(nv):
            h = af[seg_rows(m, k), :] * h_f[m] + hf[seg_rows(m, k), :]
            of[seg_rows(m, k), :] = h
            nf.append(h)
            h = ab[seg_rows(m, kb), :] * h_b[m] + hb[seg_rows(m, kb), :]
            ob[seg_rows(m, kb), :] = h
            nb_.append(h)
        return tuple(nf), tuple(nb_)

    h0_f = tuple(cin[0, m * SUBLANES:(m + 1) * SUBLANES, :] for m in range(nv))
    h0_b = tuple(cin[1, m * SUBLANES:(m + 1) * SUBLANES, :] for m in range(nv))
    lax.fori_loop(0, pitch, scan_body, (h0_f, h0_b), unroll=4)

    def out_ctx(c, carry):
        rows = pl.ds(pl.multiple_of(c * tc, SUBLANES), tc)
        brows = pl.ds(pl.multiple_of(ll + c * tc, SUBLANES), tc)
        yc_ref[rows, :] = (_gelu_tanh(gc_ref[rows, :].astype(F32)) * (of[rows, :] + ob[brows, :])).astype(yc_ref.dtype)
        return carry

    def out_lat(c, carry):
        rows = pl.ds(pl.multiple_of(c * tc, SUBLANES), tc)
        frows = pl.ds(pl.multiple_of(lc + c * tc, SUBLANES), tc)
        yl_ref[rows, :] = (_gelu_tanh(gl_ref[rows, :].astype(F32)) * (of[frows, :] + ob[rows, :])).astype(yl_ref.dtype)
        return carry

    lax.fori_loop(0, lc // tc, out_ctx, 0)
    lax.fori_loop(0, ll // tc, out_lat, 0)


def _rglru(p_ctx, p_lat, conv_w, conv_b, gate_w, gate_b, lam, n_batch, lc, ll):
    w = RNN_BLOCK_W
    nb = RNN_BLOCKS
    s = lc + ll
    rx0 = COL_RX // w
    rg0 = COL_RG // w
    pitch = _seg_pitch(s)
    n_rows = RNN_SEGMENTS * pitch
    kern = functools.partial(_rglru_kernel, lc=lc, ll=ll, pitch=pitch)
    return pl.pallas_call(
        kern,
        grid=(n_batch, nb),
        in_specs=[pl.BlockSpec((lc, w), lambda b, c: (b, rx0 + c)),
                  pl.BlockSpec((ll, w), lambda b, c: (b, rx0 + c)),
                  pl.BlockSpec((lc, w), lambda b, c: (b, rg0 + c)),
                  pl.BlockSpec((ll, w), lambda b, c: (b, rg0 + c)),
                  pl.BlockSpec((RNN_CONV_W, w), lambda b, c: (0, c)),
                  pl.BlockSpec((1, w), lambda b, c: (0, c)),
                  pl.BlockSpec((None, w, 4 * w), lambda b, c: (c, 0, 0)),
                  pl.BlockSpec((None, 1, 4 * w), lambda b, c: (c, 0, 0)),
                  pl.BlockSpec((2, w), lambda b, c: (0, c))],
        out_specs=[pl.BlockSpec((lc, w), lambda b, c: (b, c)),
                   pl.BlockSpec((ll, w), lambda b, c: (b, c))],
        out_shape=[jax.ShapeDtypeStruct((n_batch * lc, D_RNN), BF16),
                   jax.ShapeDtypeStruct((n_batch * ll, D_RNN), BF16)],
        scratch_shapes=[pltpu.VMEM((s + 4 * SUBLANES, w), F32),
                        pltpu.VMEM((n_rows, w), F32), pltpu.VMEM((n_rows, w), F32),
                        pltpu.VMEM((n_rows, w), F32), pltpu.VMEM((n_rows, w), F32),
                        pltpu.VMEM((n_rows, w), F32), pltpu.VMEM((n_rows, w), F32),
                        pltpu.VMEM((4, RNN_SEGMENTS, w), F32), pltpu.VMEM((2, RNN_SEGMENTS, w), F32)],
        compiler_params=_cparams(2),
        name="rglru",
    )(p_ctx, p_lat, p_ctx, p_lat, conv_w, conv_b, gate_w, gate_b, lam)


ATT_TQ = 512
ATT_KEY_CHUNK = 512
LOG2E = 1.4426950408889634
ATT_SAFE_SPREAD = 96.0


def _attn_lat_kernel(q_ref, qn_ref, kc_ref, vc_ref, kl_ref, vl_ref, cq_ref, sq_ref, cqn_ref, sqn_ref, ck_ref, sk_ref,
                     qw_ref, kw_ref, o_ref, k_s, v_s, q_cur, q_nxt, qnorm_cur, qnorm_nxt, p_s, sc_s, *, lc):
    hd = ATT_HEAD_DIM
    half = hd // 4
    tq = q_ref.shape[0]
    s_len = k_s.shape[0]
    i = pl.program_id(2)
    qw = qw_ref[...] * (hd ** -0.5 * LOG2E)

    def prep_q(src_ref, cos_ref, sin_ref, q_dst, qnorm_dst, sc_idx):
        cos = cos_ref[...]
        sin = sin_ref[...]
        qmax = None
        for g in range(ATT_GROUP):
            rows = slice(g * tq, (g + 1) * tq)
            q = _rms_rows(src_ref[:, g * hd:(g + 1) * hd].astype(F32)) * qw
            q = _rope(q, cos, sin, half).astype(BF16)
            q_dst[rows, :] = q
            q32 = q.astype(F32)
            qn = jnp.sqrt(jnp.sum(q32 * q32, axis=-1, keepdims=True))
            qnorm_dst[rows, :] = jnp.broadcast_to(qn, (tq, hd))
            qmax = jnp.max(qn) if qmax is None else jnp.maximum(qmax, jnp.max(qn))
        sc_s[sc_idx] = qmax

    @pl.when(i == 0)
    def _():
        prep_q(q_ref, cq_ref, sq_ref, q_cur, qnorm_cur, 1)
        kw = kw_ref[...]
        kc = (_rms_rows(kc_ref[...].astype(F32)) * kw).astype(BF16)
        kl = _rope(_rms_rows(kl_ref[...].astype(F32)) * kw, ck_ref[...], sk_ref[...], half).astype(BF16)
        k_s[0:lc, :] = kc
        k_s[lc:, :] = kl
        v_s[0:lc, 0:hd] = vc_ref[...]
        v_s[lc:, 0:hd] = vl_ref[...]
        v_s[:, hd:] = jnp.ones((s_len, hd), BF16)
        kc32 = kc.astype(F32)
        kl32 = kl.astype(F32)
        kn_c = jnp.max(jnp.sqrt(jnp.sum(kc32 * kc32, axis=-1, keepdims=True)))
        kn_l = jnp.max(jnp.sqrt(jnp.sum(kl32 * kl32, axis=-1, keepdims=True)))
        sc_s[0] = jnp.maximum(kn_c, kn_l)

    kmax = sc_s[0]
    safe = 2.0 * sc_s[1] * kmax < ATT_SAFE_SPREAD

    @pl.when(safe)
    def _():
        prep_q(qn_ref, cqn_ref, sqn_ref, q_nxt, qnorm_nxt, 2)
        q_all = q_cur[...]
        bound = qnorm_cur[...] * kmax
        for c0 in range(0, s_len, ATT_KEY_CHUNK):
            n = min(ATT_KEY_CHUNK, s_len - c0)
            s = lax.dot_general(q_all, k_s[c0:c0 + n, :], (((1,), (1,)), ((), ())), preferred_element_type=F32)
            p_s[:, c0:c0 + n] = jnp.exp2(s - jnp.concatenate([bound] * (n // hd), axis=1)).astype(BF16)
        acc = jnp.dot(p_s[...], v_s[...], preferred_element_type=F32)
        for g in range(ATT_GROUP):
            a = acc[g * tq:(g + 1) * tq, :]
            o_ref[:, g * hd:(g + 1) * hd] = (a[:, 0:hd] / a[:, hd:]).astype(o_ref.dtype)

    @pl.when(jnp.logical_not(safe))
    def _():
        prep_q(qn_ref, cqn_ref, sqn_ref, q_nxt, qnorm_nxt, 2)
        k_all = k_s[...]
        v_all = v_s[:, 0:hd]
        for g in range(ATT_GROUP):
            s = lax.dot_general(q_cur[g * tq:(g + 1) * tq, :], k_all, (((1,), (1,)), ((), ())),
                                preferred_element_type=F32)
            p = jnp.exp2(s - jnp.max(s, axis=-1, keepdims=True))
            denom = jnp.sum(p, axis=-1, keepdims=True)
            o = jnp.dot(p.astype(BF16), v_all, preferred_element_type=F32)
            o_ref[:, g * hd:(g + 1) * hd] = (o / denom).astype(o_ref.dtype)

    q_cur[...] = q_nxt[...]
    qnorm_cur[...] = qnorm_nxt[...]
    sc_s[1] = sc_s[2]


def _attn_ctx_kernel(q_ref, kc_ref, vc_ref, qw_ref, kw_ref, o_ref):
    hd = ATT_HEAD_DIM
    k = (_rms_rows(kc_ref[...].astype(F32)) * kw_ref[...]).astype(BF16)
    qw = qw_ref[...] * (hd ** -0.5)
    v_c = vc_ref[...]
    for g in range(ATT_GROUP):
        q = (_rms_rows(q_ref[:, g * hd:(g + 1) * hd].astype(F32)) * qw).astype(BF16)
        s = lax.dot_general(q, k, (((1,), (1,)), ((), ())), preferred_element_type=F32)
        m = jnp.max(s, axis=-1, keepdims=True)
        p = jnp.exp(s - m)
        denom = jnp.sum(p, axis=-1, keepdims=True)
        o = jnp.dot(p.astype(BF16), v_c, preferred_element_type=F32)
        o_ref[:, g * hd:(g + 1) * hd] = (o / denom).astype(o_ref.dtype)


def _attn_lat(p_ctx, p_lat, cos, sin, q_norm_w, k_norm_w, n_batch, lc, ll):
    hd = ATT_HEAD_DIM
    gw = ATT_GROUP * hd
    tq = ATT_TQ
    nq = ll // tq
    q0 = COL_AQ // gw
    k0 = COL_AK // hd
    v0 = COL_AV // hd
    kern = functools.partial(_attn_lat_kernel, lc=lc)

    def nxt(i):
        return jnp.minimum(i + 1, nq - 1)

    return pl.pallas_call(
        kern,
        grid=(n_batch, ATT_KV_HEADS, nq),
        in_specs=[pl.BlockSpec((tq, gw), lambda b, h, i: (b * nq + i, q0 + h)),
                  pl.BlockSpec((tq, gw), lambda b, h, i: (b * nq + nxt(i), q0 + h)),
                  pl.BlockSpec((lc, hd), lambda b, h, i: (b, k0 + h)),
                  pl.BlockSpec((lc, hd), lambda b, h, i: (b, v0 + h)),
                  pl.BlockSpec((ll, hd), lambda b, h, i: (b, k0 + h)),
                  pl.BlockSpec((ll, hd), lambda b, h, i: (b, v0 + h)),
                  pl.BlockSpec((tq, hd), lambda b, h, i: (i, 0)),
                  pl.BlockSpec((tq, hd), lambda b, h, i: (i, 0)),
                  pl.BlockSpec((tq, hd), lambda b, h, i: (nxt(i), 0)),
                  pl.BlockSpec((tq, hd), lambda b, h, i: (nxt(i), 0)),
                  pl.BlockSpec((ll, hd), lambda b, h, i: (0, 0), pipeline_mode=pl.Buffered(1)),
                  pl.BlockSpec((ll, hd), lambda b, h, i: (0, 0), pipeline_mode=pl.Buffered(1)),
                  pl.BlockSpec((1, hd), lambda b, h, i: (0, 0)),
                  pl.BlockSpec((1, hd), lambda b, h, i: (0, 0))],
        out_specs=pl.BlockSpec((tq, gw), lambda b, h, i: (b * nq + i, h)),
        out_shape=jax.ShapeDtypeStruct((n_batch * ll, ATT_HEADS * hd), BF16),
        scratch_shapes=[pltpu.VMEM((lc + ll, hd), BF16),
                        pltpu.VMEM((lc + ll, 2 * hd), BF16),
                        pltpu.VMEM((ATT_GROUP * tq, hd), BF16), pltpu.VMEM((ATT_GROUP * tq, hd), BF16),
                        pltpu.VMEM((ATT_GROUP * tq, hd), F32), pltpu.VMEM((ATT_GROUP * tq, hd), F32),
                        pltpu.VMEM((ATT_GROUP * tq, lc + ll), BF16),
                        pltpu.SMEM((3,), F32)],
        compiler_params=_cparams(3),
        name="attn_lat",
    )(p_lat, p_lat, p_ctx, p_ctx, p_lat, p_lat, cos, sin, cos, sin, cos, sin, q_norm_w, k_norm_w)


def _attn_ctx(p_ctx, q_norm_w, k_norm_w, n_batch, lc):
    hd = ATT_HEAD_DIM
    gw = ATT_GROUP * hd
    q0 = COL_AQ // gw
    k0 = COL_AK // hd
    v0 = COL_AV // hd
    return pl.pallas_call(
        _attn_ctx_kernel,
        grid=(n_batch, ATT_KV_HEADS),
        in_specs=[pl.BlockSpec((lc, gw), lambda b, h: (b, q0 + h)),
                  pl.BlockSpec((lc, hd), lambda b, h: (b, k0 + h)),
                  pl.BlockSpec((lc, hd), lambda b, h: (b, v0 + h)),
                  pl.BlockSpec((1, hd), lambda b, h: (0, 0)),
                  pl.BlockSpec((1, hd), lambda b, h: (0, 0))],
        out_specs=pl.BlockSpec((lc, gw), lambda b, h: (b, h)),
        out_shape=jax.ShapeDtypeStruct((n_batch * lc, ATT_HEADS * hd), BF16),
        compiler_params=_cparams(2),
        name="attn_ctx",
    )(p_ctx, p_ctx, p_ctx, q_norm_w, k_norm_w)


def _ret_kernel(qc_ref, kc_ref, vc_ref, gc_ref, ql_ref, kl_ref, vl_ref, gl_ref, rcos_ref, rsin_ref, ccos_ref,
                csin_ref, dec_ref, nw_ref, yc_ref, yl_ref,
                qb_s, kb_s, qf_s, qr_s, kf_s, sbs_s, sf_s, sb_s, o_a, o_b, *, lc, ll):
    c = RET_CHUNK
    dk = RET_QK_DIM
    dv = RET_V_DIM
    half = dk // 4
    n_c = lc // c
    n_l = ll // c
    rows_per_chunk = c // GRID_W
    dec = dec_ref[...]
    log_g = jnp.minimum(dec, 0.0) - jnp.log(1.0 + jnp.exp(-jnp.abs(dec)))
    lg_f = log_g[0:1, :]
    lg_b = log_g[1:2, :]
    pos = lax.broadcasted_iota(jnp.int32, (c, dk), 0).astype(F32)
    qd_f = jnp.exp((pos + 1.0) * lg_f)
    kd_f = jnp.exp((c - 1.0 - pos) * lg_f)
    cd_f = jnp.exp(float(c) * lg_f)
    qd_b = jnp.exp((c - pos) * lg_b)
    kd_b = jnp.exp(pos * lg_b)
    cd_b = jnp.exp(float(c) * lg_b)
    ii = lax.broadcasted_iota(jnp.int32, (c, c), 0)
    jj = lax.broadcasted_iota(jnp.int32, (c, c), 1)
    diff = (ii - jj).astype(F32)
    intra = jnp.where(ii >= jj, jnp.exp(jnp.maximum(diff, 0.0) * lg_f[:, 0:c]),
                      jnp.exp(jnp.maximum(-diff, 0.0) * lg_b[:, 0:c]))
    nw = nw_ref[...]
    k_scale = dk ** -0.5
    col_cos = jnp.concatenate([ccos_ref[...]] * rows_per_chunk, axis=0)
    col_sin = jnp.concatenate([csin_ref[...]] * rows_per_chunk, axis=0)

    def rope_tables(ci):
        def row_part(ref):
            return jnp.concatenate(
                [jnp.broadcast_to(ref[pl.ds(ci * rows_per_chunk + r, 1), :], (GRID_W, dk // 2))
                 for r in range(rows_per_chunk)], axis=0)
        return (jnp.concatenate([row_part(rcos_ref), col_cos], axis=1),
                jnp.concatenate([row_part(rsin_ref), col_sin], axis=1))

    def state_update(s_ref, k_dec, v, c_dec):
        kv = lax.dot_general(k_dec, v, (((0,), (0,)), ((), ())), preferred_element_type=F32)
        s_ref[...] = s_ref[...] * c_dec + kv

    def sweep1_chunk(q, k, v, g, orows):
        qb_s[orows, :] = q.astype(BF16)
        kb_s[orows, :] = k.astype(BF16)
        qf_s[orows, :] = (q * qd_f).astype(BF16)
        qr_s[orows, :] = (q * qd_b).astype(BF16)
        kf_s[orows, :] = (k * kd_f).astype(BF16)
        sbs_s[g] = sb_s[...].astype(BF16)
        state_update(sb_s, (k * kd_b).astype(BF16), v, cd_b)

    sf_s[...] = jnp.zeros((dk, dv), F32)
    sb_s[...] = jnp.zeros((dk, dv), F32)

    for ci in reversed(range(n_c)):
        rows = pl.ds(ci * c, c)
        sweep1_chunk(qc_ref[rows, :].astype(F32), kc_ref[rows, :].astype(F32) * k_scale, vc_ref[rows, :], ci, rows)

    def sweep1_body(i, carry):
        ci = n_l - 1 - i
        rows = pl.ds(pl.multiple_of(ci * c, c), c)
        orows = pl.ds(pl.multiple_of(lc + ci * c, c), c)
        cos, sin = rope_tables(ci)
        q = _rope(ql_ref[rows, :].astype(F32), cos, sin, half)
        k = _rope(kl_ref[rows, :].astype(F32), cos, sin, half) * k_scale
        sweep1_chunk(q, k, vl_ref[rows, :], n_c + ci, orows)
        return carry

    lax.fori_loop(0, n_l, sweep1_body, 0, unroll=4)

    def raw_out(g, orows, v):
        att = lax.dot_general(qb_s[orows, :], kb_s[orows, :], (((1,), (1,)), ((), ())),
                              preferred_element_type=F32) * intra
        o = (jnp.dot(att.astype(BF16), v, preferred_element_type=F32)
             + jnp.dot(qf_s[orows, :], sf_s[...].astype(BF16), preferred_element_type=F32)
             + jnp.dot(qr_s[orows, :], sbs_s[g], preferred_element_type=F32))
        state_update(sf_s, kf_s[orows, :], v, cd_f)
        return o

    def finish(o, gate):
        mu = jnp.mean(o, axis=-1, keepdims=True)
        oc = o - mu
        var = jnp.mean(oc * oc, axis=-1, keepdims=True)
        y = oc * lax.rsqrt(var + EPS) * nw
        return _silu(gate) * y

    for ci in range(n_c):
        rows = pl.ds(ci * c, c)
        yc_ref[rows, :] = finish(raw_out(ci, rows, vc_ref[rows, :]), gc_ref[rows, :].astype(F32)).astype(yc_ref.dtype)

    gsz = RET_GROUP
    n_groups = n_l // gsz

    def group_matmuls(grp, o_dst):
        for j in range(gsz):
            ci = grp * gsz + j
            rows = pl.ds(pl.multiple_of(ci * c, c), c)
            orows = pl.ds(pl.multiple_of(lc + ci * c, c), c)
            o_dst[j * c:(j + 1) * c, :] = raw_out(n_c + ci, orows, vl_ref[rows, :])

    def group_finish(grp, o_src):
        for j in range(gsz):
            ci = grp * gsz + j
            rows = pl.ds(pl.multiple_of(ci * c, c), c)
            yl_ref[rows, :] = finish(o_src[j * c:(j + 1) * c, :], gl_ref[rows, :].astype(F32)).astype(yl_ref.dtype)

    group_matmuls(0, o_a)

    def pair_body(p, carry):
        group_matmuls(2 * p + 1, o_b)
        group_finish(2 * p, o_a)
        group_matmuls(2 * p + 2, o_a)
        group_finish(2 * p + 1, o_b)
        return carry

    lax.fori_loop(0, n_groups // 2 - 1, pair_body, 0)
    group_matmuls(n_groups - 1, o_b)
    group_finish(n_groups - 2, o_a)
    group_finish(n_groups - 1, o_b)


def _retention(p_ctx, p_lat, cos, sin, dec, norm_w, n_batch, lc, ll):
    dk = RET_QK_DIM
    q0 = COL_RQ // dk
    k0 = COL_RK // dk
    v0 = COL_RV // dk
    g0 = COL_RGT // dk
    n_chunks = (lc + ll) // RET_CHUNK
    assert RET_CHUNK % GRID_W == 0 and ll % (2 * RET_GROUP * RET_CHUNK) == 0
    row_cos, row_sin = cos[::GRID_W, :dk // 2], sin[::GRID_W, :dk // 2]
    col_cos, col_sin = cos[:GRID_W, dk // 2:], sin[:GRID_W, dk // 2:]
    kern = functools.partial(_ret_kernel, lc=lc, ll=ll)

    def col(off):
        return lambda b, h: (b, off + h)

    const = lambda b, h: (0, 0)
    return pl.pallas_call(
        kern,
        grid=(n_batch, RET_HEADS),
        in_specs=[pl.BlockSpec((lc, dk), col(q0)), pl.BlockSpec((lc, dk), col(k0)),
                  pl.BlockSpec((lc, dk), col(v0)), pl.BlockSpec((lc, dk), col(g0)),
                  pl.BlockSpec((ll, dk), col(q0)), pl.BlockSpec((ll, dk), col(k0)),
                  pl.BlockSpec((ll, dk), col(v0)), pl.BlockSpec((ll, dk), col(g0)),
                  pl.BlockSpec(row_cos.shape, const), pl.BlockSpec(row_sin.shape, const),
                  pl.BlockSpec(col_cos.shape, const), pl.BlockSpec(col_sin.shape, const),
                  pl.BlockSpec((None, 2, dk), lambda b, h: (h, 0, 0)),
                  pl.BlockSpec((1, dk), lambda b, h: (0, h))],
        out_specs=[pl.BlockSpec((lc, dk), lambda b, h: (b, h)),
                   pl.BlockSpec((ll, dk), lambda b, h: (b, h))],
        out_shape=[jax.ShapeDtypeStruct((n_batch * lc, RET_HEADS * RET_V_DIM), BF16),
                   jax.ShapeDtypeStruct((n_batch * ll, RET_HEADS * RET_V_DIM), BF16)],
        scratch_shapes=[pltpu.VMEM((lc + ll, dk), BF16)] * 5
        + [pltpu.VMEM((n_chunks, dk, RET_V_DIM), BF16)]
        + [pltpu.VMEM((dk, RET_V_DIM), F32)] * 2
        + [pltpu.VMEM((RET_GROUP * RET_CHUNK, RET_V_DIM), F32)] * 2,
        compiler_params=_cparams(2),
        name="retention",
    )(p_ctx, p_ctx, p_ctx, p_ctx, p_lat, p_lat, p_lat, p_lat, row_cos, row_sin, col_cos, col_sin, dec, norm_w)


def _merge_kernel(x_ref, yr_ref, ya_ref, yt_ref, g0_ref, g1_ref, g2_ref, wb_ref, wo_ref, m_ref, nw_ref,
                  xo_ref, vo_ref):
    m = (_sigmoid(g0_ref[...].astype(F32)) * jnp.dot(yr_ref[...], wb_ref[0], preferred_element_type=F32)
         + _sigmoid(g1_ref[...].astype(F32)) * jnp.dot(ya_ref[...], wb_ref[1], preferred_element_type=F32)
         + _sigmoid(g2_ref[...].astype(F32)) * jnp.dot(yt_ref[...], wb_ref[2], preferred_element_type=F32))
    out = jnp.dot(m.astype(BF16), wo_ref[...], preferred_element_type=F32)
    x1 = x_ref[...] + m_ref[2:3, :] * out
    xo_ref[...] = x1
    v = _rms_rows(x1) * nw_ref[...]
    vo_ref[...] = (v * (1.0 + m_ref[4:5, :]) + m_ref[3:4, :]).astype(vo_ref.dtype)


def _merge(x2d, y_rnn, y_att, y_ret, p, mods, mod_row, norm_w, w_branch, w_out, tm):
    n_tok, d = x2d.shape
    bg0 = COL_BG // d
    tok = lambda i: (i, 0)
    const2 = lambda i: (0, 0)
    return pl.pallas_call(
        _merge_kernel,
        grid=(n_tok // tm,),
        in_specs=[pl.BlockSpec((tm, d), tok), pl.BlockSpec((tm, d), tok), pl.BlockSpec((tm, d), tok),
                  pl.BlockSpec((tm, d), tok),
                  pl.BlockSpec((tm, d), lambda i: (i, bg0)),
                  pl.BlockSpec((tm, d), lambda i: (i, bg0 + 1)),
                  pl.BlockSpec((tm, d), lambda i: (i, bg0 + 2)),
                  pl.BlockSpec((N_BRANCH, d, d), lambda i: (0, 0, 0)),
                  pl.BlockSpec((d, d), const2),
                  pl.BlockSpec((None, 6, d), lambda i: (mod_row(i), 0, 0)),
                  pl.BlockSpec((1, d), const2)],
        out_specs=[pl.BlockSpec((tm, d), tok), pl.BlockSpec((tm, d), tok)],
        out_shape=[jax.ShapeDtypeStruct((n_tok, d), F32), jax.ShapeDtypeStruct((n_tok, d), BF16)],
        compiler_params=_cparams(1),
        name="merge",
    )(x2d, y_rnn, y_att, y_ret, p, p, p, w_branch, w_out, mods, norm_w.reshape(1, d))


FFN_CHUNK = 256
FFN_HALO = 16


def _ffn_kernel(v_ref, vp_ref, vn_ref, wu_ref, cw_ref, cb_ref, wd_ref, x_ref, m_ref, fw_ref, o_ref,
                vext_s, a_s, h_s, *, tiles_per_seq, final):
    tm = v_ref.shape[0]
    f = h_s.shape[1]
    fc = FFN_CHUNK
    n_slab = fc // LANES
    ti = pl.program_id(0) % tiles_per_seq
    vext_s[0:FFN_HALO, :] = vp_ref[...]
    vext_s[FFN_HALO:FFN_HALO + tm, :] = v_ref[...]
    vext_s[FFN_HALO + tm:, :] = vn_ref[...]
    v_ext = vext_s[...]
    v_main = v_ref[...]
    first = ti == 0
    last = ti == tiles_per_seq - 1
    row0 = jnp.minimum(pl.program_id(0), 0)
    cw = cw_ref[...] * 0.5
    cb = cb_ref[...] * 0.5
    for c in range(f // fc):
        cols = slice(c * fc, (c + 1) * fc)
        a_ext = jnp.dot(v_ext, wu_ref[:, cols], preferred_element_type=F32)
        b = jnp.dot(v_main, wu_ref[:, f + c * fc:f + (c + 1) * fc], preferred_element_type=F32)
        taps = [[], [], []]
        for j in range(n_slab):
            slab = (c % 2) * n_slab + j
            lanes = slice(j * LANES, (j + 1) * LANES)
            a_s[slab] = a_ext[:, lanes]
            a_s[slab, FFN_HALO - 1:FFN_HALO, :] = jnp.where(first, 0.0, a_ext[FFN_HALO - 1:FFN_HALO, lanes])
            a_s[slab, FFN_HALO + tm:FFN_HALO + tm + 1, :] = jnp.where(
                last, 0.0, a_ext[FFN_HALO + tm:FFN_HALO + tm + 1, lanes])
            for k in range(FFN_CONV_W):
                taps[k].append(a_s[slab, pl.ds(row0 + (FFN_HALO - 1 + k), tm), :])
        a_m1, a_0, a_p1 = (jnp.concatenate(t, axis=1) for t in taps)
        half_conv = cw[0:1, cols] * a_m1 + cw[1:2, cols] * a_0 + cw[2:3, cols] * a_p1 + cb[:, cols]
        h_s[:, cols] = ((half_conv * (jnp.tanh(half_conv) + 1.0)) * b).astype(BF16)
    out = jnp.dot(h_s[...], wd_ref[...], preferred_element_type=F32)
    x2 = x_ref[...] + m_ref[5:6, :] * out
    if final:
        x2 = _rms_rows(x2) * fw_ref[...]
    o_ref[...] = x2


def _ffn(v2d, w_up, conv_w, conv_b, w_down, x2d, mods, mod_row, final_w, tm, seq_len, final):
    n_tok, d = x2d.shape
    f = D_FF
    tiles_per_seq = seq_len // tm
    hb = tm // FFN_HALO
    n_hblk = n_tok // FFN_HALO
    assert f % FFN_CHUNK == 0 and tm % FFN_HALO == 0
    kern = functools.partial(_ffn_kernel, tiles_per_seq=tiles_per_seq, final=final)
    resident = dict(pipeline_mode=pl.Buffered(1))
    return pl.pallas_call(
        kern,
        grid=(n_tok // tm,),
        in_specs=[pl.BlockSpec((tm, d), lambda i: (i, 0)),
                  pl.BlockSpec((FFN_HALO, d), lambda i: (jnp.maximum(i * hb - 1, 0), 0)),
                  pl.BlockSpec((FFN_HALO, d), lambda i: (jnp.minimum((i + 1) * hb, n_hblk - 1), 0)),
                  pl.BlockSpec((d, 2 * f), lambda i: (0, 0), **resident),
                  pl.BlockSpec((FFN_CONV_W, f), lambda i: (0, 0)),
                  pl.BlockSpec((1, f), lambda i: (0, 0)),
                  pl.BlockSpec((f, d), lambda i: (0, 0), **resident),
                  pl.BlockSpec((tm, d), lambda i: (i, 0)),
                  pl.BlockSpec((None, 6, d), lambda i: (mod_row(i), 0, 0)),
                  pl.BlockSpec((1, d), lambda i: (0, 0))],
        out_specs=pl.BlockSpec((tm, d), lambda i: (i, 0)),
        out_shape=jax.ShapeDtypeStruct((n_tok, d), F32),
        scratch_shapes=[pltpu.VMEM((tm + 2 * FFN_HALO, d), BF16),
                        pltpu.VMEM((2 * FFN_CHUNK // LANES, tm + 2 * FFN_HALO, LANES), F32),
                        pltpu.VMEM((tm, f), BF16)],
        compiler_params=_cparams(1),
        name="ffn",
    )(v2d, v2d, v2d, w_up, conv_w, conv_b.reshape(1, f), w_down, x2d, mods, final_w.reshape(1, d))


def _rope_tables(seq_len, head_dim):
    n_freq = head_dim // 4
    t = np.arange(seq_len)
    row = (t // GRID_W).astype(np.float32)
    col = (t % GRID_W).astype(np.float32)
    inv = (ROPE_THETA ** (-np.arange(n_freq, dtype=np.float32) / n_freq)).astype(np.float32)
    ang = np.stack([row[:, None] * inv, col[:, None] * inv], axis=1).astype(np.float64)
    cos = np.cos(ang)
    sin = np.sin(ang)
    cos_t = np.stack([cos, cos], axis=2).reshape(seq_len, head_dim).astype(np.float32)
    sin_t = np.stack([-sin, sin], axis=2).reshape(seq_len, head_dim).astype(np.float32)
    return jnp.asarray(cos_t), jnp.asarray(sin_t)


def kernel(x, c, ctx, c_ctx, mod_w, mod_b, norm1_w, norm2_w, w_in, rnn_conv_w, rnn_conv_b, rglru_w, rglru_b,
           rglru_lam, q_norm_w, k_norm_w, ret_decay, ret_norm_w, w_branch, w_out, ffn_up, ffn_conv_w,
           ffn_conv_b, ffn_down, final_norm_w):
    n_batch, ll, d = x.shape
    lc = ctx.shape[1]
    depth = mod_w.shape[0]
    assert d == D_MODEL and n_batch + 1 <= MOD_ROWS
    assert ll % 1024 == 0 and lc % RNN_CHUNK == 0 and lc % RET_CHUNK == 0

    cos_a, sin_a = _rope_tables(ll, ATT_HEAD_DIM)
    cos_r, sin_r = _rope_tables(ll, RET_QK_DIM)

    c_all = jnp.zeros((MOD_ROWS, d), F32).at[:n_batch].set(c).at[n_batch].set(c_ctx)
    mods = _modulation(c_all, mod_w, mod_b).reshape(depth, MOD_ROWS, 6, d)

    tm_lat = 1024
    tm_ctx = lc
    tm_ctx_big = next(t for t in (1024, 512, 256, lc) if (n_batch * lc) % t == 0)
    tm_ctx_mix = min(tm_ctx_big, 512)
    lat_tiles = ll // tm_lat
    lat_row_big = lambda i: i // lat_tiles
    ctx_row = lambda i: n_batch
    tm_mix = 512
    lat_row_mix = lambda i: i // (ll // tm_mix)

    x2 = x.reshape(n_batch * ll, d)
    cx2 = ctx.reshape(n_batch * lc, d)

    for l in range(depth):
        last = l == depth - 1
        w_in_p = jnp.concatenate([w_in[l][:, s:s + n] for s, n in _REF_SEGMENTS], axis=1).astype(BF16)
        gate_w = jnp.transpose(rglru_w[l], (2, 3, 0, 1, 4)).reshape(RNN_BLOCKS, RNN_BLOCK_W, 4 * RNN_BLOCK_W)
        gate_w = gate_w.astype(BF16)
        gate_b = jnp.transpose(rglru_b[l].reshape(2, 2, RNN_BLOCKS, RNN_BLOCK_W), (2, 0, 1, 3))
        gate_b = gate_b.reshape(RNN_BLOCKS, 1, 4 * RNN_BLOCK_W)
        dec = jnp.broadcast_to(jnp.transpose(ret_decay[l])[:, :, None], (RET_HEADS, 2, RET_QK_DIM))
        wb = w_branch[l].astype(BF16)
        wo = w_out[l].astype(BF16)
        w_up = ffn_up[l].astype(BF16)
        w_dn = ffn_down[l].astype(BF16)
        ml = mods[l]

        p_lat = _inproj(x2, ml, lat_row_big, norm1_w[l], w_in_p, tm_lat)
        p_ctx = _inproj(cx2, ml, ctx_row, norm1_w[l], w_in_p, tm_ctx_big)

        yr_c, yr_l = _rglru(p_ctx, p_lat, rnn_conv_w[l], rnn_conv_b[l].reshape(1, D_RNN), gate_w, gate_b,
                            rglru_lam[l], n_batch, lc, ll)
        ya_l = _attn_lat(p_ctx, p_lat, cos_a, sin_a, q_norm_w[l].reshape(1, -1), k_norm_w[l].reshape(1, -1),
                         n_batch, lc, ll)
        yt_c, yt_l = _retention(p_ctx, p_lat, cos_r, sin_r, dec, ret_norm_w[l].reshape(1, -1), n_batch, lc, ll)

        x2, v_lat = _merge(x2, yr_l, ya_l, yt_l, p_lat, ml, lat_row_mix, norm2_w[l], wb, wo, tm_mix)
        x2 = _ffn(v_lat, w_up, ffn_conv_w[l], ffn_conv_b[l], w_dn, x2, ml, lat_row_mix, final_norm_w,
                  tm_mix, ll, final=last)

        if not last:
            ya_c = _attn_ctx(p_ctx, q_norm_w[l].reshape(1, -1), k_norm_w[l].reshape(1, -1), n_batch, lc)
            cx2, v_ctx = _merge(cx2, yr_c, ya_c, yt_c, p_ctx, ml, ctx_row, norm2_w[l], wb, wo, tm_ctx_mix)
            cx2 = _ffn(v_ctx, w_up, ffn_conv_w[l], ffn_conv_b[l], w_dn, cx2, ml, ctx_row, final_norm_w,
                       tm_ctx, lc, final=False)

    return x2.reshape(n_batch, ll, d)
```

```python
import functools

import numpy as np
import jax
import jax.numpy as jnp
from jax import lax
from jax.experimental import pallas as pl
from jax.experimental.pallas import tpu as pltpu

F32 = jnp.float32
BF16 = jnp.bfloat16

D_MODEL = 1024
GRID_W = 64
D_RNN = 1024
RNN_BLOCKS = 8
RNN_BLOCK_W = D_RNN // RNN_BLOCKS
RNN_CONV_W = 4
RNN_PAD_L = 2
RG_C = 8.0
ATT_HEADS = 8
ATT_KV_HEADS = 2
ATT_HEAD_DIM = 128
ATT_GROUP = ATT_HEADS // ATT_KV_HEADS
ROPE_THETA = 10000.0
RET_HEADS = 4
RET_QK_DIM = 256
RET_V_DIM = 256
RET_CHUNK = 128
RET_GROUP = 8
N_BRANCH = 3
D_FF = 2816
FFN_CONV_W = 3
EPS = 1e-6

COL_RX = 0
COL_RG = 1024
COL_AQ = 2048
COL_RQ = 3072
COL_RK = 4096
COL_RV = 5120
COL_RGT = 6144
COL_BG = 7168
COL_AK = 10240
COL_AV = 10496
IN_COLS = 10752
_REF_SEGMENTS = ((0, 1024), (1024, 1024), (2048, 1024), (3584, 1024), (4608, 1024), (5632, 1024),
                 (6656, 1024), (7680, 3072), (3072, 256), (3328, 256))

SUBLANES = 8
LANES = 128
MOD_ROWS = 16
VMEM_LIMIT = 52 * 1024 * 1024


def _cparams(n_grid):
    return pltpu.CompilerParams(dimension_semantics=("arbitrary",) * n_grid, vmem_limit_bytes=VMEM_LIMIT)


def _sigmoid(x):
    return 0.5 * jnp.tanh(0.5 * x) + 0.5


def _silu(x):
    return x * _sigmoid(x)


def _gelu_tanh(x):
    c = float(np.sqrt(2.0 / np.pi))
    half = 0.5 * x
    return half * jnp.tanh(x * (c + (c * 0.044715) * (x * x))) + half


def _rms_rows(x):
    return x * lax.rsqrt(jnp.mean(x * x, axis=-1, keepdims=True) + EPS)


def _rope(x, cos, sin_signed, half):
    n = x.shape[-1]
    from_lo = pltpu.roll(x, half, axis=1)
    from_hi = pltpu.roll(x, n - half, axis=1)
    lane = lax.broadcasted_iota(jnp.int32, x.shape, 1)
    partner = jnp.where((lane & half) != 0, from_lo, from_hi)
    return x * cos + partner * sin_signed


def _mod_kernel(c_ref, w_ref, b_ref, o_ref):
    sc = _silu(c_ref[...])
    o_ref[...] = jnp.dot(sc.astype(BF16), w_ref[...].astype(BF16), preferred_element_type=F32) + b_ref[...]


def _modulation(c_all, mod_w, mod_b):
    depth, d, n = mod_w.shape
    tn = 1536
    return pl.pallas_call(
        _mod_kernel,
        grid=(depth, n // tn),
        in_specs=[pl.BlockSpec((MOD_ROWS, d), lambda l, j: (0, 0)),
                  pl.BlockSpec((None, d, tn), lambda l, j: (l, 0, j)),
                  pl.BlockSpec((None, 1, tn), lambda l, j: (l, 0, j))],
        out_specs=pl.BlockSpec((None, MOD_ROWS, tn), lambda l, j: (l, 0, j)),
        out_shape=jax.ShapeDtypeStruct((depth, MOD_ROWS, n), F32),
        compiler_params=_cparams(2),
        name="modulation",
    )(c_all, mod_w, mod_b.reshape(depth, 1, n))


def _inproj_kernel(x_ref, m_ref, nw_ref, w_ref, o_ref, u_ref):
    @pl.when(pl.program_id(1) == 0)
    def _():
        y = _rms_rows(x_ref[...]) * nw_ref[...]
        u_ref[...] = (y * (1.0 + m_ref[1:2, :]) + m_ref[0:1, :]).astype(BF16)

    o_ref[...] = jnp.dot(u_ref[...], w_ref[...], preferred_element_type=F32).astype(o_ref.dtype)


def _inproj(x2d, mods, mod_row, norm_w, w, tm):
    n_tok, d = x2d.shape
    n = w.shape[1]
    tn = n // 3
    return pl.pallas_call(
        _inproj_kernel,
        grid=(n_tok // tm, n // tn),
        in_specs=[pl.BlockSpec((tm, d), lambda i, j: (i, 0)),
                  pl.BlockSpec((None, 6, d), lambda i, j: (mod_row(i), 0, 0)),
                  pl.BlockSpec((1, d), lambda i, j: (0, 0)),
                  pl.BlockSpec((d, tn), lambda i, j: (0, j))],
        out_specs=pl.BlockSpec((tm, tn), lambda i, j: (i, j)),
        out_shape=jax.ShapeDtypeStruct((n_tok, n), BF16),
        scratch_shapes=[pltpu.VMEM((tm, d), BF16)],
        compiler_params=_cparams(2),
        name="inproj",
    )(x2d, mods, norm_w.reshape(1, d), w)


RNN_CHUNK = 256


RNN_SEGMENTS = 32
RNN_SEG_VREGS = RNN_SEGMENTS // SUBLANES
TINY = 1e-37


def _seg_pitch(n_rows):
    p = -(-n_rows // RNN_SEGMENTS)
    while p % 8 != 4:
        p += 1
    return p


def _rglru_kernel(xc_ref, xl_ref, gc_ref, gl_ref, cw_ref, cb_ref, gw_ref, gb_ref, lam_ref,
                  yc_ref, yl_ref, xs, af, hf, ab, hb, of, ob, tot, cin, *, lc, ll, pitch):
    w = xs.shape[1]
    tc = RNN_CHUNK
    pad = SUBLANES
    s_len = lc + ll
    n_rows = RNN_SEGMENTS * pitch
    for ref in (af, hf, ab, hb):
        ref[s_len:n_rows, :] = jnp.zeros((n_rows - s_len, w), F32)
    zero_pad = jnp.zeros((pad, w), F32)
    c_base = pad
    l_base = lc + 3 * pad
    xs[0:pad, :] = zero_pad
    xs[c_base + lc:c_base + lc + 2 * pad, :] = jnp.zeros((2 * pad, w), F32)
    xs[l_base + ll:l_base + ll + pad, :] = zero_pad
    xs[c_base:c_base + lc, :] = xc_ref[...].astype(F32)
    xs[l_base:l_base + ll, :] = xl_ref[...].astype(F32)

    lam = lam_ref[...]
    sp = jnp.maximum(-lam, 0.0) + jnp.log(1.0 + jnp.exp(-jnp.abs(lam)))
    half_rate_f = sp[0:1, :] * (-0.5 * RG_C * LOG2E)
    half_rate_b = sp[1:2, :] * (-0.5 * RG_C * LOG2E)
    cw = cw_ref[...]
    cb = cb_ref[...]
    gw_half = (gw_ref[...].astype(F32) * 0.5).astype(BF16)
    gb_half = gb_ref[...] * 0.5

    def gates_chunk(xs_base, base_f, base_b, t0):
        x = cb + jnp.zeros((tc, w), F32)
        for k in range(RNN_CONV_W):
            x = x + cw[k:k + 1, :] * xs[pl.ds(xs_base + t0 + (k - RNN_PAD_L), tc), :]
        g = jnp.dot(x.astype(BF16), gw_half, preferred_element_type=F32) + gb_half
        x_half = 0.5 * x
        for direction, (a_ref, b_ref, half_rate, base) in enumerate(
                ((af, hf, half_rate_f, base_f), (ab, hb, half_rate_b, base_b))):
            t_r = jnp.tanh(g[:, (2 * direction) * w:(2 * direction + 1) * w])
            t_i = jnp.tanh(g[:, (2 * direction + 1) * w:(2 * direction + 2) * w])
            a = jnp.exp2(t_r * half_rate + half_rate)
            z = 1.0 - a * a
            rows = pl.ds(pl.multiple_of(base + t0, SUBLANES), tc)
            a_ref[rows, :] = a
            b_ref[rows, :] = (z * lax.rsqrt(jnp.maximum(z, TINY))) * (t_i * x_half + x_half)

    def ctx_body(c, carry):
        gates_chunk(c_base, 0, ll, c * tc)
        return carry

    def lat_body(c, carry):
        gates_chunk(l_base, lc, 0, c * tc)
        return carry

    lax.fori_loop(0, lc // tc, ctx_body, 0)
    lax.fori_loop(0, ll // tc, lat_body, 0, unroll=8)

    def seg_rows(m, k):
        return pl.ds(m * SUBLANES * pitch + k, SUBLANES, stride=pitch)

    nv = RNN_SEG_VREGS
    ones = jnp.ones((SUBLANES, w), F32)
    zeros = jnp.zeros((SUBLANES, w), F32)

    def totals_body(k, carry):
        pf, tf, pb, tb = carry
        kb = pitch - 1 - k
        npf, ntf, npb, ntb = [], [], [], []
        for m in range(nv):
            a = af[seg_rows(m, k), :]
            npf.append(a * pf[m])
            ntf.append(a * tf[m] + hf[seg_rows(m, k), :])
            a = ab[seg_rows(m, kb), :]
            npb.append(a * pb[m])
            ntb.append(a * tb[m] + hb[seg_rows(m, kb), :])
        return tuple(npf), tuple(ntf), tuple(npb), tuple(ntb)

    pf, tf, pb, tb = lax.fori_loop(0, pitch, totals_body,
                                   ((ones,) * nv, (zeros,) * nv, (ones,) * nv, (zeros,) * nv), unroll=4)
    for m in range(nv):
        rows = slice(m * SUBLANES, (m + 1) * SUBLANES)
        tot[0, rows, :] = pf[m]
        tot[1, rows, :] = tf[m]
        tot[2, rows, :] = pb[m]
        tot[3, rows, :] = tb[m]

    c = jnp.zeros((1, w), F32)
    for j in range(RNN_SEGMENTS):
        cin[0, j:j + 1, :] = c
        c = tot[0, j:j + 1, :] * c + tot[1, j:j + 1, :]
    c = jnp.zeros((1, w), F32)
    for j in reversed(range(RNN_SEGMENTS)):
        cin[1, j:j + 1, :] = c
        c = tot[2, j:j + 1, :] * c + tot[3, j:j + 1, :]

    def scan_body(k, carry):
        h_f, h_b = carry
        kb = pitch - 1 - k
        nf, nb_ = [], []
        for m in range(nv):
            h = af[seg_rows(m, k), :] * h_f[m] + hf[seg_rows(m, k), :]
            of[seg_rows(m, k), :] = h
            nf.append(h)
            h = ab[seg_rows(m, kb), :] * h_b[m] + hb[seg_rows(m, kb), :]
            ob[seg_rows(m, kb), :] = h
            nb_.append(h)
        return tuple(nf), tuple(nb_)

    h0_f = tuple(cin[0, m * SUBLANES:(m + 1) * SUBLANES, :] for m in range(nv))
    h0_b = tuple(cin[1, m * SUBLANES:(m + 1) * SUBLANES, :] for m in range(nv))
    lax.fori_loop(0, pitch, scan_body, (h0_f, h0_b), unroll=4)

    def out_ctx(c, carry):
        rows = pl.ds(pl.multiple_of(c * tc, SUBLANES), tc)
        brows = pl.ds(pl.multiple_of(ll + c * tc, SUBLANES), tc)
        yc_ref[rows, :] = (_gelu_tanh(gc_ref[rows, :].astype(F32)) * (of[rows, :] + ob[brows, :])).astype(yc_ref.dtype)
        return carry

    def out_lat(c, carry):
        rows = pl.ds(pl.multiple_of(c * tc, SUBLANES), tc)
        frows = pl.ds(pl.multiple_of(lc + c * tc, SUBLANES), tc)
        yl_ref[rows, :] = (_gelu_tanh(gl_ref[rows, :].astype(F32)) * (of[frows, :] + ob[rows, :])).astype(yl_ref.dtype)
        return carry

    lax.fori_loop(0, lc // tc, out_ctx, 0)
    lax.fori_loop(0, ll // tc, out_lat, 0, unroll=2)


def _rglru(p_ctx, p_lat, conv_w, conv_b, gate_w, gate_b, lam, n_batch, lc, ll):
    w = RNN_BLOCK_W
    nb = RNN_BLOCKS
    s = lc + ll
    rx0 = COL_RX // w
    rg0 = COL_RG // w
    pitch = _seg_pitch(s)
    n_rows = RNN_SEGMENTS * pitch
    kern = functools.partial(_rglru_kernel, lc=lc, ll=ll, pitch=pitch)
    return pl.pallas_call(
        kern,
        grid=(n_batch, nb),
        in_specs=[pl.BlockSpec((lc, w), lambda b, c: (b, rx0 + c)),
                  pl.BlockSpec((ll, w), lambda b, c: (b, rx0 + c)),
                  pl.BlockSpec((lc, w), lambda b, c: (b, rg0 + c)),
                  pl.BlockSpec((ll, w), lambda b, c: (b, rg0 + c)),
                  pl.BlockSpec((RNN_CONV_W, w), lambda b, c: (0, c)),
                  pl.BlockSpec((1, w), lambda b, c: (0, c)),
                  pl.BlockSpec((None, w, 4 * w), lambda b, c: (c, 0, 0)),
                  pl.BlockSpec((None, 1, 4 * w), lambda b, c: (c, 0, 0)),
                  pl.BlockSpec((2, w), lambda b, c: (0, c))],
        out_specs=[pl.BlockSpec((lc, w), lambda b, c: (b, c)),
                   pl.BlockSpec((ll, w), lambda b, c: (b, c))],
        out_shape=[jax.ShapeDtypeStruct((n_batch * lc, D_RNN), BF16),
                   jax.ShapeDtypeStruct((n_batch * ll, D_RNN), BF16)],
        scratch_shapes=[pltpu.VMEM((s + 4 * SUBLANES, w), F32),
                        pltpu.VMEM((n_rows, w), F32), pltpu.VMEM((n_rows, w), F32),
                        pltpu.VMEM((n_rows, w), F32), pltpu.VMEM((n_rows, w), F32),
                        pltpu.VMEM((n_rows, w), F32), pltpu.VMEM((n_rows, w), F32),
                        pltpu.VMEM((4, RNN_SEGMENTS, w), F32), pltpu.VMEM((2, RNN_SEGMENTS, w), F32)],
        compiler_params=_cparams(2),
        name="rglru",
    )(p_ctx, p_lat, p_ctx, p_lat, conv_w, conv_b, gate_w, gate_b, lam)


ATT_TQ = 512
ATT_KEY_CHUNK = 512
LOG2E = 1.4426950408889634
ATT_SAFE_SPREAD = 96.0


def _attn_lat_kernel(q_ref, qn_ref, kc_ref, vc_ref, kl_ref, vl_ref, cq_ref, sq_ref, cqn_ref, sqn_ref, ck_ref, sk_ref,
                     qw_ref, kw_ref, o_ref, k_s, v_s, q_cur, q_nxt, qnorm_cur, qnorm_nxt, p_s, sc_s, *, lc):
    hd = ATT_HEAD_DIM
    half = hd // 4
    tq = q_ref.shape[0]
    s_len = k_s.shape[0]
    i = pl.program_id(2)
    qw = qw_ref[...] * (hd ** -0.5 * LOG2E)

    def prep_q(src_ref, cos_ref, sin_ref, q_dst, qnorm_dst, sc_idx):
        cos = cos_ref[...]
        sin = sin_ref[...]
        qmax = None
        for g in range(ATT_GROUP):
            rows = slice(g * tq, (g + 1) * tq)
            q = _rms_rows(src_ref[:, g * hd:(g + 1) * hd].astype(F32)) * qw
            q = _rope(q, cos, sin, half).astype(BF16)
            q_dst[rows, :] = q
            q32 = q.astype(F32)
            qn = jnp.sqrt(jnp.sum(q32 * q32, axis=-1, keepdims=True))
            qnorm_dst[rows, :] = jnp.broadcast_to(qn, (tq, hd))
            qmax = jnp.max(qn) if qmax is None else jnp.maximum(qmax, jnp.max(qn))
        sc_s[sc_idx] = qmax

    @pl.when(i == 0)
    def _():
        prep_q(q_ref, cq_ref, sq_ref, q_cur, qnorm_cur, 1)
        kw = kw_ref[...]
        kc = (_rms_rows(kc_ref[...].astype(F32)) * kw).astype(BF16)
        kl = _rope(_rms_rows(kl_ref[...].astype(F32)) * kw, ck_ref[...], sk_ref[...], half).astype(BF16)
        k_s[0:lc, :] = kc
        k_s[lc:, :] = kl
        v_s[0:lc, 0:hd] = vc_ref[...]
        v_s[lc:, 0:hd] = vl_ref[...]
        v_s[:, hd:] = jnp.ones((s_len, hd), BF16)
        kc32 = kc.astype(F32)
        kl32 = kl.astype(F32)
        kn_c = jnp.max(jnp.sqrt(jnp.sum(kc32 * kc32, axis=-1, keepdims=True)))
        kn_l = jnp.max(jnp.sqrt(jnp.sum(kl32 * kl32, axis=-1, keepdims=True)))
        sc_s[0] = jnp.maximum(kn_c, kn_l)

    kmax = sc_s[0]
    safe = 2.0 * sc_s[1] * kmax < ATT_SAFE_SPREAD

    @pl.when(safe)
    def _():
        prep_q(qn_ref, cqn_ref, sqn_ref, q_nxt, qnorm_nxt, 2)
        q_all = q_cur[...]
        bound = qnorm_cur[...] * kmax
        for c0 in range(0, s_len, ATT_KEY_CHUNK):
            n = min(ATT_KEY_CHUNK, s_len - c0)
            s = lax.dot_general(q_all, k_s[c0:c0 + n, :], (((1,), (1,)), ((), ())), preferred_element_type=F32)
            p_s[:, c0:c0 + n] = jnp.exp2(s - jnp.concatenate([bound] * (n // hd), axis=1)).astype(BF16)
        acc = jnp.dot(p_s[...], v_s[...], preferred_element_type=F32)
        for g in range(ATT_GROUP):
            a = acc[g * tq:(g + 1) * tq, :]
            o_ref[:, g * hd:(g + 1) * hd] = (a[:, 0:hd] / a[:, hd:]).astype(o_ref.dtype)

    @pl.when(jnp.logical_not(safe))
    def _():
        prep_q(qn_ref, cqn_ref, sqn_ref, q_nxt, qnorm_nxt, 2)
        k_all = k_s[...]
        v_all = v_s[:, 0:hd]
        for g in range(ATT_GROUP):
            s = lax.dot_general(q_cur[g * tq:(g + 1) * tq, :], k_all, (((1,), (1,)), ((), ())),
                                preferred_element_type=F32)
            p = jnp.exp2(s - jnp.max(s, axis=-1, keepdims=True))
            denom = jnp.sum(p, axis=-1, keepdims=True)
            o = jnp.dot(p.astype(BF16), v_all, preferred_element_type=F32)
            o_ref[:, g * hd:(g + 1) * hd] = (o / denom).astype(o_ref.dtype)

    q_cur[...] = q_nxt[...]
    qnorm_cur[...] = qnorm_nxt[...]
    sc_s[1] = sc_s[2]


def _attn_ctx_kernel(q_ref, kc_ref, vc_ref, qw_ref, kw_ref, o_ref):
    hd = ATT_HEAD_DIM
    k = (_rms_rows(kc_ref[...].astype(F32)) * kw_ref[...]).astype(BF16)
    qw = qw_ref[...] * (hd ** -0.5)
    v_c = vc_ref[...]
    for g in range(ATT_GROUP):
        q = (_rms_rows(q_ref[:, g * hd:(g + 1) * hd].astype(F32)) * qw).astype(BF16)
        s = lax.dot_general(q, k, (((1,), (1,)), ((), ())), preferred_element_type=F32)
        m = jnp.max(s, axis=-1, keepdims=True)
        p = jnp.exp(s - m)
        denom = jnp.sum(p, axis=-1, keepdims=True)
        o = jnp.dot(p.astype(BF16), v_c, preferred_element_type=F32)
        o_ref[:, g * hd:(g + 1) * hd] = (o / denom).astype(o_ref.dtype)


def _attn_lat(p_ctx, p_lat, cos, sin, q_norm_w, k_norm_w, n_batch, lc, ll):
    hd = ATT_HEAD_DIM
    gw = ATT_GROUP * hd
    tq = ATT_TQ
    nq = ll // tq
    q0 = COL_AQ // gw
    k0 = COL_AK // hd
    v0 = COL_AV // hd
    kern = functools.partial(_attn_lat_kernel, lc=lc)

    def nxt(i):
        return jnp.minimum(i + 1, nq - 1)

    return pl.pallas_call(
        kern,
        grid=(n_batch, ATT_KV_HEADS, nq),
        in_specs=[pl.BlockSpec((tq, gw), lambda b, h, i: (b * nq + i, q0 + h)),
                  pl.BlockSpec((tq, gw), lambda b, h, i: (b * nq + nxt(i), q0 + h)),
                  pl.BlockSpec((lc, hd), lambda b, h, i: (b, k0 + h)),
                  pl.BlockSpec((lc, hd), lambda b, h, i: (b, v0 + h)),
                  pl.BlockSpec((ll, hd), lambda b, h, i: (b, k0 + h)),
                  pl.BlockSpec((ll, hd), lambda b, h, i: (b, v0 + h)),
                  pl.BlockSpec((tq, hd), lambda b, h, i: (i, 0)),
                  pl.BlockSpec((tq, hd), lambda b, h, i: (i, 0)),
                  pl.BlockSpec((tq, hd), lambda b, h, i: (nxt(i), 0)),
                  pl.BlockSpec((tq, hd), lambda b, h, i: (nxt(i), 0)),
                  pl.BlockSpec((ll, hd), lambda b, h, i: (0, 0), pipeline_mode=pl.Buffered(1)),
                  pl.BlockSpec((ll, hd), lambda b, h, i: (0, 0), pipeline_mode=pl.Buffered(1)),
                  pl.BlockSpec((1, hd), lambda b, h, i: (0, 0)),
                  pl.BlockSpec((1, hd), lambda b, h, i: (0, 0))],
        out_specs=pl.BlockSpec((tq, gw), lambda b, h, i: (b * nq + i, h)),
        out_shape=jax.ShapeDtypeStruct((n_batch * ll, ATT_HEADS * hd), BF16),
        scratch_shapes=[pltpu.VMEM((lc + ll, hd), BF16),
                        pltpu.VMEM((lc + ll, 2 * hd), BF16),
                        pltpu.VMEM((ATT_GROUP * tq, hd), BF16), pltpu.VMEM((ATT_GROUP * tq, hd), BF16),
                        pltpu.VMEM((ATT_GROUP * tq, hd), F32), pltpu.VMEM((ATT_GROUP * tq, hd), F32),
                        pltpu.VMEM((ATT_GROUP * tq, lc + ll), BF16),
                        pltpu.SMEM((3,), F32)],
        compiler_params=_cparams(3),
        name="attn_lat",
    )(p_lat, p_lat, p_ctx, p_ctx, p_lat, p_lat, cos, sin, cos, sin, cos, sin, q_norm_w, k_norm_w)


def _attn_ctx(p_ctx, q_norm_w, k_norm_w, n_batch, lc):
    hd = ATT_HEAD_DIM
    gw = ATT_GROUP * hd
    q0 = COL_AQ // gw
    k0 = COL_AK // hd
    v0 = COL_AV // hd
    return pl.pallas_call(
        _attn_ctx_kernel,
        grid=(n_batch, ATT_KV_HEADS),
        in_specs=[pl.BlockSpec((lc, gw), lambda b, h: (b, q0 + h)),
                  pl.BlockSpec((lc, hd), lambda b, h: (b, k0 + h)),
                  pl.BlockSpec((lc, hd), lambda b, h: (b, v0 + h)),
                  pl.BlockSpec((1, hd), lambda b, h: (0, 0)),
                  pl.BlockSpec((1, hd), lambda b, h: (0, 0))],
        out_specs=pl.BlockSpec((lc, gw), lambda b, h: (b, h)),
        out_shape=jax.ShapeDtypeStruct((n_batch * lc, ATT_HEADS * hd), BF16),
        compiler_params=_cparams(2),
        name="attn_ctx",
    )(p_ctx, p_ctx, p_ctx, q_norm_w, k_norm_w)


def _ret_kernel(qc_ref, kc_ref, vc_ref, gc_ref, ql_ref, kl_ref, vl_ref, gl_ref, rcos_ref, rsin_ref, ccos_ref,
                csin_ref, dec_ref, nw_ref, yc_ref, yl_ref,
                qb_s, kb_s, qf_s, qr_s, kf_s, sbs_s, sf_s, sb_s, o_a, o_b, *, lc, ll):
    c = RET_CHUNK
    dk = RET_QK_DIM
    dv = RET_V_DIM
    half = dk // 4
    n_c = lc // c
    n_l = ll // c
    rows_per_chunk = c // GRID_W
    dec = dec_ref[...]
    log_g = jnp.minimum(dec, 0.0) - jnp.log(1.0 + jnp.exp(-jnp.abs(dec)))
    lg_f = log_g[0:1, :]
    lg_b = log_g[1:2, :]
    pos = lax.broadcasted_iota(jnp.int32, (c, dk), 0).astype(F32)
    qd_f = jnp.exp((pos + 1.0) * lg_f)
    kd_f = jnp.exp((c - 1.0 - pos) * lg_f)
    cd_f = jnp.exp(float(c) * lg_f)
    qd_b = jnp.exp((c - pos) * lg_b)
    kd_b = jnp.exp(pos * lg_b)
    cd_b = jnp.exp(float(c) * lg_b)
    ii = lax.broadcasted_iota(jnp.int32, (c, c), 0)
    jj = lax.broadcasted_iota(jnp.int32, (c, c), 1)
    diff = (ii - jj).astype(F32)
    intra = jnp.where(ii >= jj, jnp.exp(jnp.maximum(diff, 0.0) * lg_f[:, 0:c]),
                      jnp.exp(jnp.maximum(-diff, 0.0) * lg_b[:, 0:c]))
    nw = nw_ref[...]
    k_scale = dk ** -0.5
    col_cos = jnp.concatenate([ccos_ref[...]] * rows_per_chunk, axis=0)
    col_sin = jnp.concatenate([csin_ref[...]] * rows_per_chunk, axis=0)

    def rope_tables(ci):
        def row_part(ref):
            return jnp.concatenate(
                [jnp.broadcast_to(ref[pl.ds(ci * rows_per_chunk + r, 1), :], (GRID_W, dk // 2))
                 for r in range(rows_per_chunk)], axis=0)
        return (jnp.concatenate([row_part(rcos_ref), col_cos], axis=1),
                jnp.concatenate([row_part(rsin_ref), col_sin], axis=1))

    def state_update(s_ref, k_dec, v, c_dec):
        kv = lax.dot_general(k_dec, v, (((0,), (0,)), ((), ())), preferred_element_type=F32)
        s_ref[...] = s_ref[...] * c_dec + kv

    def sweep1_chunk(q, k, v, g, orows):
        qb_s[orows, :] = q.astype(BF16)
        kb_s[orows, :] = k.astype(BF16)
        qf_s[orows, :] = (q * qd_f).astype(BF16)
        qr_s[orows, :] = (q * qd_b).astype(BF16)
        kf_s[orows, :] = (k * kd_f).astype(BF16)
        sbs_s[g] = sb_s[...].astype(BF16)
        state_update(sb_s, (k * kd_b).astype(BF16), v, cd_b)

    sf_s[...] = jnp.zeros((dk, dv), F32)
    sb_s[...] = jnp.zeros((dk, dv), F32)

    for ci in reversed(range(n_c)):
        rows = pl.ds(ci * c, c)
        sweep1_chunk(qc_ref[rows, :].astype(F32), kc_ref[rows, :].astype(F32) * k_scale, vc_ref[rows, :], ci, rows)

    def sweep1_body(i, carry):
        ci = n_l - 1 - i
        rows = pl.ds(pl.multiple_of(ci * c, c), c)
        orows = pl.ds(pl.multiple_of(lc + ci * c, c), c)
        cos, sin = rope_tables(ci)
        q = _rope(ql_ref[rows, :].astype(F32), cos, sin, half)
        k = _rope(kl_ref[rows, :].astype(F32), cos, sin, half) * k_scale
        sweep1_chunk(q, k, vl_ref[rows, :], n_c + ci, orows)
        return carry

    lax.fori_loop(0, n_l, sweep1_body, 0, unroll=4)

    def raw_out(g, orows, v):
        att = lax.dot_general(qb_s[orows, :], kb_s[orows, :], (((1,), (1,)), ((), ())),
                              preferred_element_type=F32) * intra
        o = (jnp.dot(att.astype(BF16), v, preferred_element_type=F32)
             + jnp.dot(qf_s[orows, :], sf_s[...].astype(BF16), preferred_element_type=F32)
             + jnp.dot(qr_s[orows, :], sbs_s[g], preferred_element_type=F32))
        state_update(sf_s, kf_s[orows, :], v, cd_f)
        return o

    def finish(o, gate):
        mu = jnp.mean(o, axis=-1, keepdims=True)
        oc = o - mu
        var = jnp.mean(oc * oc, axis=-1, keepdims=True)
        y = oc * lax.rsqrt(var + EPS) * nw
        return _silu(gate) * y

    for ci in range(n_c):
        rows = pl.ds(ci * c, c)
        yc_ref[rows, :] = finish(raw_out(ci, rows, vc_ref[rows, :]), gc_ref[rows, :].astype(F32)).astype(yc_ref.dtype)

    gsz = RET_GROUP
    n_groups = n_l // gsz

    def group_matmuls(grp, o_dst):
        for j in range(gsz):
            ci = grp * gsz + j
            rows = pl.ds(pl.multiple_of(ci * c, c), c)
            orows = pl.ds(pl.multiple_of(lc + ci * c, c), c)
            o_dst[j * c:(j + 1) * c, :] = raw_out(n_c + ci, orows, vl_ref[rows, :])

    def group_finish(grp, o_src):
        for j in range(gsz):
            ci = grp * gsz + j
            rows = pl.ds(pl.multiple_of(ci * c, c), c)
            yl_ref[rows, :] = finish(o_src[j * c:(j + 1) * c, :], gl_ref[rows, :].astype(F32)).astype(yl_ref.dtype)

    group_matmuls(0, o_a)

    def pair_body(p, carry):
        group_matmuls(2 * p + 1, o_b)
        group_finish(2 * p, o_a)
        group_matmuls(2 * p + 2, o_a)
        group_finish(2 * p + 1, o_b)
        return carry

    lax.fori_loop(0, n_groups // 2 - 1, pair_body, 0)
    group_matmuls(n_groups - 1, o_b)
    group_finish(n_groups - 2, o_a)
    group_finish(n_groups - 1, o_b)


def _retention(p_ctx, p_lat, cos, sin, dec, norm_w, n_batch, lc, ll):
    dk = RET_QK_DIM
    q0 = COL_RQ // dk
    k0 = COL_RK // dk
    v0 = COL_RV // dk
    g0 = COL_RGT // dk
    n_chunks = (lc + ll) // RET_CHUNK
    assert RET_CHUNK % GRID_W == 0 and ll % (2 * RET_GROUP * RET_CHUNK) == 0
    row_cos, row_sin = cos[::GRID_W, :dk // 2], sin[::GRID_W, :dk // 2]
    col_cos, col_sin = cos[:GRID_W, dk // 2:], sin[:GRID_W, dk // 2:]
    kern = functools.partial(_ret_kernel, lc=lc, ll=ll)

    def col(off):
        return lambda b, h: (b, off + h)

    const = lambda b, h: (0, 0)
    return pl.pallas_call(
        kern,
        grid=(n_batch, RET_HEADS),
        in_specs=[pl.BlockSpec((lc, dk), col(q0)), pl.BlockSpec((lc, dk), col(k0)),
                  pl.BlockSpec((lc, dk), col(v0)), pl.BlockSpec((lc, dk), col(g0)),
                  pl.BlockSpec((ll, dk), col(q0)), pl.BlockSpec((ll, dk), col(k0)),
                  pl.BlockSpec((ll, dk), col(v0)), pl.BlockSpec((ll, dk), col(g0)),
                  pl.BlockSpec(row_cos.shape, const), pl.BlockSpec(row_sin.shape, const),
                  pl.BlockSpec(col_cos.shape, const), pl.BlockSpec(col_sin.shape, const),
                  pl.BlockSpec((None, 2, dk), lambda b, h: (h, 0, 0)),
                  pl.BlockSpec((1, dk), lambda b, h: (0, h))],
        out_specs=[pl.BlockSpec((lc, dk), lambda b, h: (b, h)),
                   pl.BlockSpec((ll, dk), lambda b, h: (b, h))],
        out_shape=[jax.ShapeDtypeStruct((n_batch * lc, RET_HEADS * RET_V_DIM), BF16),
                   jax.ShapeDtypeStruct((n_batch * ll, RET_HEADS * RET_V_DIM), BF16)],
        scratch_shapes=[pltpu.VMEM((lc + ll, dk), BF16)] * 5
        + [pltpu.VMEM((n_chunks, dk, RET_V_DIM), BF16)]
        + [pltpu.VMEM((dk, RET_V_DIM), F32)] * 2
        + [pltpu.VMEM((RET_GROUP * RET_CHUNK, RET_V_DIM), F32)] * 2,
        compiler_params=_cparams(2),
        name="retention",
    )(p_ctx, p_ctx, p_ctx, p_ctx, p_lat, p_lat, p_lat, p_lat, row_cos, row_sin, col_cos, col_sin, dec, norm_w)


def _merge_kernel(x_ref, yr_ref, ya_ref, yt_ref, g0_ref, g1_ref, g2_ref, wb_ref, wo_ref, m_ref, nw_ref,
                  xo_ref, vo_ref):
    m = (_sigmoid(g0_ref[...].astype(F32)) * jnp.dot(yr_ref[...], wb_ref[0], preferred_element_type=F32)
         + _sigmoid(g1_ref[...].astype(F32)) * jnp.dot(ya_ref[...], wb_ref[1], preferred_element_type=F32)
         + _sigmoid(g2_ref[...].astype(F32)) * jnp.dot(yt_ref[...], wb_ref[2], preferred_element_type=F32))
    out = jnp.dot(m.astype(BF16), wo_ref[...], preferred_element_type=F32)
    x1 = x_ref[...] + m_ref[2:3, :] * out
    xo_ref[...] = x1
    v = _rms_rows(x1) * nw_ref[...]
    vo_ref[...] = (v * (1.0 + m_ref[4:5, :]) + m_ref[3:4, :]).astype(vo_ref.dtype)


def _merge(x2d, y_rnn, y_att, y_ret, p, mods, mod_row, norm_w, w_branch, w_out, tm):
    n_tok, d = x2d.shape
    bg0 = COL_BG // d
    tok = lambda i: (i, 0)
    const2 = lambda i: (0, 0)
    return pl.pallas_call(
        _merge_kernel,
        grid=(n_tok // tm,),
        in_specs=[pl.BlockSpec((tm, d), tok), pl.BlockSpec((tm, d), tok), pl.BlockSpec((tm, d), tok),
                  pl.BlockSpec((tm, d), tok),
                  pl.BlockSpec((tm, d), lambda i: (i, bg0)),
                  pl.BlockSpec((tm, d), lambda i: (i, bg0 + 1)),
                  pl.BlockSpec((tm, d), lambda i: (i, bg0 + 2)),
                  pl.BlockSpec((N_BRANCH, d, d), lambda i: (0, 0, 0)),
                  pl.BlockSpec((d, d), const2),
                  pl.BlockSpec((None, 6, d), lambda i: (mod_row(i), 0, 0)),
                  pl.BlockSpec((1, d), const2)],
        out_specs=[pl.BlockSpec((tm, d), tok), pl.BlockSpec((tm, d), tok)],
        out_shape=[jax.ShapeDtypeStruct((n_tok, d), F32), jax.ShapeDtypeStruct((n_tok, d), BF16)],
        compiler_params=_cparams(1),
        name="merge",
    )(x2d, y_rnn, y_att, y_ret, p, p, p, w_branch, w_out, mods, norm_w.reshape(1, d))


FFN_CHUNK = 256
FFN_HALO = 16


def _ffn_kernel(v_ref, vp_ref, vn_ref, wu_ref, cw_ref, cb_ref, wd_ref, x_ref, m_ref, fw_ref, o_ref,
                vext_s, a_s, h_s, *, tiles_per_seq, final):
    tm = v_ref.shape[0]
    f = h_s.shape[1]
    fc = FFN_CHUNK
    n_slab = fc // LANES
    ti = pl.program_id(0) % tiles_per_seq
    vext_s[0:FFN_HALO, :] = vp_ref[...]
    vext_s[FFN_HALO:FFN_HALO + tm, :] = v_ref[...]
    vext_s[FFN_HALO + tm:, :] = vn_ref[...]
    v_ext = vext_s[...]
    v_main = v_ref[...]
    first = ti == 0
    last = ti == tiles_per_seq - 1
    row0 = jnp.minimum(pl.program_id(0), 0)
    cw = cw_ref[...] * 0.5
    cb = cb_ref[...] * 0.5
    for c in range(f // fc):
        cols = slice(c * fc, (c + 1) * fc)
        a_ext = jnp.dot(v_ext, wu_ref[:, cols], preferred_element_type=F32)
        b = jnp.dot(v_main, wu_ref[:, f + c * fc:f + (c + 1) * fc], preferred_element_type=F32)
        taps = [[], [], []]
        for j in range(n_slab):
            slab = (c % 2) * n_slab + j
            lanes = slice(j * LANES, (j + 1) * LANES)
            a_s[slab] = a_ext[:, lanes]
            a_s[slab, FFN_HALO - 1:FFN_HALO, :] = jnp.where(first, 0.0, a_ext[FFN_HALO - 1:FFN_HALO, lanes])
            a_s[slab, FFN_HALO + tm:FFN_HALO + tm + 1, :] = jnp.where(
                last, 0.0, a_ext[FFN_HALO + tm:FFN_HALO + tm + 1, lanes])
            for k in range(FFN_CONV_W):
                taps[k].append(a_s[slab, pl.ds(row0 + (FFN_HALO - 1 + k), tm), :])
        a_m1, a_0, a_p1 = (jnp.concatenate(t, axis=1) for t in taps)
        half_conv = cw[0:1, cols] * a_m1 + cw[1:2, cols] * a_0 + cw[2:3, cols] * a_p1 + cb[:, cols]
        h_s[:, cols] = ((half_conv * (jnp.tanh(half_conv) + 1.0)) * b).astype(BF16)
    out = jnp.dot(h_s[...], wd_ref[...], preferred_element_type=F32)
    x2 = x_ref[...] + m_ref[5:6, :] * out
    if final:
        x2 = _rms_rows(x2) * fw_ref[...]
    o_ref[...] = x2


def _ffn(v2d, w_up, conv_w, conv_b, w_down, x2d, mods, mod_row, final_w, tm, seq_len, final):
    n_tok, d = x2d.shape
    f = D_FF
    tiles_per_seq = seq_len // tm
    hb = tm // FFN_HALO
    n_hblk = n_tok // FFN_HALO
    assert f % FFN_CHUNK == 0 and tm % FFN_HALO == 0
    kern = functools.partial(_ffn_kernel, tiles_per_seq=tiles_per_seq, final=final)
    resident = dict(pipeline_mode=pl.Buffered(1))
    return pl.pallas_call(
        kern,
        grid=(n_tok // tm,),
        in_specs=[pl.BlockSpec((tm, d), lambda i: (i, 0)),
                  pl.BlockSpec((FFN_HALO, d), lambda i: (jnp.maximum(i * hb - 1, 0), 0)),
                  pl.BlockSpec((FFN_HALO, d), lambda i: (jnp.minimum((i + 1) * hb, n_hblk - 1), 0)),
                  pl.BlockSpec((d, 2 * f), lambda i: (0, 0), **resident),
                  pl.BlockSpec((FFN_CONV_W, f), lambda i: (0, 0)),
                  pl.BlockSpec((1, f), lambda i: (0, 0)),
                  pl.BlockSpec((f, d), lambda i: (0, 0), **resident),
                  pl.BlockSpec((tm, d), lambda i: (i, 0)),
                  pl.BlockSpec((None, 6, d), lambda i: (mod_row(i), 0, 0)),
                  pl.BlockSpec((1, d), lambda i: (0, 0))],
        out_specs=pl.BlockSpec((tm, d), lambda i: (i, 0)),
        out_shape=jax.ShapeDtypeStruct((n_tok, d), F32),
        scratch_shapes=[pltpu.VMEM((tm + 2 * FFN_HALO, d), BF16),
                        pltpu.VMEM((2 * FFN_CHUNK // LANES, tm + 2 * FFN_HALO, LANES), F32),
                        pltpu.VMEM((tm, f), BF16)],
        compiler_params=_cparams(1),
        name="ffn",
    )(v2d, v2d, v2d, w_up, conv_w, conv_b.reshape(1, f), w_down, x2d, mods, final_w.reshape(1, d))


def _rope_tables(seq_len, head_dim):
    n_freq = head_dim // 4
    t = np.arange(seq_len)
    row = (t // GRID_W).astype(np.float32)
    col = (t % GRID_W).astype(np.float32)
    inv = (ROPE_THETA ** (-np.arange(n_freq, dtype=np.float32) / n_freq)).astype(np.float32)
    ang = np.stack([row[:, None] * inv, col[:, None] * inv], axis=1).astype(np.float64)
    cos = np.cos(ang)
    sin = np.sin(ang)
    cos_t = np.stack([cos, cos], axis=2).reshape(seq_len, head_dim).astype(np.float32)
    sin_t = np.stack([-sin, sin], axis=2).reshape(seq_len, head_dim).astype(np.float32)
    return jnp.asarray(cos_t), jnp.asarray(sin_t)


def kernel(x, c, ctx, c_ctx, mod_w, mod_b, norm1_w, norm2_w, w_in, rnn_conv_w, rnn_conv_b, rglru_w, rglru_b,
           rglru_lam, q_norm_w, k_norm_w, ret_decay, ret_norm_w, w_branch, w_out, ffn_up, ffn_conv_w,
           ffn_conv_b, ffn_down, final_norm_w):
    n_batch, ll, d = x.shape
    lc = ctx.shape[1]
    depth = mod_w.shape[0]
    assert d == D_MODEL and n_batch + 1 <= MOD_ROWS
    assert ll % 1024 == 0 and lc % RNN_CHUNK == 0 and lc % RET_CHUNK == 0

    cos_a, sin_a = _rope_tables(ll, ATT_HEAD_DIM)
    cos_r, sin_r = _rope_tables(ll, RET_QK_DIM)

    c_all = jnp.zeros((MOD_ROWS, d), F32).at[:n_batch].set(c).at[n_batch].set(c_ctx)
    mods = _modulation(c_all, mod_w, mod_b).reshape(depth, MOD_ROWS, 6, d)

    tm_lat = 1024
    tm_ctx = lc
    tm_ctx_big = next(t for t in (1024, 512, 256, lc) if (n_batch * lc) % t == 0)
    tm_ctx_mix = min(tm_ctx_big, 512)
    lat_tiles = ll // tm_lat
    lat_row_big = lambda i: i // lat_tiles
    ctx_row = lambda i: n_batch
    tm_mix = 512
    lat_row_mix = lambda i: i // (ll // tm_mix)

    x2 = x.reshape(n_batch * ll, d)
    cx2 = ctx.reshape(n_batch * lc, d)

    for l in range(depth):
        last = l == depth - 1
        w_in_p = jnp.concatenate([w_in[l][:, s:s + n] for s, n in _REF_SEGMENTS], axis=1).astype(BF16)
        gate_w = jnp.transpose(rglru_w[l], (2, 3, 0, 1, 4)).reshape(RNN_BLOCKS, RNN_BLOCK_W, 4 * RNN_BLOCK_W)
        gate_w = gate_w.astype(BF16)
        gate_b = jnp.transpose(rglru_b[l].reshape(2, 2, RNN_BLOCKS, RNN_BLOCK_W), (2, 0, 1, 3))
        gate_b = gate_b.reshape(RNN_BLOCKS, 1, 4 * RNN_BLOCK_W)
        dec = jnp.broadcast_to(jnp.transpose(ret_decay[l])[:, :, None], (RET_HEADS, 2, RET_QK_DIM))
        wb = w_branch[l].astype(BF16)
        wo = w_out[l].astype(BF16)
        w_up = ffn_up[l].astype(BF16)
        w_dn = ffn_down[l].astype(BF16)
        ml = mods[l]

        p_lat = _inproj(x2, ml, lat_row_big, norm1_w[l], w_in_p, tm_lat)
        p_ctx = _inproj(cx2, ml, ctx_row, norm1_w[l], w_in_p, tm_ctx_big)

        yr_c, yr_l = _rglru(p_ctx, p_lat, rnn_conv_w[l], rnn_conv_b[l].reshape(1, D_RNN), gate_w, gate_b,
                            rglru_lam[l], n_batch, lc, ll)
        ya_l = _attn_lat(p_ctx, p_lat, cos_a, sin_a, q_norm_w[l].reshape(1, -1), k_norm_w[l].reshape(1, -1),
                         n_batch, lc, ll)
        yt_c, yt_l = _retention(p_ctx, p_lat, cos_r, sin_r, dec, ret_norm_w[l].reshape(1, -1), n_batch, lc, ll)

        x2, v_lat = _merge(x2, yr_l, ya_l, yt_l, p_lat, ml, lat_row_mix, norm2_w[l], wb, wo, tm_mix)
        x2 = _ffn(v_lat, w_up, ffn_conv_w[l], ffn_conv_b[l], w_dn, x2, ml, lat_row_mix, final_norm_w,
                  tm_mix, ll, final=last)

        if not last:
            ya_c = _attn_ctx(p_ctx, q_norm_w[l].reshape(1, -1), k_norm_w[l].reshape(1, -1), n_batch, lc)
            cx2, v_ctx = _merge(cx2, yr_c, ya_c, yt_c, p_ctx, ml, ctx_row, norm2_w[l], wb, wo, tm_ctx_mix)
            cx2 = _ffn(v_ctx, w_up, ffn_conv_w[l], ffn_conv_b[l], w_dn, cx2, ml, ctx_row, final_norm_w,
                       tm_ctx, lc, final=False)

    return x2.reshape(n_batch, ll, d)
```

```python
import functools

import numpy as np
import jax
import jax.numpy as jnp
from jax import lax
from jax.experimental import pallas as pl
from jax.experimental.pallas import tpu as pltpu

F32 = jnp.float32
BF16 = jnp.bfloat16

D_MODEL = 1024
GRID_W = 64
D_RNN = 1024
RNN_BLOCKS = 8
RNN_BLOCK_W = D_RNN // RNN_BLOCKS
RNN_CONV_W = 4
RNN_PAD_L = 2
RG_C = 8.0
ATT_HEADS = 8
ATT_KV_HEADS = 2
ATT_HEAD_DIM = 128
ATT_GROUP = ATT_HEADS // ATT_KV_HEADS
ROPE_THETA = 10000.0
RET_HEADS = 4
RET_QK_DIM = 256
RET_V_DIM = 256
RET_CHUNK = 128
RET_GROUP = 8
N_BRANCH = 3
D_FF = 2816
FFN_CONV_W = 3
EPS = 1e-6

COL_RX = 0
COL_RG = 1024
COL_AQ = 2048
COL_RQ = 3072
COL_RK = 4096
COL_RV = 5120
COL_RGT = 6144
COL_BG = 7168
COL_AK = 10240
COL_AV = 10496
IN_COLS = 10752
_REF_SEGMENTS = ((0, 1024), (1024, 1024), (2048, 1024), (3584, 1024), (4608, 1024), (5632, 1024),
                 (6656, 1024), (7680, 3072), (3072, 256), (3328, 256))

SUBLANES = 8
LANES = 128
MOD_ROWS = 16
VMEM_LIMIT = 52 * 1024 * 1024


def _cparams(n_grid):
    return pltpu.CompilerParams(dimension_semantics=("arbitrary",) * n_grid, vmem_limit_bytes=VMEM_LIMIT)


def _sigmoid(x):
    return 0.5 * jnp.tanh(0.5 * x) + 0.5


def _silu(x):
    return x * _sigmoid(x)


def _gelu_tanh(x):
    c = float(np.sqrt(2.0 / np.pi))
    half = 0.5 * x
    return half * jnp.tanh(x * (c + (c * 0.044715) * (x * x))) + half


def _rms_rows(x):
    return x * lax.rsqrt(jnp.mean(x * x, axis=-1, keepdims=True) + EPS)


def _rope(x, cos, sin_signed, half):
    n = x.shape[-1]
    from_lo = pltpu.roll(x, half, axis=1)
    from_hi = pltpu.roll(x, n - half, axis=1)
    lane = lax.broadcasted_iota(jnp.int32, x.shape, 1)
    partner = jnp.where((lane & half) != 0, from_lo, from_hi)
    return x * cos + partner * sin_signed


def _mod_kernel(c_ref, w_ref, b_ref, o_ref):
    sc = _silu(c_ref[...])
    o_ref[...] = jnp.dot(sc.astype(BF16), w_ref[...].astype(BF16), preferred_element_type=F32) + b_ref[...]


def _modulation(c_all, mod_w, mod_b):
    depth, d, n = mod_w.shape
    tn = 1536
    return pl.pallas_call(
        _mod_kernel,
        grid=(depth, n // tn),
        in_specs=[pl.BlockSpec((MOD_ROWS, d), lambda l, j: (0, 0)),
                  pl.BlockSpec((None, d, tn), lambda l, j: (l, 0, j)),
                  pl.BlockSpec((None, 1, tn), lambda l, j: (l, 0, j))],
        out_specs=pl.BlockSpec((None, MOD_ROWS, tn), lambda l, j: (l, 0, j)),
        out_shape=jax.ShapeDtypeStruct((depth, MOD_ROWS, n), F32),
        compiler_params=_cparams(2),
        name="modulation",
    )(c_all, mod_w, mod_b.reshape(depth, 1, n))


def _inproj_kernel(x_ref, m_ref, nw_ref, w_ref, o_ref, u_ref):
    @pl.when(pl.program_id(1) == 0)
    def _():
        y = _rms_rows(x_ref[...]) * nw_ref[...]
        u_ref[...] = (y * (1.0 + m_ref[1:2, :]) + m_ref[0:1, :]).astype(BF16)

    o_ref[...] = jnp.dot(u_ref[...], w_ref[...], preferred_element_type=F32).astype(o_ref.dtype)


def _inproj(x2d, mods, mod_row, norm_w, w, tm):
    n_tok, d = x2d.shape
    n = w.shape[1]
    tn = n // 3
    return pl.pallas_call(
        _inproj_kernel,
        grid=(n_tok // tm, n // tn),
        in_specs=[pl.BlockSpec((tm, d), lambda i, j: (i, 0)),
                  pl.BlockSpec((None, 6, d), lambda i, j: (mod_row(i), 0, 0)),
                  pl.BlockSpec((1, d), lambda i, j: (0, 0)),
                  pl.BlockSpec((d, tn), lambda i, j: (0, j))],
        out_specs=pl.BlockSpec((tm, tn), lambda i, j: (i, j)),
        out_shape=jax.ShapeDtypeStruct((n_tok, n), BF16),
        scratch_shapes=[pltpu.VMEM((tm, d), BF16)],
        compiler_params=_cparams(2),
        name="inproj",
    )(x2d, mods, norm_w.reshape(1, d), w)


RNN_CHUNK = 256


RNN_SEGMENTS = 32
RNN_SEG_VREGS = RNN_SEGMENTS // SUBLANES
TINY = 1e-37


def _seg_pitch(n_rows):
    p = -(-n_rows // RNN_SEGMENTS)
    while p % 8 != 4:
        p += 1
    return p


def _rglru_kernel(xc_ref, xl_ref, gc_ref, gl_ref, cw_ref, cb_ref, gw_ref, gb_ref, lam_ref,
                  yc_ref, yl_ref, xs, af, hf, ab, hb, of, ob, tot, cin, *, lc, ll, pitch):
    w = xs.shape[1]
    tc = RNN_CHUNK
    pad = SUBLANES
    s_len = lc + ll
    n_rows = RNN_SEGMENTS * pitch
    for ref in (af, hf, ab, hb):
        ref[s_len:n_rows, :] = jnp.zeros((n_rows - s_len, w), F32)
    zero_pad = jnp.zeros((pad, w), F32)
    c_base = pad
    l_base = lc + 3 * pad
    xs[0:pad, :] = zero_pad
    xs[c_base + lc:c_base + lc + 2 * pad, :] = jnp.zeros((2 * pad, w), F32)
    xs[l_base + ll:l_base + ll + pad, :] = zero_pad
    xs[c_base:c_base + lc, :] = xc_ref[...].astype(F32)
    xs[l_base:l_base + ll, :] = xl_ref[...].astype(F32)

    lam = lam_ref[...]
    sp = jnp.maximum(-lam, 0.0) + jnp.log(1.0 + jnp.exp(-jnp.abs(lam)))
    half_rate_f = sp[0:1, :] * (-0.5 * RG_C * LOG2E)
    half_rate_b = sp[1:2, :] * (-0.5 * RG_C * LOG2E)
    cw = cw_ref[...]
    cb = cb_ref[...]
    gw_half = (gw_ref[...].astype(F32) * 0.5).astype(BF16)
    gb_half = gb_ref[...] * 0.5

    def gates_chunk(xs_base, base_f, base_b, t0):
        x = cb + jnp.zeros((tc, w), F32)
        for k in range(RNN_CONV_W):
            x = x + cw[k:k + 1, :] * xs[pl.ds(xs_base + t0 + (k - RNN_PAD_L), tc), :]
        g = jnp.dot(x.astype(BF16), gw_half, preferred_element_type=F32) + gb_half
        x_half = 0.5 * x
        for direction, (a_ref, b_ref, half_rate, base) in enumerate(
                ((af, hf, half_rate_f, base_f), (ab, hb, half_rate_b, base_b))):
            t_r = jnp.tanh(g[:, (2 * direction) * w:(2 * direction + 1) * w])
            t_i = jnp.tanh(g[:, (2 * direction + 1) * w:(2 * direction + 2) * w])
            a = jnp.exp2(t_r * half_rate + half_rate)
            z = 1.0 - a * a
            rows = pl.ds(pl.multiple_of(base + t0, SUBLANES), tc)
            a_ref[rows, :] = a
            b_ref[rows, :] = (z * lax.rsqrt(jnp.maximum(z, TINY))) * (t_i * x_half + x_half)

    def ctx_body(c, carry):
        gates_chunk(c_base, 0, ll, c * tc)
        return carry

    def lat_body(c, carry):
        gates_chunk(l_base, lc, 0, c * tc)
        return carry

    lax.fori_loop(0, lc // tc, ctx_body, 0)
    lax.fori_loop(0, ll // tc, lat_body, 0, unroll=16)

    def seg_rows(m, k):
        return pl.ds(m * SUBLANES * pitch + k, SUBLANES, stride=pitch)

    nv = RNN_SEG_VREGS
    ones = jnp.ones((SUBLANES, w), F32)
    zeros = jnp.zeros((SUBLANES, w), F32)

    def totals_body(k, carry):
        pf, tf, pb, tb = carry
        kb = pitch - 1 - k
        npf, ntf, npb, ntb = [], [], [], []
        for m in range(nv):
            a = af[seg_rows(m, k), :]
            npf.append(a * pf[m])
            ntf.append(a * tf[m] + hf[seg_rows(m, k), :])
            a = ab[seg_rows(m, kb), :]
            npb.append(a * pb[m])
            ntb.append(a * tb[m] + hb[seg_rows(m, kb), :])
        return tuple(npf), tuple(ntf), tuple(npb), tuple(ntb)

    pf, tf, pb, tb = lax.fori_loop(0, pitch, totals_body,
                                   ((ones,) * nv, (zeros,) * nv, (ones,) * nv, (zeros,) * nv), unroll=4)
    for m in range(nv):
        rows = slice(m * SUBLANES, (m + 1) * SUBLANES)
        tot[0, rows, :] = pf[m]
        tot[1, rows, :] = tf[m]
        tot[2, rows, :] = pb[m]
        tot[3, rows, :] = tb[m]

    c = jnp.zeros((1, w), F32)
    for j in range(RNN_SEGMENTS):
        cin[0, j:j + 1, :] = c
        c = tot[0, j:j + 1, :] * c + tot[1, j:j + 1, :]
    c = jnp.zeros((1, w), F32)
    for j in reversed(range(RNN_SEGMENTS)):
        cin[1, j:j + 1, :] = c
        c = tot[2, j:j + 1, :] * c + tot[3, j:j + 1, :]

    def scan_body(k, carry):
        h_f, h_b = carry
        kb = pitch - 1 - k
        nf, nb_ = [], []
        for m in range(nv):
            h = af[seg_rows(m, k), :] * h_f[m] + hf[seg_rows(m, k), :]
            of[seg_rows(m, k), :] = h
            nf.append(h)
            h = ab[seg_rows(m, kb), :] * h_b[m] + hb[seg_rows(m, kb), :]
            ob[seg_rows(m, kb), :] = h
            nb_.append(h)
        return tuple(nf), tuple(nb_)

    h0_f = tuple(cin[0, m * SUBLANES:(m + 1) * SUBLANES, :] for m in range(nv))
    h0_b = tuple(cin[1, m * SUBLANES:(m + 1) * SUBLANES, :] for m in range(nv))
    lax.fori_loop(0, pitch, scan_body, (h0_f, h0_b), unroll=4)

    def out_ctx(c, carry):
        rows = pl.ds(pl.multiple_of(c * tc, SUBLANES), tc)
        brows = pl.ds(pl.multiple_of(ll + c * tc, SUBLANES), tc)
        yc_ref[rows, :] = (_gelu_tanh(gc_ref[rows, :].astype(F32)) * (of[rows, :] + ob[brows, :])).astype(yc_ref.dtype)
        return carry

    def out_lat(c, carry):
        rows = pl.ds(pl.multiple_of(c * tc, SUBLANES), tc)
        frows = pl.ds(pl.multiple_of(lc + c * tc, SUBLANES), tc)
        yl_ref[rows, :] = (_gelu_tanh(gl_ref[rows, :].astype(F32)) * (of[frows, :] + ob[rows, :])).astype(yl_ref.dtype)
        return carry

    lax.fori_loop(0, lc // tc, out_ctx, 0)
    lax.fori_loop(0, ll // tc, out_lat, 0, unroll=2)


def _rglru(p_ctx, p_lat, conv_w, conv_b, gate_w, gate_b, lam, n_batch, lc, ll):
    w = RNN_BLOCK_W
    nb = RNN_BLOCKS
    s = lc + ll
    rx0 = COL_RX // w
    rg0 = COL_RG // w
    pitch = _seg_pitch(s)
    n_rows = RNN_SEGMENTS * pitch
    kern = functools.partial(_rglru_kernel, lc=lc, ll=ll, pitch=pitch)
    return pl.pallas_call(
        kern,
        grid=(n_batch, nb),
        in_specs=[pl.BlockSpec((lc, w), lambda b, c: (b, rx0 + c)),
                  pl.BlockSpec((ll, w), lambda b, c: (b, rx0 + c)),
                  pl.BlockSpec((lc, w), lambda b, c: (b, rg0 + c)),
                  pl.BlockSpec((ll, w), lambda b, c: (b, rg0 + c)),
                  pl.BlockSpec((RNN_CONV_W, w), lambda b, c: (0, c)),
                  pl.BlockSpec((1, w), lambda b, c: (0, c)),
                  pl.BlockSpec((None, w, 4 * w), lambda b, c: (c, 0, 0)),
                  pl.BlockSpec((None, 1, 4 * w), lambda b, c: (c, 0, 0)),
                  pl.BlockSpec((2, w), lambda b, c: (0, c))],
        out_specs=[pl.BlockSpec((lc, w), lambda b, c: (b, c)),
                   pl.BlockSpec((ll, w), lambda b, c: (b, c))],
        out_shape=[jax.ShapeDtypeStruct((n_batch * lc, D_RNN), BF16),
                   jax.ShapeDtypeStruct((n_batch * ll, D_RNN), BF16)],
        scratch_shapes=[pltpu.VMEM((s + 4 * SUBLANES, w), F32),
                        pltpu.VMEM((n_rows, w), F32), pltpu.VMEM((n_rows, w), F32),
                        pltpu.VMEM((n_rows, w), F32), pltpu.VMEM((n_rows, w), F32),
                        pltpu.VMEM((n_rows, w), F32), pltpu.VMEM((n_rows, w), F32),
                        pltpu.VMEM((4, RNN_SEGMENTS, w), F32), pltpu.VMEM((2, RNN_SEGMENTS, w), F32)],
        compiler_params=_cparams(2),
        name="rglru",
    )(p_ctx, p_lat, p_ctx, p_lat, conv_w, conv_b, gate_w, gate_b, lam)


ATT_TQ = 512
ATT_KEY_CHUNK = 512
LOG2E = 1.4426950408889634
ATT_SAFE_SPREAD = 96.0


def _attn_lat_kernel(q_ref, qn_ref, kc_ref, vc_ref, kl_ref, vl_ref, cq_ref, sq_ref, cqn_ref, sqn_ref, ck_ref, sk_ref,
                     qw_ref, kw_ref, o_ref, k_s, v_s, q_cur, q_nxt, qnorm_cur, qnorm_nxt, p_s, sc_s, *, lc):
    hd = ATT_HEAD_DIM
    half = hd // 4
    tq = q_ref.shape[0]
    s_len = k_s.shape[0]
    i = pl.program_id(2)
    qw = qw_ref[...] * (hd ** -0.5 * LOG2E)

    def prep_q(src_ref, cos_ref, sin_ref, q_dst, qnorm_dst, sc_idx):
        cos = cos_ref[...]
        sin = sin_ref[...]
        qmax = None
        for g in range(ATT_GROUP):
            rows = slice(g * tq, (g + 1) * tq)
            q = _rms_rows(src_ref[:, g * hd:(g + 1) * hd].astype(F32)) * qw
            q = _rope(q, cos, sin, half).astype(BF16)
            q_dst[rows, :] = q
            q32 = q.astype(F32)
            qn = jnp.sqrt(jnp.sum(q32 * q32, axis=-1, keepdims=True))
            qnorm_dst[rows, :] = jnp.broadcast_to(qn, (tq, hd))
            qmax = jnp.max(qn) if qmax is None else jnp.maximum(qmax, jnp.max(qn))
        sc_s[sc_idx] = qmax

    @pl.when(i == 0)
    def _():
        prep_q(q_ref, cq_ref, sq_ref, q_cur, qnorm_cur, 1)
        kw = kw_ref[...]
        kc = (_rms_rows(kc_ref[...].astype(F32)) * kw).astype(BF16)
        kl = _rope(_rms_rows(kl_ref[...].astype(F32)) * kw, ck_ref[...], sk_ref[...], half).astype(BF16)
        k_s[0:lc, :] = kc
        k_s[lc:, :] = kl
        v_s[0:lc, 0:hd] = vc_ref[...]
        v_s[lc:, 0:hd] = vl_ref[...]
        v_s[:, hd:] = jnp.ones((s_len, hd), BF16)
        kc32 = kc.astype(F32)
        kl32 = kl.astype(F32)
        kn_c = jnp.max(jnp.sqrt(jnp.sum(kc32 * kc32, axis=-1, keepdims=True)))
        kn_l = jnp.max(jnp.sqrt(jnp.sum(kl32 * kl32, axis=-1, keepdims=True)))
        sc_s[0] = jnp.maximum(kn_c, kn_l)

    kmax = sc_s[0]
    safe = 2.0 * sc_s[1] * kmax < ATT_SAFE_SPREAD

    @pl.when(safe)
    def _():
        prep_q(qn_ref, cqn_ref, sqn_ref, q_nxt, qnorm_nxt, 2)
        q_all = q_cur[...]
        bound = qnorm_cur[...] * kmax
        for c0 in range(0, s_len, ATT_KEY_CHUNK):
            n = min(ATT_KEY_CHUNK, s_len - c0)
            s = lax.dot_general(q_all, k_s[c0:c0 + n, :], (((1,), (1,)), ((), ())), preferred_element_type=F32)
            p_s[:, c0:c0 + n] = jnp.exp2(s - jnp.concatenate([bound] * (n // hd), axis=1)).astype(BF16)
        acc = jnp.dot(p_s[...], v_s[...], preferred_element_type=F32)
        for g in range(ATT_GROUP):
            a = acc[g * tq:(g + 1) * tq, :]
            o_ref[:, g * hd:(g + 1) * hd] = (a[:, 0:hd] / a[:, hd:]).astype(o_ref.dtype)

    @pl.when(jnp.logical_not(safe))
    def _():
        prep_q(qn_ref, cqn_ref, sqn_ref, q_nxt, qnorm_nxt, 2)
        k_all = k_s[...]
        v_all = v_s[:, 0:hd]
        for g in range(ATT_GROUP):
            s = lax.dot_general(q_cur[g * tq:(g + 1) * tq, :], k_all, (((1,), (1,)), ((), ())),
                                preferred_element_type=F32)
            p = jnp.exp2(s - jnp.max(s, axis=-1, keepdims=True))
            denom = jnp.sum(p, axis=-1, keepdims=True)
            o = jnp.dot(p.astype(BF16), v_all, preferred_element_type=F32)
            o_ref[:, g * hd:(g + 1) * hd] = (o / denom).astype(o_ref.dtype)

    q_cur[...] = q_nxt[...]
    qnorm_cur[...] = qnorm_nxt[...]
    sc_s[1] = sc_s[2]


def _attn_ctx_kernel(q_ref, kc_ref, vc_ref, qw_ref, kw_ref, o_ref):
    hd = ATT_HEAD_DIM
    k = (_rms_rows(kc_ref[...].astype(F32)) * kw_ref[...]).astype(BF16)
    qw = qw_ref[...] * (hd ** -0.5)
    v_c = vc_ref[...]
    for g in range(ATT_GROUP):
        q = (_rms_rows(q_ref[:, g * hd:(g + 1) * hd].astype(F32)) * qw).astype(BF16)
        s = lax.dot_general(q, k, (((1,), (1,)), ((), ())), preferred_element_type=F32)
        m = jnp.max(s, axis=-1, keepdims=True)
        p = jnp.exp(s - m)
        denom = jnp.sum(p, axis=-1, keepdims=True)
        o = jnp.dot(p.astype(BF16), v_c, preferred_element_type=F32)
        o_ref[:, g * hd:(g + 1) * hd] = (o / denom).astype(o_ref.dtype)


def _attn_lat(p_ctx, p_lat, cos, sin, q_norm_w, k_norm_w, n_batch, lc, ll):
    hd = ATT_HEAD_DIM
    gw = ATT_GROUP * hd
    tq = ATT_TQ
    nq = ll // tq
    q0 = COL_AQ // gw
    k0 = COL_AK // hd
    v0 = COL_AV // hd
    kern = functools.partial(_attn_lat_kernel, lc=lc)

    def nxt(i):
        return jnp.minimum(i + 1, nq - 1)

    return pl.pallas_call(
        kern,
        grid=(n_batch, ATT_KV_HEADS, nq),
        in_specs=[pl.BlockSpec((tq, gw), lambda b, h, i: (b * nq + i, q0 + h)),
                  pl.BlockSpec((tq, gw), lambda b, h, i: (b * nq + nxt(i), q0 + h)),
                  pl.BlockSpec((lc, hd), lambda b, h, i: (b, k0 + h)),
                  pl.BlockSpec((lc, hd), lambda b, h, i: (b, v0 + h)),
                  pl.BlockSpec((ll, hd), lambda b, h, i: (b, k0 + h)),
                  pl.BlockSpec((ll, hd), lambda b, h, i: (b, v0 + h)),
                  pl.BlockSpec((tq, hd), lambda b, h, i: (i, 0)),
                  pl.BlockSpec((tq, hd), lambda b, h, i: (i, 0)),
                  pl.BlockSpec((tq, hd), lambda b, h, i: (nxt(i), 0)),
                  pl.BlockSpec((tq, hd), lambda b, h, i: (nxt(i), 0)),
                  pl.BlockSpec((ll, hd), lambda b, h, i: (0, 0), pipeline_mode=pl.Buffered(1)),
                  pl.BlockSpec((ll, hd), lambda b, h, i: (0, 0), pipeline_mode=pl.Buffered(1)),
                  pl.BlockSpec((1, hd), lambda b, h, i: (0, 0)),
                  pl.BlockSpec((1, hd), lambda b, h, i: (0, 0))],
        out_specs=pl.BlockSpec((tq, gw), lambda b, h, i: (b * nq + i, h)),
        out_shape=jax.ShapeDtypeStruct((n_batch * ll, ATT_HEADS * hd), BF16),
        scratch_shapes=[pltpu.VMEM((lc + ll, hd), BF16),
                        pltpu.VMEM((lc + ll, 2 * hd), BF16),
                        pltpu.VMEM((ATT_GROUP * tq, hd), BF16), pltpu.VMEM((ATT_GROUP * tq, hd), BF16),
                        pltpu.VMEM((ATT_GROUP * tq, hd), F32), pltpu.VMEM((ATT_GROUP * tq, hd), F32),
                        pltpu.VMEM((ATT_GROUP * tq, lc + ll), BF16),
                        pltpu.SMEM((3,), F32)],
        compiler_params=_cparams(3),
        name="attn_lat",
    )(p_lat, p_lat, p_ctx, p_ctx, p_lat, p_lat, cos, sin, cos, sin, cos, sin, q_norm_w, k_norm_w)


def _attn_ctx(p_ctx, q_norm_w, k_norm_w, n_batch, lc):
    hd = ATT_HEAD_DIM
    gw = ATT_GROUP * hd
    q0 = COL_AQ // gw
    k0 = COL_AK // hd
    v0 = COL_AV // hd
    return pl.pallas_call(
        _attn_ctx_kernel,
        grid=(n_batch, ATT_KV_HEADS),
        in_specs=[pl.BlockSpec((lc, gw), lambda b, h: (b, q0 + h)),
                  pl.BlockSpec((lc, hd), lambda b, h: (b, k0 + h)),
                  pl.BlockSpec((lc, hd), lambda b, h: (b, v0 + h)),
                  pl.BlockSpec((1, hd), lambda b, h: (0, 0)),
                  pl.BlockSpec((1, hd), lambda b, h: (0, 0))],
        out_specs=pl.BlockSpec((lc, gw), lambda b, h: (b, h)),
        out_shape=jax.ShapeDtypeStruct((n_batch * lc, ATT_HEADS * hd), BF16),
        compiler_params=_cparams(2),
        name="attn_ctx",
    )(p_ctx, p_ctx, p_ctx, q_norm_w, k_norm_w)


def _ret_kernel(qc_ref, kc_ref, vc_ref, gc_ref, ql_ref, kl_ref, vl_ref, gl_ref, rcos_ref, rsin_ref, ccos_ref,
                csin_ref, dec_ref, nw_ref, yc_ref, yl_ref,
                qb_s, kb_s, qf_s, qr_s, kf_s, sbs_s, sf_s, sb_s, o_a, o_b, *, lc, ll):
    c = RET_CHUNK
    dk = RET_QK_DIM
    dv = RET_V_DIM
    half = dk // 4
    n_c = lc // c
    n_l = ll // c
    rows_per_chunk = c // GRID_W
    dec = dec_ref[...]
    log_g = jnp.minimum(dec, 0.0) - jnp.log(1.0 + jnp.exp(-jnp.abs(dec)))
    lg_f = log_g[0:1, :]
    lg_b = log_g[1:2, :]
    pos = lax.broadcasted_iota(jnp.int32, (c, dk), 0).astype(F32)
    qd_f = jnp.exp((pos + 1.0) * lg_f)
    kd_f = jnp.exp((c - 1.0 - pos) * lg_f)
    cd_f = jnp.exp(float(c) * lg_f)
    qd_b = jnp.exp((c - pos) * lg_b)
    kd_b = jnp.exp(pos * lg_b)
    cd_b = jnp.exp(float(c) * lg_b)
    ii = lax.broadcasted_iota(jnp.int32, (c, c), 0)
    jj = lax.broadcasted_iota(jnp.int32, (c, c), 1)
    diff = (ii - jj).astype(F32)
    intra = jnp.where(ii >= jj, jnp.exp(jnp.maximum(diff, 0.0) * lg_f[:, 0:c]),
                      jnp.exp(jnp.maximum(-diff, 0.0) * lg_b[:, 0:c]))
    nw = nw_ref[...]
    k_scale = dk ** -0.5
    col_cos = jnp.concatenate([ccos_ref[...]] * rows_per_chunk, axis=0)
    col_sin = jnp.concatenate([csin_ref[...]] * rows_per_chunk, axis=0)

    def rope_tables(ci):
        def row_part(ref):
            return jnp.concatenate(
                [jnp.broadcast_to(ref[pl.ds(ci * rows_per_chunk + r, 1), :], (GRID_W, dk // 2))
                 for r in range(rows_per_chunk)], axis=0)
        return (jnp.concatenate([row_part(rcos_ref), col_cos], axis=1),
                jnp.concatenate([row_part(rsin_ref), col_sin], axis=1))

    def state_update(s_ref, k_dec, v, c_dec):
        kv = lax.dot_general(k_dec, v, (((0,), (0,)), ((), ())), preferred_element_type=F32)
        s_ref[...] = s_ref[...] * c_dec + kv

    def sweep1_chunk(q, k, v, g, orows):
        qb_s[orows, :] = q.astype(BF16)
        kb_s[orows, :] = k.astype(BF16)
        qf_s[orows, :] = (q * qd_f).astype(BF16)
        qr_s[orows, :] = (q * qd_b).astype(BF16)
        kf_s[orows, :] = (k * kd_f).astype(BF16)
        sbs_s[g] = sb_s[...].astype(BF16)
        state_update(sb_s, (k * kd_b).astype(BF16), v, cd_b)

    sf_s[...] = jnp.zeros((dk, dv), F32)
    sb_s[...] = jnp.zeros((dk, dv), F32)

    for ci in reversed(range(n_c)):
        rows = pl.ds(ci * c, c)
        sweep1_chunk(qc_ref[rows, :].astype(F32), kc_ref[rows, :].astype(F32) * k_scale, vc_ref[rows, :], ci, rows)

    def sweep1_body(i, carry):
        ci = n_l - 1 - i
        rows = pl.ds(pl.multiple_of(ci * c, c), c)
        orows = pl.ds(pl.multiple_of(lc + ci * c, c), c)
        cos, sin = rope_tables(ci)
        q = _rope(ql_ref[rows, :].astype(F32), cos, sin, half)
        k = _rope(kl_ref[rows, :].astype(F32), cos, sin, half) * k_scale
        sweep1_chunk(q, k, vl_ref[rows, :], n_c + ci, orows)
        return carry

    lax.fori_loop(0, n_l, sweep1_body, 0, unroll=4)

    def raw_out(g, orows, v):
        att = lax.dot_general(qb_s[orows, :], kb_s[orows, :], (((1,), (1,)), ((), ())),
                              preferred_element_type=F32) * intra
        o = (jnp.dot(att.astype(BF16), v, preferred_element_type=F32)
             + jnp.dot(qf_s[orows, :], sf_s[...].astype(BF16), preferred_element_type=F32)
             + jnp.dot(qr_s[orows, :], sbs_s[g], preferred_element_type=F32))
        state_update(sf_s, kf_s[orows, :], v, cd_f)
        return o

    def finish(o, gate):
        mu = jnp.mean(o, axis=-1, keepdims=True)
        oc = o - mu
        var = jnp.mean(oc * oc, axis=-1, keepdims=True)
        y = oc * lax.rsqrt(var + EPS) * nw
        return _silu(gate) * y

    for ci in range(n_c):
        rows = pl.ds(ci * c, c)
        yc_ref[rows, :] = finish(raw_out(ci, rows, vc_ref[rows, :]), gc_ref[rows, :].astype(F32)).astype(yc_ref.dtype)

    gsz = RET_GROUP
    n_groups = n_l // gsz

    def group_matmuls(grp, o_dst):
        for j in range(gsz):
            ci = grp * gsz + j
            rows = pl.ds(pl.multiple_of(ci * c, c), c)
            orows = pl.ds(pl.multiple_of(lc + ci * c, c), c)
            o_dst[j * c:(j + 1) * c, :] = raw_out(n_c + ci, orows, vl_ref[rows, :])

    def group_finish(grp, o_src):
        for j in range(gsz):
            ci = grp * gsz + j
            rows = pl.ds(pl.multiple_of(ci * c, c), c)
            yl_ref[rows, :] = finish(o_src[j * c:(j + 1) * c, :], gl_ref[rows, :].astype(F32)).astype(yl_ref.dtype)

    group_matmuls(0, o_a)

    def pair_body(p, carry):
        group_matmuls(2 * p + 1, o_b)
        group_finish(2 * p, o_a)
        group_matmuls(2 * p + 2, o_a)
        group_finish(2 * p + 1, o_b)
        return carry

    lax.fori_loop(0, n_groups // 2 - 1, pair_body, 0)
    group_matmuls(n_groups - 1, o_b)
    group_finish(n_groups - 2, o_a)
    group_finish(n_groups - 1, o_b)


def _retention(p_ctx, p_lat, cos, sin, dec, norm_w, n_batch, lc, ll):
    dk = RET_QK_DIM
    q0 = COL_RQ // dk
    k0 = COL_RK // dk
    v0 = COL_RV // dk
    g0 = COL_RGT // dk
    n_chunks = (lc + ll) // RET_CHUNK
    assert RET_CHUNK % GRID_W == 0 and ll % (2 * RET_GROUP * RET_CHUNK) == 0
    row_cos, row_sin = cos[::GRID_W, :dk // 2], sin[::GRID_W, :dk // 2]
    col_cos, col_sin = cos[:GRID_W, dk // 2:], sin[:GRID_W, dk // 2:]
    kern = functools.partial(_ret_kernel, lc=lc, ll=ll)

    def col(off):
        return lambda b, h: (b, off + h)

    const = lambda b, h: (0, 0)
    return pl.pallas_call(
        kern,
        grid=(n_batch, RET_HEADS),
        in_specs=[pl.BlockSpec((lc, dk), col(q0)), pl.BlockSpec((lc, dk), col(k0)),
                  pl.BlockSpec((lc, dk), col(v0)), pl.BlockSpec((lc, dk), col(g0)),
                  pl.BlockSpec((ll, dk), col(q0)), pl.BlockSpec((ll, dk), col(k0)),
                  pl.BlockSpec((ll, dk), col(v0)), pl.BlockSpec((ll, dk), col(g0)),
                  pl.BlockSpec(row_cos.shape, const), pl.BlockSpec(row_sin.shape, const),
                  pl.BlockSpec(col_cos.shape, const), pl.BlockSpec(col_sin.shape, const),
                  pl.BlockSpec((None, 2, dk), lambda b, h: (h, 0, 0)),
                  pl.BlockSpec((1, dk), lambda b, h: (0, h))],
        out_specs=[pl.BlockSpec((lc, dk), lambda b, h: (b, h)),
                   pl.BlockSpec((ll, dk), lambda b, h: (b, h))],
        out_shape=[jax.ShapeDtypeStruct((n_batch * lc, RET_HEADS * RET_V_DIM), BF16),
                   jax.ShapeDtypeStruct((n_batch * ll, RET_HEADS * RET_V_DIM), BF16)],
        scratch_shapes=[pltpu.VMEM((lc + ll, dk), BF16)] * 5
        + [pltpu.VMEM((n_chunks, dk, RET_V_DIM), BF16)]
        + [pltpu.VMEM((dk, RET_V_DIM), F32)] * 2
        + [pltpu.VMEM((RET_GROUP * RET_CHUNK, RET_V_DIM), F32)] * 2,
        compiler_params=_cparams(2),
        name="retention",
    )(p_ctx, p_ctx, p_ctx, p_ctx, p_lat, p_lat, p_lat, p_lat, row_cos, row_sin, col_cos, col_sin, dec, norm_w)


def _merge_kernel(x_ref, yr_ref, ya_ref, yt_ref, g0_ref, g1_ref, g2_ref, wb_ref, wo_ref, m_ref, nw_ref,
                  xo_ref, vo_ref):
    m = (_sigmoid(g0_ref[...].astype(F32)) * jnp.dot(yr_ref[...], wb_ref[0], preferred_element_type=F32)
         + _sigmoid(g1_ref[...].astype(F32)) * jnp.dot(ya_ref[...], wb_ref[1], preferred_element_type=F32)
         + _sigmoid(g2_ref[...].astype(F32)) * jnp.dot(yt_ref[...], wb_ref[2], preferred_element_type=F32))
    out = jnp.dot(m.astype(BF16), wo_ref[...], preferred_element_type=F32)
    x1 = x_ref[...] + m_ref[2:3, :] * out
    xo_ref[...] = x1
    v = _rms_rows(x1) * nw_ref[...]
    vo_ref[...] = (v * (1.0 + m_ref[4:5, :]) + m_ref[3:4, :]).astype(vo_ref.dtype)


def _merge(x2d, y_rnn, y_att, y_ret, p, mods, mod_row, norm_w, w_branch, w_out, tm):
    n_tok, d = x2d.shape
    bg0 = COL_BG // d
    tok = lambda i: (i, 0)
    const2 = lambda i: (0, 0)
    return pl.pallas_call(
        _merge_kernel,
        grid=(n_tok // tm,),
        in_specs=[pl.BlockSpec((tm, d), tok), pl.BlockSpec((tm, d), tok), pl.BlockSpec((tm, d), tok),
                  pl.BlockSpec((tm, d), tok),
                  pl.BlockSpec((tm, d), lambda i: (i, bg0)),
                  pl.BlockSpec((tm, d), lambda i: (i, bg0 + 1)),
                  pl.BlockSpec((tm, d), lambda i: (i, bg0 + 2)),
                  pl.BlockSpec((N_BRANCH, d, d), lambda i: (0, 0, 0)),
                  pl.BlockSpec((d, d), const2),
                  pl.BlockSpec((None, 6, d), lambda i: (mod_row(i), 0, 0)),
                  pl.BlockSpec((1, d), const2)],
        out_specs=[pl.BlockSpec((tm, d), tok), pl.BlockSpec((tm, d), tok)],
        out_shape=[jax.ShapeDtypeStruct((n_tok, d), F32), jax.ShapeDtypeStruct((n_tok, d), BF16)],
        compiler_params=_cparams(1),
        name="merge",
    )(x2d, y_rnn, y_att, y_ret, p, p, p, w_branch, w_out, mods, norm_w.reshape(1, d))


FFN_CHUNK = 256
FFN_HALO = 16


def _ffn_kernel(v_ref, vp_ref, vn_ref, wu_ref, cw_ref, cb_ref, wd_ref, x_ref, m_ref, fw_ref, o_ref,
                vext_s, a_s, h_s, *, tiles_per_seq, final):
    tm = v_ref.shape[0]
    f = h_s.shape[1]
    fc = FFN_CHUNK
    n_slab = fc // LANES
    ti = pl.program_id(0) % tiles_per_seq
    vext_s[0:FFN_HALO, :] = vp_ref[...]
    vext_s[FFN_HALO:FFN_HALO + tm, :] = v_ref[...]
    vext_s[FFN_HALO + tm:, :] = vn_ref[...]
    v_ext = vext_s[...]
    v_main = v_ref[...]
    first = ti == 0
    last = ti == tiles_per_seq - 1
    row0 = jnp.minimum(pl.program_id(0), 0)
    cw = cw_ref[...] * 0.5
    cb = cb_ref[...] * 0.5
    for c in range(f // fc):
        cols = slice(c * fc, (c + 1) * fc)
        a_ext = jnp.dot(v_ext, wu_ref[:, cols], preferred_element_type=F32)
        b = jnp.dot(v_main, wu_ref[:, f + c * fc:f + (c + 1) * fc], preferred_element_type=F32)
        taps = [[], [], []]
        for j in range(n_slab):
            slab = (c % 2) * n_slab + j
            lanes = slice(j * LANES, (j + 1) * LANES)
            a_s[slab] = a_ext[:, lanes]
            a_s[slab, FFN_HALO - 1:FFN_HALO, :] = jnp.where(first, 0.0, a_ext[FFN_HALO - 1:FFN_HALO, lanes])
            a_s[slab, FFN_HALO + tm:FFN_HALO + tm + 1, :] = jnp.where(
                last, 0.0, a_ext[FFN_HALO + tm:FFN_HALO + tm + 1, lanes])
            for k in range(FFN_CONV_W):
                taps[k].append(a_s[slab, pl.ds(row0 + (FFN_HALO - 1 + k), tm), :])
        a_m1, a_0, a_p1 = (jnp.concatenate(t, axis=1) for t in taps)
        half_conv = cw[0:1, cols] * a_m1 + cw[1:2, cols] * a_0 + cw[2:3, cols] * a_p1 + cb[:, cols]
        h_s[:, cols] = ((half_conv * (jnp.tanh(half_conv) + 1.0)) * b).astype(BF16)
    out = jnp.dot(h_s[...], wd_ref[...], preferred_element_type=F32)
    x2 = x_ref[...] + m_ref[5:6, :] * out
    if final:
        x2 = _rms_rows(x2) * fw_ref[...]
    o_ref[...] = x2


def _ffn(v2d, w_up, conv_w, conv_b, w_down, x2d, mods, mod_row, final_w, tm, seq_len, final):
    n_tok, d = x2d.shape
    f = D_FF
    tiles_per_seq = seq_len // tm
    hb = tm // FFN_HALO
    n_hblk = n_tok // FFN_HALO
    assert f % FFN_CHUNK == 0 and tm % FFN_HALO == 0
    kern = functools.partial(_ffn_kernel, tiles_per_seq=tiles_per_seq, final=final)
    resident = dict(pipeline_mode=pl.Buffered(1))
    return pl.pallas_call(
        kern,
        grid=(n_tok // tm,),
        in_specs=[pl.BlockSpec((tm, d), lambda i: (i, 0)),
                  pl.BlockSpec((FFN_HALO, d), lambda i: (jnp.maximum(i * hb - 1, 0), 0)),
                  pl.BlockSpec((FFN_HALO, d), lambda i: (jnp.minimum((i + 1) * hb, n_hblk - 1), 0)),
                  pl.BlockSpec((d, 2 * f), lambda i: (0, 0), **resident),
                  pl.BlockSpec((FFN_CONV_W, f), lambda i: (0, 0)),
                  pl.BlockSpec((1, f), lambda i: (0, 0)),
                  pl.BlockSpec((f, d), lambda i: (0, 0), **resident),
                  pl.BlockSpec((tm, d), lambda i: (i, 0)),
                  pl.BlockSpec((None, 6, d), lambda i: (mod_row(i), 0, 0)),
                  pl.BlockSpec((1, d), lambda i: (0, 0))],
        out_specs=pl.BlockSpec((tm, d), lambda i: (i, 0)),
        out_shape=jax.ShapeDtypeStruct((n_tok, d), F32),
        scratch_shapes=[pltpu.VMEM((tm + 2 * FFN_HALO, d), BF16),
                        pltpu.VMEM((2 * FFN_CHUNK // LANES, tm + 2 * FFN_HALO, LANES), F32),
                        pltpu.VMEM((tm, f), BF16)],
        compiler_params=_cparams(1),
        name="ffn",
    )(v2d, v2d, v2d, w_up, conv_w, conv_b.reshape(1, f), w_down, x2d, mods, final_w.reshape(1, d))


def _rope_tables(seq_len, head_dim):
    n_freq = head_dim // 4
    t = np.arange(seq_len)
    row = (t // GRID_W).astype(np.float32)
    col = (t % GRID_W).astype(np.float32)
    inv = (ROPE_THETA ** (-np.arange(n_freq, dtype=np.float32) / n_freq)).astype(np.float32)
    ang = np.stack([row[:, None] * inv, col[:, None] * inv], axis=1).astype(np.float64)
    cos = np.cos(ang)
    sin = np.sin(ang)
    cos_t = np.stack([cos, cos], axis=2).reshape(seq_len, head_dim).astype(np.float32)
    sin_t = np.stack([-sin, sin], axis=2).reshape(seq_len, head_dim).astype(np.float32)
    return jnp.asarray(cos_t), jnp.asarray(sin_t)


def kernel(x, c, ctx, c_ctx, mod_w, mod_b, norm1_w, norm2_w, w_in, rnn_conv_w, rnn_conv_b, rglru_w, rglru_b,
           rglru_lam, q_norm_w, k_norm_w, ret_decay, ret_norm_w, w_branch, w_out, ffn_up, ffn_conv_w,
           ffn_conv_b, ffn_down, final_norm_w):
    n_batch, ll, d = x.shape
    lc = ctx.shape[1]
    depth = mod_w.shape[0]
    assert d == D_MODEL and n_batch + 1 <= MOD_ROWS
    assert ll % 1024 == 0 and lc % RNN_CHUNK == 0 and lc % RET_CHUNK == 0

    cos_a, sin_a = _rope_tables(ll, ATT_HEAD_DIM)
    cos_r, sin_r = _rope_tables(ll, RET_QK_DIM)

    c_all = jnp.zeros((MOD_ROWS, d), F32).at[:n_batch].set(c).at[n_batch].set(c_ctx)
    mods = _modulation(c_all, mod_w, mod_b).reshape(depth, MOD_ROWS, 6, d)

    tm_lat = 1024
    tm_ctx = lc
    tm_ctx_big = next(t for t in (1024, 512, 256, lc) if (n_batch * lc) % t == 0)
    tm_ctx_mix = min(tm_ctx_big, 512)
    lat_tiles = ll // tm_lat
    lat_row_big = lambda i: i // lat_tiles
    ctx_row = lambda i: n_batch
    tm_mix = 512
    lat_row_mix = lambda i: i // (ll // tm_mix)

    x2 = x.reshape(n_batch * ll, d)
    cx2 = ctx.reshape(n_batch * lc, d)

    for l in range(depth):
        last = l == depth - 1
        w_in_p = jnp.concatenate([w_in[l][:, s:s + n] for s, n in _REF_SEGMENTS], axis=1).astype(BF16)
        gate_w = jnp.transpose(rglru_w[l], (2, 3, 0, 1, 4)).reshape(RNN_BLOCKS, RNN_BLOCK_W, 4 * RNN_BLOCK_W)
        gate_w = gate_w.astype(BF16)
        gate_b = jnp.transpose(rglru_b[l].reshape(2, 2, RNN_BLOCKS, RNN_BLOCK_W), (2, 0, 1, 3))
        gate_b = gate_b.reshape(RNN_BLOCKS, 1, 4 * RNN_BLOCK_W)
        dec = jnp.broadcast_to(jnp.transpose(ret_decay[l])[:, :, None], (RET_HEADS, 2, RET_QK_DIM))
        wb = w_branch[l].astype(BF16)
        wo = w_out[l].astype(BF16)
        w_up = ffn_up[l].astype(BF16)
        w_dn = ffn_down[l].astype(BF16)
        ml = mods[l]

        p_lat = _inproj(x2, ml, lat_row_big, norm1_w[l], w_in_p, tm_lat)
        p_ctx = _inproj(cx2, ml, ctx_row, norm1_w[l], w_in_p, tm_ctx_big)

        yr_c, yr_l = _rglru(p_ctx, p_lat, rnn_conv_w[l], rnn_conv_b[l].reshape(1, D_RNN), gate_w, gate_b,
                            rglru_lam[l], n_batch, lc, ll)
        ya_l = _attn_lat(p_ctx, p_lat, cos_a, sin_a, q_norm_w[l].reshape(1, -1), k_norm_w[l].reshape(1, -1),
                         n_batch, lc, ll)
        yt_c, yt_l = _retention(p_ctx, p_lat, cos_r, sin_r, dec, ret_norm_w[l].reshape(1, -1), n_batch, lc, ll)

        x2, v_lat = _merge(x2, yr_l, ya_l, yt_l, p_lat, ml, lat_row_mix, norm2_w[l], wb, wo, tm_mix)
        x2 = _ffn(v_lat, w_up, ffn_conv_w[l], ffn_conv_b[l], w_dn, x2, ml, lat_row_mix, final_norm_w,
                  tm_mix, ll, final=last)

        if not last:
            ya_c = _attn_ctx(p_ctx, q_norm_w[l].reshape(1, -1), k_norm_w[l].reshape(1, -1), n_batch, lc)
            cx2, v_ctx = _merge(cx2, yr_c, ya_c, yt_c, p_ctx, ml, ctx_row, norm2_w[l], wb, wo, tm_ctx_mix)
            cx2 = _ffn(v_ctx, w_up, ffn_conv_w[l], ffn_conv_b[l], w_dn, cx2, ml, ctx_row, final_norm_w,
                       tm_ctx, lc, final=False)

    return x2.reshape(n_batch, ll, d)
```
